```python
import math
import jax
import jax.numpy as jnp
from jax import lax
import numpy as np

D_MODEL = 1024
BATCH = 16
SEQ = 2048
DEPTH = 2

GRID_W = 64
CTX_LEN = 256
N_MOD = 6
NORM_EPS = 1e-6
NEG_INF = -1e30
ROPE_BASE = 10000.0
Q_BLOCK = 128
A_HEAD_DIM = 64
A_HEADS = D_MODEL // (2 * A_HEAD_DIM)
A_WIDTH = A_HEADS * A_HEAD_DIM
WIN_ROWS_MAX = 8
WIN_COLS = 16
B_WIDTH = D_MODEL // 2
B_CONV = 31
C_WIDTH = D_MODEL // 2
C_CONV = 3
D_QK_DIM = 64
D_V_DIM = 2 * D_QK_DIM
D_HEADS = D_MODEL // (4 * D_QK_DIM)
D_QK_COLS = D_HEADS * 2 * D_QK_DIM
D_WIDTH = D_HEADS * D_V_DIM
EVEN_IN = 3 * A_WIDTH + 2 * B_WIDTH
EVEN_OUT = A_WIDTH + B_WIDTH
ODD_IN = 3 * C_WIDTH + 2 * D_QK_COLS + D_WIDTH
ODD_OUT = C_WIDTH + D_WIDTH
N_GROUPS = 4
EXPERTS_PER_GROUP = 8
N_EXPERTS = N_GROUPS * EXPERTS_PER_GROUP
TOP_K = 2
D_EXPERT = D_MODEL // 4
N_EVEN = (DEPTH + 1) // 2
N_ODD = DEPTH // 2

kernel_name = "hybrid_natten_conformer_shortconv_diffattn_hmoe_dit"


def rmsnorm(x, g):
    x32 = x.astype(jnp.float32)
    y = x32 * lax.rsqrt(jnp.mean(x32 * x32, axis=-1, keepdims=True) + NORM_EPS)
    return (y * g.astype(jnp.float32)).astype(x.dtype)


def layernorm(x, g, b):
    x32 = x.astype(jnp.float32)
    mu = jnp.mean(x32, axis=-1, keepdims=True)
    var = jnp.mean(jnp.square(x32 - mu), axis=-1, keepdims=True)
    y = (x32 - mu) * lax.rsqrt(var + NORM_EPS) * g.astype(jnp.float32) + b.astype(jnp.float32)
    return y.astype(x.dtype)


def depthwise_conv(x, w):
    k, ch = w.shape
    pad = k // 2
    return lax.conv_general_dilated(
        x, w[:, None, :].astype(x.dtype), window_strides=(1,), padding=[(pad, pad)],
        dimension_numbers=('NWC', 'WIO', 'NWC'), feature_group_count=ch)


def axial_rope_tables(n_tok):
    quarter = D_QK_DIM // 4
    t = jnp.arange(n_tok, dtype=jnp.int32)
    row = (t // GRID_W).astype(jnp.float32)
    col = (t % GRID_W).astype(jnp.float32)
    inv = jnp.power(ROPE_BASE, -jnp.arange(quarter, dtype=jnp.float32) / quarter)
    ang = jnp.concatenate([row[:, None] * inv, col[:, None] * inv], axis=-1)
    return jnp.cos(ang), jnp.sin(ang)


def apply_axial_rope(x, cos, sin):
    quarter = x.shape[-1] // 4
    x32 = x.astype(jnp.float32)
    cos = cos[:, None, None, :]
    sin = sin[:, None, None, :]
    parts = []
    for axis in range(2):
        seg = x32[..., 2 * axis * quarter:(2 * axis + 2) * quarter]
        x1, x2 = seg[..., :quarter], seg[..., quarter:]
        ca = cos[..., axis * quarter:(axis + 1) * quarter]
        sa = sin[..., axis * quarter:(axis + 1) * quarter]
        parts += [x1 * ca - x2 * sa, x2 * ca + x1 * sa]
    return jnp.concatenate(parts, axis=-1).astype(x.dtype)


def context_attention(q, k, v):
    s = jnp.einsum('bqhd,bkhd->bhqk', q, k).astype(jnp.float32)
    p = jax.nn.softmax(s, axis=-1).astype(v.dtype)
    return jnp.einsum('bhqk,bkhd->bqhd', p, v)


def neighbourhood_attention(q, k, v, kc, vc, rpb):
    bsz, n_tok, heads, hd = q.shape
    rows = n_tok // GRID_W
    win_r = min(WIN_ROWS_MAX, rows)
    qg = (q * hd ** -0.5).reshape(bsz, rows, GRID_W, heads, hd)
    kg = k.reshape(bsz, rows, GRID_W, heads, hd)
    vg = v.reshape(bsz, rows, GRID_W, heads, hd)
    col = np.arange(GRID_W)
    c_start = np.clip(col - WIN_COLS // 2, 0, GRID_W - WIN_COLS)
    col_ok = (col[None, :] >= c_start[:, None]) & (col[None, :] < c_start[:, None] + WIN_COLS)
    dc = np.clip(col[None, :] - col[:, None], -(WIN_COLS - 1), WIN_COLS - 1) + WIN_COLS - 1

    def one_row(r):
        r_start = jnp.clip(r - win_r // 2, 0, rows - win_r)
        q_r = lax.dynamic_index_in_dim(qg, r, axis=1, keepdims=False)
        k_b = lax.dynamic_slice_in_dim(kg, r_start, win_r, axis=1)
        v_b = lax.dynamic_slice_in_dim(vg, r_start, win_r, axis=1)
        dr = r_start + jnp.arange(win_r) - r + WIN_ROWS_MAX - 1
        bias = rpb[:, dr[:, None, None], dc[None, :, :]]
        s_loc = jnp.einsum('bqhd,bwkhd->bhqwk', q_r, k_b).astype(jnp.float32)
        s_loc = s_loc + bias.transpose(0, 2, 1, 3)[None].astype(jnp.float32)
        s_loc = jnp.where(col_ok[None, None, :, None, :], s_loc, NEG_INF)
        s_ctx = jnp.einsum('bqhd,bchd->bhqc', q_r, kc).astype(jnp.float32)
        s = jnp.concatenate([s_loc.reshape(bsz, heads, GRID_W, win_r * GRID_W), s_ctx], axis=-1)
        p = jax.nn.softmax(s, axis=-1).astype(v.dtype)
        p_loc = p[..., :win_r * GRID_W].reshape(bsz, heads, GRID_W, win_r, GRID_W)
        return (jnp.einsum('bhqwk,bwkhd->bqhd', p_loc, v_b)
                + jnp.einsum('bhqc,bchd->bqhd', p[..., win_r * GRID_W:], vc))

    out = lax.map(one_row, jnp.arange(rows))
    return out.transpose(1, 0, 2, 3, 4).reshape(bsz, n_tok, heads * hd)


def diff_weights(s, lam):
    p = jax.nn.softmax(s, axis=-1)
    return p[:, :, 0] - lam * p[:, :, 1]


def diff_latent_attention(q, k, v, kc, vc, lam):
    bsz, n_tok, heads, _, hd = q.shape
    nb = n_tok // Q_BLOCK
    qb = q.reshape(bsz, nb, Q_BLOCK, heads, 2, hd).transpose(1, 0, 2, 3, 4, 5)

    def block(q_blk):
        s = jnp.concatenate([jnp.einsum('bqhmd,bkhmd->bhmqk', q_blk, k).astype(jnp.float32),
                             jnp.einsum('bqhmd,bchmd->bhmqc', q_blk, kc).astype(jnp.float32)], axis=-1)
        w = diff_weights(s, lam).astype(v.dtype)
        return (jnp.einsum('bhqk,bkhd->bqhd', w[..., :n_tok], v)
                + jnp.einsum('bhqc,bchd->bqhd', w[..., n_tok:], vc))

    out = lax.map(block, qb)
    return out.transpose(1, 0, 2, 3, 4).reshape(bsz, n_tok, heads, D_V_DIM)


def diff_context_attention(q, k, v, lam):
    s = jnp.einsum('bqhmd,bkhmd->bhmqk', q, k).astype(jnp.float32)
    w = diff_weights(s, lam).astype(v.dtype)
    return jnp.einsum('bhqk,bkhd->bqhd', w, v)


def even_mixer(hx, hc, w_in, w_out, rpb, dw, dw_b, ln_g, ln_b, ctx_out):
    def heads(t):
        return t.reshape(t.shape[0], t.shape[1], A_HEADS, A_HEAD_DIM)

    def conformer_conv(u):
        y = depthwise_conv(u, dw) + dw_b
        return jax.nn.silu(layernorm(y, ln_g, ln_b))

    px = hx @ w_in
    qx, kx, vx, ux, gx = jnp.split(px, [A_WIDTH, 2 * A_WIDTH, 3 * A_WIDTH, 3 * A_WIDTH + B_WIDTH], axis=-1)
    if ctx_out:
        pc = hc @ w_in
        qc, kc, vc, uc, gc = jnp.split(pc, [A_WIDTH, 2 * A_WIDTH, 3 * A_WIDTH, 3 * A_WIDTH + B_WIDTH], axis=-1)
    else:
        kc, vc = jnp.split(hc @ w_in[:, A_WIDTH:3 * A_WIDTH], 2, axis=-1)
    kc, vc = heads(kc), heads(vc)
    a_x = neighbourhood_attention(heads(qx), heads(kx), heads(vx), kc, vc, rpb)
    b_x = conformer_conv(ux * jax.nn.sigmoid(gx))
    y_x = jnp.concatenate([a_x, b_x], axis=-1) @ w_out
    if not ctx_out:
        return y_x, None
    a_c = context_attention(heads(qc) * A_HEAD_DIM ** -0.5, kc, vc).reshape(hc.shape[0], hc.shape[1], A_WIDTH)
    b_c = conformer_conv(uc * jax.nn.sigmoid(gc))
    y_c = jnp.concatenate([a_c, b_c], axis=-1) @ w_out
    return y_x, y_c


def odd_mixer(hx, hc, w_in, w_out, conv_w, lam_p, subln_g, lam_init, cos, sin, ctx_out):
    lam_p32 = lam_p.astype(jnp.float32)
    lam = (jnp.exp(jnp.sum(lam_p32[0] * lam_p32[1])) - jnp.exp(jnp.sum(lam_p32[2] * lam_p32[3]))
           + lam_init)
    scale = D_QK_DIM ** -0.5

    def qk_heads(t):
        return t.reshape(t.shape[0], t.shape[1], D_HEADS, 2, D_QK_DIM)

    def v_heads(t):
        return t.reshape(t.shape[0], t.shape[1], D_HEADS, D_V_DIM)

    def short_conv(b_gate, c_gate, u):
        return b_gate * depthwise_conv(c_gate * u, conv_w)

    def diff_out(o):
        return (rmsnorm(o, subln_g) * (1.0 - lam_init)).reshape(o.shape[0], o.shape[1], D_WIDTH)

    cuts = [C_WIDTH, 2 * C_WIDTH, 3 * C_WIDTH, 3 * C_WIDTH + D_QK_COLS, 3 * C_WIDTH + 2 * D_QK_COLS]
    cbx, ccx, cux, dqx, dkx, dvx = jnp.split(hx @ w_in, cuts, axis=-1)
    if ctx_out:
        cbc, ccc, cuc, dqc, dkc, dvc = jnp.split(hc @ w_in, cuts, axis=-1)
    else:
        dkc, dvc = jnp.split(hc @ w_in[:, 3 * C_WIDTH + D_QK_COLS:], [D_QK_COLS], axis=-1)
    kc, vc = qk_heads(dkc), v_heads(dvc)
    qx = apply_axial_rope(qk_heads(dqx) * scale, cos, sin)
    kx = apply_axial_rope(qk_heads(dkx), cos, sin)
    d_x = diff_out(diff_latent_attention(qx, kx, v_heads(dvx), kc, vc, lam))
    y_x = jnp.concatenate([short_conv(cbx, ccx, cux), d_x], axis=-1) @ w_out
    if not ctx_out:
        return y_x, None
    d_c = diff_out(diff_context_attention(qk_heads(dqc) * scale, kc, vc, lam))
    y_c = jnp.concatenate([short_conv(cbc, ccc, cuc), d_c], axis=-1) @ w_out
    return y_x, y_c


def hierarchical_moe(t, wg, bg, we, be, w1, w3, w2):
    n = t.shape[0]
    g_logit = (t @ wg + bg).astype(jnp.float32)
    g_prob = jax.nn.softmax(g_logit, axis=-1)
    g_sel = jnp.argmax(g_logit, axis=-1)
    g_w = jnp.take_along_axis(g_prob, g_sel[:, None], axis=-1)
    e_logit = (t @ we + be).astype(jnp.float32).reshape(n, N_GROUPS, EXPERTS_PER_GROUP)
    e_logit = jnp.take_along_axis(e_logit, g_sel[:, None, None], axis=1)[:, 0]
    top_p, top_i = lax.top_k(jax.nn.softmax(e_logit, axis=-1), TOP_K)
    top_w = top_p / jnp.sum(top_p, axis=-1, keepdims=True) * g_w
    expert_id = g_sel[:, None] * EXPERTS_PER_GROUP + top_i
    combine = jnp.sum(jax.nn.one_hot(expert_id, N_EXPERTS, dtype=jnp.float32) * top_w[..., None], axis=1)

    def add_expert(acc, p):
        w1_e, w3_e, w2_e, c_e = p
        hidden = jax.nn.silu(t @ w1_e) * (t @ w3_e)
        return acc + c_e[:, None] * (hidden @ w2_e), None

    y, _ = lax.scan(add_expert, jnp.zeros_like(t), (w1, w3, w2, combine.T.astype(t.dtype)))
    return y


def setup_inputs(seed: int = 0) -> dict:
    key = jax.random.key(seed)
    ks = iter(jax.random.split(key, 32))

    def nrm(shape, scale):
        return jax.random.normal(next(ks), shape, jnp.float32) * scale

    return {
        "x": nrm((BATCH, SEQ, D_MODEL), 1.0),
        "c": nrm((BATCH, D_MODEL), 1.0),
        "ctx": nrm((BATCH, CTX_LEN, D_MODEL), 1.0),
        "c_ctx": nrm((D_MODEL,), 1.0),
        "w_mod": nrm((DEPTH, D_MODEL, N_MOD * D_MODEL), D_MODEL ** -0.5),
        "b_mod": nrm((DEPTH, N_MOD * D_MODEL), 0.02),
        "norm_mix": 1.0 + nrm((DEPTH, D_MODEL), 0.05),
        "norm_ffn": 1.0 + nrm((DEPTH, D_MODEL), 0.05),
        "norm_final": 1.0 + nrm((D_MODEL,), 0.05),
        "even_w_in": nrm((N_EVEN, D_MODEL, EVEN_IN), D_MODEL ** -0.5),
        "even_w_out": nrm((N_EVEN, EVEN_OUT, D_MODEL), EVEN_OUT ** -0.5),
        "even_rpb": nrm((N_EVEN, A_HEADS, 2 * WIN_ROWS_MAX - 1, 2 * WIN_COLS - 1), 0.02),
        "even_dw": nrm((N_EVEN, B_CONV, B_WIDTH), B_CONV ** -0.5),
        "even_dw_b": nrm((N_EVEN, B_WIDTH), 0.01),
        "even_ln_g": 1.0 + nrm((N_EVEN, B_WIDTH), 0.05),
        "even_ln_b": nrm((N_EVEN, B_WIDTH), 0.01),
        "odd_w_in": nrm((N_ODD, D_MODEL, ODD_IN), D_MODEL ** -0.5),
        "odd_w_out": nrm((N_ODD, ODD_OUT, D_MODEL), ODD_OUT ** -0.5),
        "odd_conv": nrm((N_ODD, C_CONV, C_WIDTH), C_CONV ** -0.5),
        "odd_lambda": nrm((N_ODD, 4, D_QK_DIM), 0.1),
        "odd_subln": 1.0 + nrm((N_ODD, D_V_DIM), 0.05),
        "moe_wg": nrm((DEPTH, D_MODEL, N_GROUPS), D_MODEL ** -0.5),
        "moe_bg": nrm((DEPTH, N_GROUPS), 0.01),
        "moe_we": nrm((DEPTH, D_MODEL, N_EXPERTS), D_MODEL ** -0.5),
        "moe_be": nrm((DEPTH, N_EXPERTS), 0.01),
        "moe_w1": nrm((DEPTH, N_EXPERTS, D_MODEL, D_EXPERT), D_MODEL ** -0.5),
        "moe_w3": nrm((DEPTH, N_EXPERTS, D_MODEL, D_EXPERT), D_MODEL ** -0.5),
        "moe_w2": nrm((DEPTH, N_EXPERTS, D_EXPERT, D_MODEL), D_EXPERT ** -0.5),
    }


def reference(x, c, ctx, c_ctx, w_mod, b_mod, norm_mix, norm_ffn, norm_final,
              even_w_in, even_w_out, even_rpb, even_dw, even_dw_b, even_ln_g, even_ln_b,
              odd_w_in, odd_w_out, odd_conv, odd_lambda, odd_subln,
              moe_wg, moe_bg, moe_we, moe_be, moe_w1, moe_w3, moe_w2):
    bsz, n_tok, d = x.shape
    n_lat = bsz * n_tok
    cos, sin = axial_rope_tables(n_tok)
    silu_c = jax.nn.silu(c)
    silu_cc = jax.nn.silu(c_ctx)
    for l in range(DEPTH):
        ctx_out = l < DEPTH - 1
        mx = jnp.split((silu_c @ w_mod[l] + b_mod[l])[:, None, :], N_MOD, axis=-1)
        mc = jnp.split(silu_cc @ w_mod[l] + b_mod[l], N_MOD, axis=-1)
        hx = rmsnorm(x, norm_mix[l]) * (1.0 + mx[1]) + mx[0]
        hc = rmsnorm(ctx, norm_mix[l]) * (1.0 + mc[1]) + mc[0]
        if l % 2 == 0:
            e = l // 2
            y_x, y_c = even_mixer(hx, hc, even_w_in[e], even_w_out[e], even_rpb[e], even_dw[e],
                                  even_dw_b[e], even_ln_g[e], even_ln_b[e], ctx_out)
        else:
            o = l // 2
            lam_init = 0.8 - 0.6 * math.exp(-0.3 * l)
            y_x, y_c = odd_mixer(hx, hc, odd_w_in[o], odd_w_out[o], odd_conv[o], odd_lambda[o],
                                 odd_subln[o], lam_init, cos, sin, ctx_out)
        x = x + mx[2] * y_x
        hx = rmsnorm(x, norm_ffn[l]) * (1.0 + mx[4]) + mx[3]
        moe_p = (moe_wg[l], moe_bg[l], moe_we[l], moe_be[l], moe_w1[l], moe_w3[l], moe_w2[l])
        if ctx_out:
            ctx = ctx + mc[2] * y_c
            hc = rmsnorm(ctx, norm_ffn[l]) * (1.0 + mc[4]) + mc[3]
            tokens = jnp.concatenate([hx.reshape(-1, d), hc.reshape(-1, d)], axis=0)
            y = hierarchical_moe(tokens, *moe_p)
            x = x + mx[5] * y[:n_lat].reshape(x.shape)
            ctx = ctx + mc[5] * y[n_lat:].reshape(ctx.shape)
        else:
            x = x + mx[5] * hierarchical_moe(hx.reshape(-1, d), *moe_p).reshape(x.shape)
    return rmsnorm(x, norm_final)
```

```python
import functools
import math

import numpy as np
import jax
import jax.numpy as jnp
from jax import lax
from jax.experimental import pallas as pl
from jax.experimental.pallas import tpu as pltpu

F32 = jnp.float32
BF16 = jnp.bfloat16

D_MODEL = 1024
GRID_W = 64
N_MOD = 6
NORM_EPS = 1e-6
NEG_INF = -1e30
ROPE_BASE = 10000.0
SEG = 512
A_HEAD_DIM = 64
A_HEADS = 8
WIN_ROWS_MAX = 8
WIN_COLS = 16
B_CONV = 31
C_CONV = 3
D_QK_DIM = 64
D_V_DIM = 128
D_HEADS = 4
N_GROUPS = 4
EXPERTS_PER_GROUP = 8
N_EXPERTS = 32
D_EXPERT = 256
PAIRS_PER_GROUP = EXPERTS_PER_GROUP * (EXPERTS_PER_GROUP - 1) // 2
N_BUCKETS = N_GROUPS * PAIRS_PER_GROUP

V7X_LANES = 128
V7X_VMEM_LIMIT_BYTES = 56 * 1024 * 1024

PROJ_TM = 512
ATT_TQ = 256
NAT_KROWS = 12
CONV_CHUNK = 64
CONV_PAD = 16
MOE_TM = 256
PERM_CHUNK = 1024
ROUTER_GROUP_LANE0 = 0
ROUTER_EXPERT_LANE0 = 8

_PAIR_LO = np.array([i for i in range(8) for j in range(i + 1, 8)], np.int32)
_PAIR_HI = np.array([j for i in range(8) for j in range(i + 1, 8)], np.int32)
_BUCKET_E1 = np.concatenate([g * 8 + _PAIR_LO for g in range(N_GROUPS)]).astype(np.int32)
_BUCKET_E2 = np.concatenate([g * 8 + _PAIR_HI for g in range(N_GROUPS)]).astype(np.int32)


def _cparams(*sem):
    return pltpu.CompilerParams(dimension_semantics=tuple(sem), vmem_limit_bytes=V7X_VMEM_LIMIT_BYTES)


def _dot(a, b):
    return jnp.dot(a, b, preferred_element_type=F32)


def _dot_t(a, b):
    return lax.dot_general(a, b, (((1,), (1,)), ((), ())), preferred_element_type=F32)


def _split_bf16(a):
    hi = a.astype(BF16)
    lo = (a - hi.astype(F32)).astype(BF16)
    return hi, lo


def _dot3(a_hi, a_lo, b_hi, b_lo):
    return _dot(a_hi, b_hi) + _dot(a_lo, b_hi) + _dot(a_hi, b_lo)


def _sigmoid(x):
    return 1.0 / (1.0 + jnp.exp(-x))


def _rmsnorm_mod(x, g, scale, shift):
    y = x * lax.rsqrt(jnp.mean(x * x, axis=-1, keepdims=True) + NORM_EPS)
    return (y * g) * (1.0 + scale) + shift


def _mod_body(c_ref, w_ref, b_ref, o_ref):
    c = c_ref[...]
    s = c * _sigmoid(c)
    s_hi, s_lo = _split_bf16(s)
    w_hi, w_lo = _split_bf16(w_ref[0])
    o_ref[0] = _dot3(s_hi, s_lo, w_hi, w_lo) + b_ref[0]


def _modulation(cc, w_mod, b_mod):
    depth, d, n = w_mod.shape
    rows = cc.shape[0]
    tn = 1536
    return pl.pallas_call(
        _mod_body,
        out_shape=jax.ShapeDtypeStruct((depth, rows, n), F32),
        grid=(depth, n // tn),
        in_specs=[pl.BlockSpec((rows, d), lambda l, j: (0, 0)),
                  pl.BlockSpec((1, d, tn), lambda l, j: (l, 0, j)),
                  pl.BlockSpec((1, 1, tn), lambda l, j: (l, 0, j))],
        out_specs=pl.BlockSpec((1, rows, tn), lambda l, j: (l, 0, j)),
        compiler_params=_cparams("arbitrary", "arbitrary"),
        name="modulation",
    )(cc, w_mod, b_mod.reshape(depth, 1, n))


def _rope(v, cos, sin, lane):
    up = pltpu.roll(v, V7X_LANES - 16, axis=1)
    dn = pltpu.roll(v, 16, axis=1)
    sw = jnp.where((lane & 31) < 16, up, dn)
    return v * cos + sw * sin


def _proj_body(*refs, n_seg, res_idx, shift_idx, rope_segs, q_seg, write_x):
    refs = list(refs)
    x_ref = refs.pop(0)
    y_ref = refs.pop(0) if res_idx is not None else None
    rmod_ref = refs.pop(0) if res_idx is not None else None
    mod_ref = refs.pop(0)
    g_ref = refs.pop(0)
    w_ref = refs.pop(0)
    cos_ref = sin_ref = None
    if rope_segs:
        cos_ref = refs.pop(0)
        sin_ref = refs.pop(0)
    out_ref = refs.pop(0)
    xo_ref = refs.pop(0) if write_x else None

    x = x_ref[...]
    if y_ref is not None:
        x = x + rmod_ref[0, res_idx:res_idx + 1, :] * y_ref[...]
        if write_x:
            xo_ref[...] = x
    h = _rmsnorm_mod(x, g_ref[...], mod_ref[0, shift_idx + 1:shift_idx + 2, :],
                     mod_ref[0, shift_idx:shift_idx + 1, :]).astype(BF16)
    for s in range(n_seg):
        o = _dot(h, w_ref[:, s * SEG:(s + 1) * SEG])
        if s in rope_segs:
            lane = lax.broadcasted_iota(jnp.int32, (o.shape[0], V7X_LANES), 1)
            cos = cos_ref[...]
            sin = sin_ref[...]
            if s == q_seg:
                o = o * (D_QK_DIM ** -0.5)
            o = jnp.concatenate(
                [_rope(o[:, c * V7X_LANES:(c + 1) * V7X_LANES], cos, sin, lane) for c in range(SEG // V7X_LANES)],
                axis=1)
        out_ref[:, s * SEG:(s + 1) * SEG] = o.astype(BF16)


def _proj(x, mod, g, w, *, seq, mod_row, y=None, res_mod=None, res_idx=None, shift_idx=0, rope=None, rope_segs=(), q_seg=None,
          write_x=False, name="proj"):
    t, d = x.shape
    n = w.shape[1]
    tm = min(PROJ_TM, t if seq is None else seq)
    tiles_per_seq = None if seq is None else seq // tm
    if seq is None:
        mod_map = lambda i: (mod_row, 0, 0)
    else:
        mod_map = lambda i: (i // tiles_per_seq, 0, 0)
    args = [x]
    specs = [pl.BlockSpec((tm, d), lambda i: (i, 0))]
    if y is not None:
        args += [y, res_mod]
        specs += [pl.BlockSpec((tm, d), lambda i: (i, 0)), pl.BlockSpec((1, N_MOD, d), mod_map)]
    args += [mod, g.reshape(1, d), w]
    specs += [pl.BlockSpec((1, N_MOD, d), mod_map), pl.BlockSpec((1, d), lambda i: (0, 0)),
              pl.BlockSpec((d, n), lambda i: (0, 0))]
    if rope_segs:
        args += [rope[0], rope[1]]
        specs += [pl.BlockSpec((tm, V7X_LANES), lambda i: (i % tiles_per_seq, 0))] * 2
    out_shape = [jax.ShapeDtypeStruct((t, n), BF16)]
    out_specs = [pl.BlockSpec((tm, n), lambda i: (i, 0))]
    if write_x:
        out_shape.append(jax.ShapeDtypeStruct((t, d), F32))
        out_specs.append(pl.BlockSpec((tm, d), lambda i: (i, 0)))
    body = functools.partial(_proj_body, n_seg=n // SEG, res_idx=res_idx if y is not None else None,
                             shift_idx=shift_idx, rope_segs=tuple(rope_segs), q_seg=q_seg, write_x=write_x)
    outs = pl.pallas_call(
        body, out_shape=out_shape, grid=(t // tm,), in_specs=specs, out_specs=out_specs,
        compiler_params=_cparams("arbitrary"), name=name,
    )(*args)
    return outs if write_x else outs[0]


def _rope_tables(n_tok):
    quarter = D_QK_DIM // 4
    t = np.arange(n_tok)
    row = (t // GRID_W).astype(np.float32)
    col = (t % GRID_W).astype(np.float32)
    inv = jnp.power(ROPE_BASE, -jnp.arange(quarter, dtype=F32) / quarter)
    ar = jnp.asarray(row)[:, None] * inv
    ac = jnp.asarray(col)[:, None] * inv
    cos64 = jnp.concatenate([jnp.cos(ar), jnp.cos(ar), jnp.cos(ac), jnp.cos(ac)], axis=-1)
    sin64 = jnp.concatenate([-jnp.sin(ar), jnp.sin(ar), -jnp.sin(ac), jnp.sin(ac)], axis=-1)
    return jnp.tile(cos64, (1, V7X_LANES // D_QK_DIM)), jnp.tile(sin64, (1, V7X_LANES // D_QK_DIM))


def _stack_halves(q):
    lane = lax.broadcasted_iota(jnp.int32, q.shape, 1)
    zero = jnp.zeros_like(q)
    return jnp.concatenate([jnp.where(lane < 64, q, zero), jnp.where(lane >= 64, q, zero)], axis=0)


def _softmax_parts(scores):
    m = functools.reduce(jnp.maximum, [jnp.max(s, axis=1, keepdims=True) for s in scores])
    ps = [jnp.exp(s - m) for s in scores]
    l = functools.reduce(lambda a, b: a + b, [jnp.sum(p, axis=1, keepdims=True) for p in ps])
    return ps, l


def _natten_body(ks_ref, pid_ref, q_ref, k_ref, v_ref, kc_ref, vc_ref, bias_ref, o_ref):
    j = pl.program_id(1)
    k0 = pl.multiple_of(ks_ref[j] * GRID_W, GRID_W)
    p = pid_ref[j]
    nk = NAT_KROWS * GRID_W
    tq = q_ref.shape[0]
    lane = lax.broadcasted_iota(jnp.int32, (tq, V7X_LANES), 1)
    for hp in range(A_HEADS // 2):
        cs = slice(hp * V7X_LANES, (hp + 1) * V7X_LANES)
        q2 = _stack_halves(q_ref[:, cs] * (A_HEAD_DIM ** -0.5))
        s_c = _dot_t(q2, kc_ref[:, cs])
        s_l = _dot_t(q2, k_ref[pl.ds(k0, nk), cs])
        bias = jnp.concatenate([bias_ref[p, 2 * hp], bias_ref[p, 2 * hp + 1]], axis=0).astype(F32)
        (p_c, p_l), l = _softmax_parts([s_c, s_l + bias])
        o2 = _dot(p_c.astype(BF16), vc_ref[:, cs]) + _dot(p_l.astype(BF16), v_ref[pl.ds(k0, nk), cs])
        o2 = o2 / l
        o_ref[:, cs] = jnp.where(lane < 64, o2[:tq], o2[tq:]).astype(BF16)


def _natten_tables(rows):
    q_rows = ATT_TQ // GRID_W
    nqb = rows // q_rows
    win_r = min(WIN_ROWS_MAX, rows)
    col = np.arange(GRID_W)
    c_start = np.clip(col - WIN_COLS // 2, 0, GRID_W - WIN_COLS)
    col_ok = (col[None, :] >= c_start[:, None]) & (col[None, :] < c_start[:, None] + WIN_COLS)
    dc = np.clip(col[None, :] - col[:, None], -(WIN_COLS - 1), WIN_COLS - 1) + WIN_COLS - 1
    ks_list, pats, pid = [], [], []
    for j in range(nqb):
        r0 = j * q_rows
        ks = int(np.clip(r0 - win_r // 2, 0, rows - NAT_KROWS))
        qr = r0 + np.arange(q_rows)
        rs = np.clip(qr - win_r // 2, 0, rows - win_r)
        kr = ks + np.arange(NAT_KROWS)
        row_ok = (kr[None, :] >= rs[:, None]) & (kr[None, :] < rs[:, None] + win_r)
        dr = np.clip(kr[None, :] - qr[:, None] + WIN_ROWS_MAX - 1, 0, 2 * WIN_ROWS_MAX - 2)
        valid = (row_ok[:, None, :, None] & col_ok[None, :, None, :])
        valid = valid.reshape(q_rows * GRID_W, NAT_KROWS * GRID_W)
        dr_f = np.broadcast_to(dr[:, None, :, None], (q_rows, GRID_W, NAT_KROWS, GRID_W)).reshape(valid.shape)
        dc_f = np.broadcast_to(dc[None, :, None, :], (q_rows, GRID_W, NAT_KROWS, GRID_W)).reshape(valid.shape)
        key = (valid.tobytes(), (dr_f * valid).tobytes())
        for n, (k_, *_rest) in enumerate(pats):
            if k_ == key:
                pid.append(n)
                break
        else:
            pid.append(len(pats))
            pats.append((key, valid, dr_f, dc_f))
        ks_list.append(ks)
    valid = np.stack([p_[1] for p_ in pats])
    dr_f = np.stack([p_[2] for p_ in pats])
    dc_f = np.stack([p_[3] for p_ in pats])
    return np.array(ks_list, np.int32), np.array(pid, np.int32), valid, dr_f, dc_f


def _natten(px, pc, rpb, *, bsz, n_tok, ctx_len):
    rows = n_tok // GRID_W
    assert rows >= NAT_KROWS and n_tok % ATT_TQ == 0
    ks, pid, valid, dr_f, dc_f = _natten_tables(rows)
    bias = jnp.where(valid[None], rpb[:, dr_f, dc_f], NEG_INF).astype(BF16)
    bias = bias.transpose(1, 0, 2, 3)
    n_pat = bias.shape[0]
    nqb = n_tok // ATT_TQ
    nk = NAT_KROWS * GRID_W
    grid_spec = pltpu.PrefetchScalarGridSpec(
        num_scalar_prefetch=2, grid=(bsz, nqb),
        in_specs=[pl.BlockSpec((ATT_TQ, SEG), lambda b, j, *_: (b * nqb + j, 0)),
                  pl.BlockSpec((n_tok, SEG), lambda b, j, *_: (b, 1)),
                  pl.BlockSpec((n_tok, SEG), lambda b, j, *_: (b, 2)),
                  pl.BlockSpec((ctx_len, SEG), lambda b, j, *_: (b, 1)),
                  pl.BlockSpec((ctx_len, SEG), lambda b, j, *_: (b, 2)),
                  pl.BlockSpec((n_pat, A_HEADS, ATT_TQ, nk), lambda b, j, *_: (0, 0, 0, 0))],
        out_specs=pl.BlockSpec((ATT_TQ, SEG), lambda b, j, *_: (b * nqb + j, 0)))
    return pl.pallas_call(
        _natten_body, out_shape=jax.ShapeDtypeStruct((bsz * n_tok, SEG), BF16), grid_spec=grid_spec,
        compiler_params=_cparams("arbitrary", "arbitrary"), name="natten",
    )(jnp.asarray(ks), jnp.asarray(pid), px, px, px, pc, pc, bias)


def _ctxattn_body(q_ref, k_ref, v_ref, o_ref):
    tq = q_ref.shape[0]
    lane = lax.broadcasted_iota(jnp.int32, (tq, V7X_LANES), 1)
    for hp in range(A_HEADS // 2):
        cs = slice(hp * V7X_LANES, (hp + 1) * V7X_LANES)
        q2 = _stack_halves(q_ref[:, cs] * (A_HEAD_DIM ** -0.5))
        (p,), l = _softmax_parts([_dot_t(q2, k_ref[:, cs])])
        o2 = _dot(p.astype(BF16), v_ref[:, cs]) / l
        o_ref[:, cs] = jnp.where(lane < 64, o2[:tq], o2[tq:]).astype(BF16)


def _ctxattn(pc, *, bsz, ctx_len):
    return pl.pallas_call(
        _ctxattn_body, out_shape=jax.ShapeDtypeStruct((bsz * ctx_len, SEG), BF16), grid=(bsz,),
        in_specs=[pl.BlockSpec((ctx_len, SEG), lambda b: (b, 0)),
                  pl.BlockSpec((ctx_len, SEG), lambda b: (b, 1)),
                  pl.BlockSpec((ctx_len, SEG), lambda b: (b, 2))],
        out_specs=pl.BlockSpec((ctx_len, SEG), lambda b: (b, 0)),
        compiler_params=_cparams("arbitrary"), name="ctxattn",
    )(pc, pc, pc)


def _conv_accumulate(z_ref, w_ref, t0, taps):
    pad = taps // 2
    win = z_ref.at[pl.ds(t0, CONV_CHUNK + 2 * CONV_PAD)]
    acc = None
    for k in range(taps):
        term = win[pl.ds(CONV_PAD - pad + k, CONV_CHUNK), :] * w_ref[k:k + 1, :]
        acc = term if acc is None else acc + term
    return acc


def _conformer_body(u_ref, g_ref, w_ref, b_ref, lg_ref, lb_ref, o_ref, z_ref):
    n = u_ref.shape[0]
    halo = jnp.zeros((CONV_PAD, SEG), F32)
    z_ref[pl.ds(0, CONV_PAD), :] = halo
    z_ref[pl.ds(CONV_PAD + n, CONV_PAD), :] = halo
    z_ref[pl.ds(CONV_PAD, n), :] = u_ref[...].astype(F32) * _sigmoid(g_ref[...].astype(F32))

    def chunk(i, carry):
        t0 = pl.multiple_of(i * CONV_CHUNK, CONV_CHUNK)
        y = _conv_accumulate(z_ref, w_ref, t0, B_CONV) + b_ref[...]
        mu = jnp.mean(y, axis=-1, keepdims=True)
        yc = y - mu
        var = jnp.mean(yc * yc, axis=-1, keepdims=True)
        y = yc * lax.rsqrt(var + NORM_EPS) * lg_ref[...] + lb_ref[...]
        o_ref[pl.ds(t0, CONV_CHUNK), :] = (y * _sigmoid(y)).astype(BF16)
        return carry

    lax.fori_loop(0, n // CONV_CHUNK, chunk, 0)


def _conformer(p, dw, dw_b, ln_g, ln_b, *, bsz, seq):
    vec = lambda a: a.reshape(1, SEG)
    return pl.pallas_call(
        _conformer_body, out_shape=jax.ShapeDtypeStruct((bsz * seq, SEG), BF16), grid=(bsz,),
        in_specs=[pl.BlockSpec((seq, SEG), lambda b: (b, 3)),
                  pl.BlockSpec((seq, SEG), lambda b: (b, 4)),
                  pl.BlockSpec((B_CONV, SEG), lambda b: (0, 0))] + [pl.BlockSpec((1, SEG), lambda b: (0, 0))] * 3,
        out_specs=pl.BlockSpec((seq, SEG), lambda b: (b, 0)),
        scratch_shapes=[pltpu.VMEM((seq + 2 * CONV_PAD, SEG), F32)],
        compiler_params=_cparams("arbitrary"), name="conformer_conv",
    )(p, p, dw, vec(dw_b), vec(ln_g), vec(ln_b))


def _shortconv_body(bg_ref, cg_ref, u_ref, w_ref, o_ref, z_ref):
    n = u_ref.shape[0]
    halo = jnp.zeros((CONV_PAD, SEG), F32)
    z_ref[pl.ds(0, CONV_PAD), :] = halo
    z_ref[pl.ds(CONV_PAD + n, CONV_PAD), :] = halo
    z_ref[pl.ds(CONV_PAD, n), :] = cg_ref[...].astype(F32) * u_ref[...].astype(F32)

    def chunk(i, carry):
        t0 = pl.multiple_of(i * CONV_CHUNK, CONV_CHUNK)
        y = _conv_accumulate(z_ref, w_ref, t0, C_CONV)
        o_ref[pl.ds(t0, CONV_CHUNK), :] = (bg_ref[pl.ds(t0, CONV_CHUNK), :].astype(F32) * y).astype(BF16)
        return carry

    lax.fori_loop(0, n // CONV_CHUNK, chunk, 0)


def _shortconv(p, conv_w, *, bsz, seq):
    return pl.pallas_call(
        _shortconv_body, out_shape=jax.ShapeDtypeStruct((bsz * seq, SEG), BF16), grid=(bsz,),
        in_specs=[pl.BlockSpec((seq, SEG), lambda b: (b, 0)),
                  pl.BlockSpec((seq, SEG), lambda b: (b, 1)),
                  pl.BlockSpec((seq, SEG), lambda b: (b, 2)),
                  pl.BlockSpec((C_CONV, SEG), lambda b: (0, 0))],
        out_specs=pl.BlockSpec((seq, SEG), lambda b: (b, 0)),
        scratch_shapes=[pltpu.VMEM((seq + 2 * CONV_PAD, SEG), F32)],
        compiler_params=_cparams("arbitrary"), name="short_conv",
    )(p, p, p, conv_w)


def _diffattn_body(q_ref, k_ref, v_ref, kc_ref, vc_ref, lam_ref, g_ref, o_ref, *, lam_init):
    tq = q_ref.shape[0]
    lp = lam_ref[...]
    lam = (jnp.exp(jnp.sum(lp[0:1, :] * lp[1:2, :], axis=1, keepdims=True))
           - jnp.exp(jnp.sum(lp[2:3, :] * lp[3:4, :], axis=1, keepdims=True)) + lam_init)
    for h in range(D_HEADS):
        cs = slice(h * V7X_LANES, (h + 1) * V7X_LANES)
        q2 = _stack_halves(q_ref[:, cs])
        (p_c, p_x), l = _softmax_parts([_dot_t(q2, kc_ref[:, cs]), _dot_t(q2, k_ref[:, cs])])
        r = 1.0 / l
        a1 = r[:tq]
        a2 = r[tq:] * lam
        w_c = (p_c[:tq] * a1 - p_c[tq:] * a2).astype(BF16)
        w_x = (p_x[:tq] * a1 - p_x[tq:] * a2).astype(BF16)
        o = _dot(w_c, vc_ref[:, cs]) + _dot(w_x, v_ref[:, cs])
        o = o * lax.rsqrt(jnp.mean(o * o, axis=-1, keepdims=True) + NORM_EPS) * g_ref[...]
        o_ref[:, cs] = (o * (1.0 - lam_init)).astype(BF16)


def _diffattn(px, pc, lam_p, subln, lam_init, *, bsz, n_tok, ctx_len):
    nq = n_tok // ATT_TQ
    body = functools.partial(_diffattn_body, lam_init=lam_init)
    return pl.pallas_call(
        body, out_shape=jax.ShapeDtypeStruct((bsz * n_tok, SEG), BF16), grid=(bsz, nq),
        in_specs=[pl.BlockSpec((ATT_TQ, SEG), lambda b, j: (b * nq + j, 3)),
                  pl.BlockSpec((n_tok, SEG), lambda b, j: (b, 4)),
                  pl.BlockSpec((n_tok, SEG), lambda b, j: (b, 5)),
                  pl.BlockSpec((ctx_len, SEG), lambda b, j: (b, 0)),
                  pl.BlockSpec((ctx_len, SEG), lambda b, j: (b, 1)),
                  pl.BlockSpec((4, D_QK_DIM), lambda b, j: (0, 0)),
                  pl.BlockSpec((1, D_V_DIM), lambda b, j: (0, 0))],
        out_specs=pl.BlockSpec((ATT_TQ, SEG), lambda b, j: (b * nq + j, 0)),
        compiler_params=_cparams("arbitrary", "arbitrary"), name="diff_attn",
    )(px, px, px, pc, pc, lam_p, subln.reshape(1, D_V_DIM))


def _oproj_body(a_ref, b_ref, x_ref, mod_ref, g_ref, wo_ref, wrh_ref, wrl_ref, br_ref, cin_ref,
                x1_ref, h_ref, bucket_ref, rank_ref, cout_ref, carry_ref):
    i = pl.program_id(0)
    tm = x_ref.shape[0]

    @pl.when(i == 0)
    def _():
        carry_ref[...] = cin_ref[...]

    y = _dot(a_ref[...], wo_ref[0:SEG, :]) + _dot(b_ref[...], wo_ref[SEG:2 * SEG, :])
    x1 = x_ref[...] + mod_ref[0, 2:3, :] * y
    x1_ref[...] = x1
    h = _rmsnorm_mod(x1, g_ref[...], mod_ref[0, 4:5, :], mod_ref[0, 3:4, :])
    h_ref[...] = h

    h_hi, h_lo = _split_bf16(h)
    logits = _dot3(h_hi, h_lo, wrh_ref[...], wrl_ref[...]) + br_ref[...]
    lane = lax.broadcasted_iota(jnp.int32, (tm, V7X_LANES), 1)
    none = jnp.int32(V7X_LANES)
    gl = jnp.where(lane < N_GROUPS, logits, -jnp.inf)
    gm = jnp.max(gl, axis=1, keepdims=True)
    g_sel = jnp.min(jnp.where(gl == gm, lane, none), axis=1, keepdims=True)
    e_lane0 = ROUTER_EXPERT_LANE0 + EXPERTS_PER_GROUP * g_sel
    in_group = (lane >= e_lane0) & (lane < e_lane0 + EXPERTS_PER_GROUP)
    el = jnp.where(in_group, logits, -jnp.inf)
    m1 = jnp.max(el, axis=1, keepdims=True)
    i1 = jnp.min(jnp.where(el == m1, lane, none), axis=1, keepdims=True)
    el2 = jnp.where(lane == i1, -jnp.inf, el)
    m2 = jnp.max(el2, axis=1, keepdims=True)
    i2 = jnp.min(jnp.where(el2 == m2, lane, none), axis=1, keepdims=True)
    lo = jnp.minimum(i1, i2) - e_lane0
    hi = jnp.maximum(i1, i2) - e_lane0
    bucket = g_sel * PAIRS_PER_GROUP + ((lo * (2 * EXPERTS_PER_GROUP - 1 - lo)) >> 1) + (hi - lo - 1)
    onehot = lane == bucket
    r_i = lax.broadcasted_iota(jnp.int32, (tm, tm), 0)
    c_i = lax.broadcasted_iota(jnp.int32, (tm, tm), 1)
    before = jnp.where(r_i > c_i, 1.0, 0.0).astype(BF16)
    oh = jnp.where(onehot, 1.0, 0.0)
    cum = _dot(before, oh.astype(BF16)) + carry_ref[...]
    rank = jnp.sum(jnp.where(onehot, cum, 0.0), axis=1, keepdims=True)
    bucket_ref[...] = bucket
    rank_ref[...] = rank.astype(jnp.int32)
    carry_ref[...] = carry_ref[...] + jnp.sum(oh, axis=0, keepdims=True)

    @pl.when(i == pl.num_programs(0) - 1)
    def _():
        cout_ref[...] = carry_ref[...]


def _oproj(a, b, x, mod, g, wo, wr_hi, wr_lo, br, counts_in, *, seq, mod_row, name="oproj"):
    t, d = x.shape
    tm = min(PROJ_TM, t if seq is None else seq)
    if seq is None:
        mod_map = lambda i: (mod_row, 0, 0)
    else:
        tiles_per_seq = seq // tm
        mod_map = lambda i: (i // tiles_per_seq, 0, 0)
    row = lambda i: (i, 0)
    const = lambda i: (0, 0)
    return pl.pallas_call(
        _oproj_body,
        out_shape=[jax.ShapeDtypeStruct((t, d), F32), jax.ShapeDtypeStruct((t, d), F32),
                   jax.ShapeDtypeStruct((t, 1), jnp.int32), jax.ShapeDtypeStruct((t, 1), jnp.int32),
                   jax.ShapeDtypeStruct((1, V7X_LANES), F32)],
        grid=(t // tm,),
        in_specs=[pl.BlockSpec((tm, SEG), row), pl.BlockSpec((tm, SEG), row), pl.BlockSpec((tm, d), row),
                  pl.BlockSpec((1, N_MOD, d), mod_map), pl.BlockSpec((1, d), const),
                  pl.BlockSpec((2 * SEG, d), const), pl.BlockSpec((d, V7X_LANES), const),
                  pl.BlockSpec((d, V7X_LANES), const), pl.BlockSpec((1, V7X_LANES), const),
                  pl.BlockSpec((1, V7X_LANES), const)],
        out_specs=[pl.BlockSpec((tm, d), row), pl.BlockSpec((tm, d), row), pl.BlockSpec((tm, 1), row),
                   pl.BlockSpec((tm, 1), row), pl.BlockSpec((1, V7X_LANES), const)],
        scratch_shapes=[pltpu.VMEM((1, V7X_LANES), F32)],
        compiler_params=_cparams("arbitrary"), name=name,
    )(a, b, x, mod, g.reshape(1, d), wo, wr_hi, wr_lo, br, counts_in)


def _router_weights(wg, bg, we, be):
    d = wg.shape[0]
    w = jnp.zeros((d, V7X_LANES), F32)
    w = w.at[:, ROUTER_GROUP_LANE0:ROUTER_GROUP_LANE0 + N_GROUPS].set(wg)
    w = w.at[:, ROUTER_EXPERT_LANE0:ROUTER_EXPERT_LANE0 + N_EXPERTS].set(we)
    b = jnp.zeros((1, V7X_LANES), F32)
    b = b.at[0, ROUTER_GROUP_LANE0:ROUTER_GROUP_LANE0 + N_GROUPS].set(bg)
    b = b.at[0, ROUTER_EXPERT_LANE0:ROUTER_EXPERT_LANE0 + N_EXPERTS].set(be)
    w_hi = w.astype(BF16)
    w_lo = (w - w_hi.astype(F32)).astype(BF16)
    return w_hi, w_lo, b


def _permute_body(off_ref, bucket_ref, rank_ref, rows_hbm, sorted_hbm, *rest, chunk, scatter):
    out_hbm, sem = rest[-2], rest[-1]
    base = pl.program_id(0) * chunk

    def row_copy(t):
        p = off_ref[bucket_ref[t]] + rank_ref[t]
        if scatter:
            return pltpu.make_async_copy(rows_hbm.at[pl.ds(base + t, 1)], out_hbm.at[pl.ds(p, 1)], sem)
        return pltpu.make_async_copy(sorted_hbm.at[pl.ds(p, 1)], out_hbm.at[pl.ds(base + t, 1)], sem)

    def start(t, carry):
        row_copy(t).start()
        return carry

    lax.fori_loop(0, chunk, start, 0, unroll=8)

    def wait(t, carry):
        row_copy(t).wait()
        return carry

    lax.fori_loop(0, chunk, wait, 0, unroll=8)


def _permute(offsets, bucket, rank, rows, sorted_rows, *, scatter, name):
    t, d = rows.shape
    chunk = min(PERM_CHUNK, t)
    smem_blk = pl.BlockSpec((chunk,), lambda i, *_: (i,), memory_space=pltpu.SMEM)
    any_spec = pl.BlockSpec(memory_space=pl.ANY)
    body = functools.partial(_permute_body, chunk=chunk, scatter=scatter)
    if scatter:
        out_shape = jax.ShapeDtypeStruct(sorted_rows.shape, sorted_rows.dtype)
        aliases = {4: 0}
    else:
        out_shape = jax.ShapeDtypeStruct((t, d), sorted_rows.dtype)
        aliases = {}
    grid_spec = pltpu.PrefetchScalarGridSpec(
        num_scalar_prefetch=1, grid=(t // chunk,),
        in_specs=[smem_blk, smem_blk, any_spec, any_spec], out_specs=any_spec,
        scratch_shapes=[pltpu.SemaphoreType.DMA])
    return pl.pallas_call(
        body, out_shape=out_shape, grid_spec=grid_spec, input_output_aliases=aliases,
        compiler_params=pltpu.CompilerParams(dimension_semantics=("arbitrary",), has_side_effects=True),
        name=name,
    )(offsets, bucket, rank, rows, sorted_rows)


def _moe_body(te1_ref, te2_ref, tval_ref, trow_ref, x_ref, w1a_ref, w1b_ref, w3a_ref, w3b_ref, w2a_ref, w2b_ref,
              wr_ref, br_ref, y_ref):
    i = pl.program_id(0)

    @pl.when(tval_ref[i] == 0)
    def _():
        y_ref[...] = jnp.zeros_like(y_ref)

    @pl.when(tval_ref[i] > 0)
    def _():
        tm = x_ref.shape[0]
        x = x_ref[...].astype(BF16)
        e1 = te1_ref[i]
        e2 = te2_ref[i]
        grp = e1 >> 3
        logits = _dot(x, wr_ref[...]) + br_ref[...]
        lane = lax.broadcasted_iota(jnp.int32, (tm, V7X_LANES), 1)
        pick = lambda idx: jnp.sum(jnp.where(lane == idx, logits, 0.0), axis=1, keepdims=True)
        gl = jnp.where(lane < N_GROUPS, logits, -jnp.inf)
        gm = jnp.max(gl, axis=1, keepdims=True)
        gz = jnp.sum(jnp.exp(gl - gm), axis=1, keepdims=True)
        g_w = jnp.exp(pick(ROUTER_GROUP_LANE0 + grp) - gm) / gz
        l1 = pick(ROUTER_EXPERT_LANE0 + e1)
        l2 = pick(ROUTER_EXPERT_LANE0 + e2)
        m = jnp.maximum(l1, l2)
        p1 = jnp.exp(l1 - m)
        p2 = jnp.exp(l2 - m)
        c1 = g_w * p1 / (p1 + p2)
        c2 = g_w * p2 / (p1 + p2)

        def expert(w1_ref, w3_ref, w2_ref):
            h = _dot(x, w1_ref[0])
            hid = (h * _sigmoid(h)) * _dot(x, w3_ref[0])
            return _dot(hid.astype(BF16), w2_ref[0])

        y_ref[...] = c1 * expert(w1a_ref, w3a_ref, w2a_ref) + c2 * expert(w1b_ref, w3b_ref, w2b_ref)


def _moe_sorted(xs, te1, te2, tval, trow, w1, w3, w2, wr_hi, br):
    tp, d = xs.shape
    n_tiles = tp // MOE_TM
    xmap = lambda i, te1, te2, tval, trow: (trow[i], 0)
    wa = lambda i, te1, te2, tval, trow: (te1[i], 0, 0)
    wb = lambda i, te1, te2, tval, trow: (te2[i], 0, 0)
    const = lambda i, *_: (0, 0)
    grid_spec = pltpu.PrefetchScalarGridSpec(
        num_scalar_prefetch=4, grid=(n_tiles,),
        in_specs=[pl.BlockSpec((MOE_TM, d), xmap),
                  pl.BlockSpec((1, d, D_EXPERT), wa), pl.BlockSpec((1, d, D_EXPERT), wb),
                  pl.BlockSpec((1, d, D_EXPERT), wa), pl.BlockSpec((1, d, D_EXPERT), wb),
                  pl.BlockSpec((1, D_EXPERT, d), wa), pl.BlockSpec((1, D_EXPERT, d), wb),
                  pl.BlockSpec((d, V7X_LANES), const), pl.BlockSpec((1, V7X_LANES), const)],
        out_specs=pl.BlockSpec((MOE_TM, d), lambda i, *_: (i, 0)))
    return pl.pallas_call(
        _moe_body, out_shape=jax.ShapeDtypeStruct((tp, d), F32), grid_spec=grid_spec,
        compiler_params=_cparams("arbitrary"), name="moe_experts",
    )(te1, te2, tval, trow, xs, w1, w1, w3, w3, w2, w2, wr_hi, br)


def _moe(h_parts, route_parts, counts, w1, w3, w2, wr_hi, br):
    t_total = sum(h.shape[0] for h in h_parts)
    d = h_parts[0].shape[1]
    n_tiles = t_total // MOE_TM + N_BUCKETS
    cnt = counts[0, :N_BUCKETS].astype(jnp.int32)
    tiles_b = (cnt + MOE_TM - 1) // MOE_TM
    tile_end = jnp.cumsum(tiles_b)
    n_valid = tile_end[-1]
    offsets = jnp.zeros((V7X_LANES,), jnp.int32).at[:N_BUCKETS].set((tile_end - tiles_b) * MOE_TM)
    tile_id = jnp.arange(n_tiles, dtype=jnp.int32)
    tval = (tile_id < n_valid).astype(jnp.int32)
    trow = jnp.minimum(tile_id, n_valid - 1)
    tbucket = jnp.sum((trow[:, None] >= tile_end[None, :]).astype(jnp.int32), axis=1)
    tbucket = jnp.minimum(tbucket, N_BUCKETS - 1)
    te1 = jnp.asarray(_BUCKET_E1)[tbucket]
    te2 = jnp.asarray(_BUCKET_E2)[tbucket]

    xs = jnp.zeros((n_tiles * MOE_TM, d), F32)
    for h, (bucket, rank) in zip(h_parts, route_parts):
        xs = _permute(offsets, bucket, rank, h, xs, scatter=True, name="moe_scatter_rows")
    ys = _moe_sorted(xs, te1, te2, tval, trow, w1, w3, w2, wr_hi, br)
    return [_permute(offsets, bucket, rank, h, ys, scatter=False, name="moe_gather_rows")
            for h, (bucket, rank) in zip(h_parts, route_parts)]


def _final_body(x_ref, y_ref, mod_ref, g_ref, o_ref):
    x = x_ref[...] + mod_ref[0, 5:6, :] * y_ref[...]
    o_ref[...] = x * lax.rsqrt(jnp.mean(x * x, axis=-1, keepdims=True) + NORM_EPS) * g_ref[...]


def _final(x, y, mod, g, *, seq):
    t, d = x.shape
    tm = min(PROJ_TM, seq)
    tiles_per_seq = seq // tm
    row = lambda i: (i, 0)
    return pl.pallas_call(
        _final_body, out_shape=jax.ShapeDtypeStruct((t, d), F32), grid=(t // tm,),
        in_specs=[pl.BlockSpec((tm, d), row), pl.BlockSpec((tm, d), row),
                  pl.BlockSpec((1, N_MOD, d), lambda i: (i // tiles_per_seq, 0, 0)),
                  pl.BlockSpec((1, d), lambda i: (0, 0))],
        out_specs=pl.BlockSpec((tm, d), row),
        compiler_params=_cparams("arbitrary"), name="final_norm",
    )(x, y, mod, g.reshape(1, d))


def kernel(x, c, ctx, c_ctx, w_mod, b_mod, norm_mix, norm_ffn, norm_final, even_w_in, even_w_out, even_rpb,
           even_dw, even_dw_b, even_ln_g, even_ln_b, odd_w_in, odd_w_out, odd_conv, odd_lambda, odd_subln,
           moe_wg, moe_bg, moe_we, moe_be, moe_w1, moe_w3, moe_w2):
    bsz, n_tok, d = x.shape
    ctx_len = ctx.shape[1]
    depth = w_mod.shape[0]
    assert depth == 2 and d == D_MODEL
    xs = x.reshape(bsz * n_tok, d)
    cs = ctx.reshape(bsz * ctx_len, d)

    mod_rows = -(-(bsz + 1) // 8) * 8
    cc = jnp.zeros((mod_rows, d), F32).at[:bsz].set(c).at[bsz].set(c_ctx)
    mod = _modulation(cc, w_mod, b_mod).reshape(depth, mod_rows, N_MOD, d)
    bf = lambda a: a.astype(BF16)
    zero_counts = jnp.zeros((1, V7X_LANES), F32)

    l = 0
    w_in = bf(even_w_in[0])
    w_out = bf(even_w_out[0])
    px = _proj(xs, mod[l], norm_mix[l], w_in, seq=n_tok, mod_row=None, name="proj_even_x")
    pc = _proj(cs, mod[l], norm_mix[l], w_in, seq=None, mod_row=bsz, name="proj_even_c")
    a_x = _natten(px, pc, even_rpb[0], bsz=bsz, n_tok=n_tok, ctx_len=ctx_len)
    a_c = _ctxattn(pc, bsz=bsz, ctx_len=ctx_len)
    b_x = _conformer(px, even_dw[0], even_dw_b[0], even_ln_g[0], even_ln_b[0], bsz=bsz, seq=n_tok)
    b_c = _conformer(pc, even_dw[0], even_dw_b[0], even_ln_g[0], even_ln_b[0], bsz=bsz, seq=ctx_len)
    wr_hi, wr_lo, br = _router_weights(moe_wg[l], moe_bg[l], moe_we[l], moe_be[l])
    x1, hx, bkt_x, rnk_x, counts = _oproj(a_x, b_x, xs, mod[l], norm_ffn[l], w_out, wr_hi, wr_lo, br, zero_counts,
                                          seq=n_tok, mod_row=None, name="oproj_even_x")
    c1, hc, bkt_c, rnk_c, counts = _oproj(a_c, b_c, cs, mod[l], norm_ffn[l], w_out, wr_hi, wr_lo, br, counts,
                                          seq=None, mod_row=bsz, name="oproj_even_c")
    y_x, y_c = _moe([hx, hc], [(bkt_x.reshape(-1), rnk_x.reshape(-1)), (bkt_c.reshape(-1), rnk_c.reshape(-1))],
                    counts, bf(moe_w1[l]), bf(moe_w3[l]), bf(moe_w2[l]), wr_hi, br)

    l = 1
    lam_init = 0.8 - 0.6 * math.exp(-0.3 * l)
    w_in = bf(odd_w_in[0])
    w_out = bf(odd_w_out[0])
    rope = _rope_tables(n_tok)
    px, x2 = _proj(x1, mod[l], norm_mix[l], w_in, seq=n_tok, mod_row=None, y=y_x, res_mod=mod[l - 1], res_idx=5,
                   rope=rope, rope_segs=(3, 4), q_seg=3, write_x=True, name="proj_odd_x")
    pc = _proj(c1, mod[l], norm_mix[l], w_in[:, 4 * SEG:], seq=None, mod_row=bsz, y=y_c, res_mod=mod[l - 1],
               res_idx=5, name="proj_odd_c")
    s_x = _shortconv(px, odd_conv[0], bsz=bsz, seq=n_tok)
    d_x = _diffattn(px, pc, odd_lambda[0], odd_subln[0], lam_init, bsz=bsz, n_tok=n_tok, ctx_len=ctx_len)
    wr_hi, wr_lo, br = _router_weights(moe_wg[l], moe_bg[l], moe_we[l], moe_be[l])
    x3, hx, bkt_x, rnk_x, counts = _oproj(s_x, d_x, x2, mod[l], norm_ffn[l], w_out, wr_hi, wr_lo, br, zero_counts,
                                          seq=n_tok, mod_row=None, name="oproj_odd_x")
    (y_x,) = _moe([hx], [(bkt_x.reshape(-1), rnk_x.reshape(-1))], counts,
                  bf(moe_w1[l]), bf(moe_w3[l]), bf(moe_w2[l]), wr_hi, br)
    out = _final(x3, y_x, mod[l], norm_final, seq=n_tok)
    return out.reshape(bsz, n_tok, d)
```

```python
import functools
import math

import numpy as np
import jax
import jax.numpy as jnp
from jax import lax
from jax.experimental import pallas as pl
from jax.experimental.pallas import tpu as pltpu

F32 = jnp.float32
BF16 = jnp.bfloat16

D_MODEL = 1024
GRID_W = 64
N_MOD = 6
NORM_EPS = 1e-6
NEG_INF = -1e30
ROPE_BASE = 10000.0
SEG = 512
A_HEAD_DIM = 64
A_HEADS = 8
WIN_ROWS_MAX = 8
WIN_COLS = 16
B_CONV = 31
C_CONV = 3
D_QK_DIM = 64
D_V_DIM = 128
D_HEADS = 4
N_GROUPS = 4
EXPERTS_PER_GROUP = 8
N_EXPERTS = 32
D_EXPERT = 256
PAIRS_PER_GROUP = EXPERTS_PER_GROUP * (EXPERTS_PER_GROUP - 1) // 2
N_BUCKETS = N_GROUPS * PAIRS_PER_GROUP

V7X_LANES = 128
V7X_SUBLANES = 8
V7X_VMEM_LIMIT_BYTES = 56 * 1024 * 1024

PROJ_TM = 512
ATT_TQ = 256
NAT_KROWS = 12
CONV_CHUNK = 64
CONV_PAD = 16
MOE_TM = 256
PERM_CHUNK = 1024
ROUTER_GROUP_LANE0 = 0
ROUTER_EXPERT_LANE0 = 8

_PAIR_LO = np.array([i for i in range(8) for j in range(i + 1, 8)], np.int32)
_PAIR_HI = np.array([j for i in range(8) for j in range(i + 1, 8)], np.int32)
_BUCKET_E1 = np.concatenate([g * 8 + _PAIR_LO for g in range(N_GROUPS)]).astype(np.int32)
_BUCKET_E2 = np.concatenate([g * 8 + _PAIR_HI for g in range(N_GROUPS)]).astype(np.int32)


def _cparams(*sem):
    return pltpu.CompilerParams(dimension_semantics=tuple(sem), vmem_limit_bytes=V7X_VMEM_LIMIT_BYTES)


def _dot(a, b):
    return jnp.dot(a, b, preferred_element_type=F32)


def _dot_t(a, b):
    return lax.dot_general(a, b, (((1,), (1,)), ((), ())), preferred_element_type=F32)


def _split_bf16(a):
    hi = a.astype(BF16)
    lo = (a - hi.astype(F32)).astype(BF16)
    return hi, lo


def _dot3(a_hi, a_lo, b_hi, b_lo):
    return _dot(a_hi, b_hi) + _dot(a_lo, b_hi) + _dot(a_hi, b_lo)


def _sigmoid(x):
    return 1.0 / (1.0 + jnp.exp(-x))


def _load_token_tiles(ref):
    tm = ref.shape[0] // V7X_SUBLANES
    return jnp.concatenate([ref[pl.ds(c, tm, stride=V7X_SUBLANES), :] for c in range(V7X_SUBLANES)], axis=1)


def _store_token_tiles(ref, v):
    tm = v.shape[0]
    for c in range(V7X_SUBLANES):
        ref[pl.ds(c, tm, stride=V7X_SUBLANES), :] = v[:, c * V7X_LANES:(c + 1) * V7X_LANES]


def _rmsnorm_mod(x, g, scale, shift):
    y = x * lax.rsqrt(jnp.mean(x * x, axis=-1, keepdims=True) + NORM_EPS)
    return (y * g) * (1.0 + scale) + shift


def _mod_body(c_ref, w_ref, b_ref, o_ref):
    c = c_ref[...]
    s = c * _sigmoid(c)
    s_hi, s_lo = _split_bf16(s)
    w_hi, w_lo = _split_bf16(w_ref[0])
    o_ref[0] = _dot3(s_hi, s_lo, w_hi, w_lo) + b_ref[0]


def _modulation(cc, w_mod, b_mod):
    depth, d, n = w_mod.shape
    rows = cc.shape[0]
    tn = 1536
    return pl.pallas_call(
        _mod_body,
        out_shape=jax.ShapeDtypeStruct((depth, rows, n), F32),
        grid=(depth, n // tn),
        in_specs=[pl.BlockSpec((rows, d), lambda l, j: (0, 0)),
                  pl.BlockSpec((1, d, tn), lambda l, j: (l, 0, j)),
                  pl.BlockSpec((1, 1, tn), lambda l, j: (l, 0, j))],
        out_specs=pl.BlockSpec((1, rows, tn), lambda l, j: (l, 0, j)),
        compiler_params=_cparams("arbitrary", "arbitrary"),
        name="modulation",
    )(cc, w_mod, b_mod.reshape(depth, 1, n))


def _rope(v, cos, sin, lane):
    up = pltpu.roll(v, V7X_LANES - 16, axis=1)
    dn = pltpu.roll(v, 16, axis=1)
    sw = jnp.where((lane & 31) < 16, up, dn)
    return v * cos + sw * sin


def _proj_body(*refs, n_seg, res_idx, shift_idx, rope_segs, q_seg, write_x):
    refs = list(refs)
    x_ref = refs.pop(0)
    y_ref = refs.pop(0) if res_idx is not None else None
    rmod_ref = refs.pop(0) if res_idx is not None else None
    mod_ref = refs.pop(0)
    g_ref = refs.pop(0)
    w_ref = refs.pop(0)
    cos_ref = sin_ref = None
    if rope_segs:
        cos_ref = refs.pop(0)
        sin_ref = refs.pop(0)
    out_ref = refs.pop(0)
    xo_ref = refs.pop(0) if write_x else None

    x = x_ref[...]
    if y_ref is not None:
        x = x + rmod_ref[0, res_idx:res_idx + 1, :] * _load_token_tiles(y_ref)
        if write_x:
            xo_ref[...] = x
    h = _rmsnorm_mod(x, g_ref[...], mod_ref[0, shift_idx + 1:shift_idx + 2, :],
                     mod_ref[0, shift_idx:shift_idx + 1, :]).astype(BF16)
    for s in range(n_seg):
        o = _dot(h, w_ref[:, s * SEG:(s + 1) * SEG])
        if s in rope_segs:
            lane = lax.broadcasted_iota(jnp.int32, (o.shape[0], V7X_LANES), 1)
            cos = cos_ref[...]
            sin = sin_ref[...]
            if s == q_seg:
                o = o * (D_QK_DIM ** -0.5)
            o = jnp.concatenate(
                [_rope(o[:, c * V7X_LANES:(c + 1) * V7X_LANES], cos, sin, lane) for c in range(SEG // V7X_LANES)],
                axis=1)
        out_ref[:, s * SEG:(s + 1) * SEG] = o.astype(BF16)


def _proj(x, mod, g, w, *, seq, mod_row, y=None, res_mod=None, res_idx=None, shift_idx=0, rope=None, rope_segs=(), q_seg=None,
          write_x=False, name="proj"):
    t, d = x.shape
    n = w.shape[1]
    tm = min(PROJ_TM, t if seq is None else seq)
    tiles_per_seq = None if seq is None else seq // tm
    if seq is None:
        mod_map = lambda i: (mod_row, 0, 0)
    else:
        mod_map = lambda i: (i // tiles_per_seq, 0, 0)
    args = [x]
    specs = [pl.BlockSpec((tm, d), lambda i: (i, 0))]
    if y is not None:
        args += [y, res_mod]
        specs += [pl.BlockSpec((tm * V7X_SUBLANES, V7X_LANES), lambda i: (i, 0)),
                  pl.BlockSpec((1, N_MOD, d), mod_map)]
    args += [mod, g.reshape(1, d), w]
    specs += [pl.BlockSpec((1, N_MOD, d), mod_map), pl.BlockSpec((1, d), lambda i: (0, 0)),
              pl.BlockSpec((d, n), lambda i: (0, 0))]
    if rope_segs:
        args += [rope[0], rope[1]]
        specs += [pl.BlockSpec((tm, V7X_LANES), lambda i: (i % tiles_per_seq, 0))] * 2
    out_shape = [jax.ShapeDtypeStruct((t, n), BF16)]
    out_specs = [pl.BlockSpec((tm, n), lambda i: (i, 0))]
    if write_x:
        out_shape.append(jax.ShapeDtypeStruct((t, d), F32))
        out_specs.append(pl.BlockSpec((tm, d), lambda i: (i, 0)))
    body = functools.partial(_proj_body, n_seg=n // SEG, res_idx=res_idx if y is not None else None,
                             shift_idx=shift_idx, rope_segs=tuple(rope_segs), q_seg=q_seg, write_x=write_x)
    outs = pl.pallas_call(
        body, out_shape=out_shape, grid=(t // tm,), in_specs=specs, out_specs=out_specs,
        compiler_params=_cparams("arbitrary"), name=name,
    )(*args)
    return outs if write_x else outs[0]


def _rope_tables(n_tok):
    quarter = D_QK_DIM // 4
    t = np.arange(n_tok)
    row = (t // GRID_W).astype(np.float32)
    col = (t % GRID_W).astype(np.float32)
    inv = jnp.power(ROPE_BASE, -jnp.arange(quarter, dtype=F32) / quarter)
    ar = jnp.asarray(row)[:, None] * inv
    ac = jnp.asarray(col)[:, None] * inv
    cos64 = jnp.concatenate([jnp.cos(ar), jnp.cos(ar), jnp.cos(ac), jnp.cos(ac)], axis=-1)
    sin64 = jnp.concatenate([-jnp.sin(ar), jnp.sin(ar), -jnp.sin(ac), jnp.sin(ac)], axis=-1)
    return jnp.tile(cos64, (1, V7X_LANES // D_QK_DIM)), jnp.tile(sin64, (1, V7X_LANES // D_QK_DIM))


def _stack_halves(q):
    lane = lax.broadcasted_iota(jnp.int32, q.shape, 1)
    zero = jnp.zeros_like(q)
    return jnp.concatenate([jnp.where(lane < 64, q, zero), jnp.where(lane >= 64, q, zero)], axis=0)


def _softmax_parts(scores):
    m = functools.reduce(jnp.maximum, [jnp.max(s, axis=1, keepdims=True) for s in scores])
    ps = [jnp.exp(s - m) for s in scores]
    l = functools.reduce(lambda a, b: a + b, [jnp.sum(p, axis=1, keepdims=True) for p in ps])
    return ps, l


def _natten_body(ks_ref, pid_ref, q_ref, k_ref, v_ref, kc_ref, vc_ref, bias_ref, o_ref):
    j = pl.program_id(1)
    k0 = pl.multiple_of(ks_ref[j] * GRID_W, GRID_W)
    p = pid_ref[j]
    nk = NAT_KROWS * GRID_W
    tq = q_ref.shape[0]
    lane = lax.broadcasted_iota(jnp.int32, (tq, V7X_LANES), 1)
    for hp in range(A_HEADS // 2):
        cs = slice(hp * V7X_LANES, (hp + 1) * V7X_LANES)
        q2 = _stack_halves(q_ref[:, cs] * (A_HEAD_DIM ** -0.5))
        s_c = _dot_t(q2, kc_ref[:, cs])
        s_l = _dot_t(q2, k_ref[pl.ds(k0, nk), cs])
        bias = jnp.concatenate([bias_ref[p, 2 * hp], bias_ref[p, 2 * hp + 1]], axis=0).astype(F32)
        (p_c, p_l), l = _softmax_parts([s_c, s_l + bias])
        o2 = _dot(p_c.astype(BF16), vc_ref[:, cs]) + _dot(p_l.astype(BF16), v_ref[pl.ds(k0, nk), cs])
        o2 = o2 / l
        o_ref[:, cs] = jnp.where(lane < 64, o2[:tq], o2[tq:]).astype(BF16)


def _natten_tables(rows):
    q_rows = ATT_TQ // GRID_W
    nqb = rows // q_rows
    win_r = min(WIN_ROWS_MAX, rows)
    col = np.arange(GRID_W)
    c_start = np.clip(col - WIN_COLS // 2, 0, GRID_W - WIN_COLS)
    col_ok = (col[None, :] >= c_start[:, None]) & (col[None, :] < c_start[:, None] + WIN_COLS)
    dc = np.clip(col[None, :] - col[:, None], -(WIN_COLS - 1), WIN_COLS - 1) + WIN_COLS - 1
    ks_list, pats, pid = [], [], []
    for j in range(nqb):
        r0 = j * q_rows
        ks = int(np.clip(r0 - win_r // 2, 0, rows - NAT_KROWS))
        qr = r0 + np.arange(q_rows)
        rs = np.clip(qr - win_r // 2, 0, rows - win_r)
        kr = ks + np.arange(NAT_KROWS)
        row_ok = (kr[None, :] >= rs[:, None]) & (kr[None, :] < rs[:, None] + win_r)
        dr = np.clip(kr[None, :] - qr[:, None] + WIN_ROWS_MAX - 1, 0, 2 * WIN_ROWS_MAX - 2)
        key = (row_ok.tobytes(), (dr * row_ok).tobytes())
        for n, (k_, *_rest) in enumerate(pats):
            if k_ == key:
                pid.append(n)
                break
        else:
            pid.append(len(pats))
            pats.append((key, row_ok, dr))
        ks_list.append(ks)
    row_ok = np.stack([p_[1] for p_ in pats])
    dr = np.stack([p_[2] for p_ in pats])
    return np.array(ks_list, np.int32), np.array(pid, np.int32), row_ok, dr, col_ok, dc


def _natten_bias(rpb, row_ok, dr, col_ok, dc):
    n_pat, q_rows, k_rows = dr.shape
    n_dr, n_dc = rpb.shape[1], rpb.shape[2]
    oh_r = np.zeros((n_pat * q_rows * k_rows, n_dr), np.float32)
    oh_r[np.arange(oh_r.shape[0]), dr.reshape(-1)] = 1.0
    oh_c = np.zeros((n_dc, GRID_W * GRID_W), np.float32)
    oh_c[dc.reshape(-1), np.arange(GRID_W * GRID_W)] = 1.0
    hi = lax.Precision.HIGHEST
    t1 = jnp.einsum("mr,hrc->hmc", jnp.asarray(oh_r), rpb, precision=hi)
    t2 = jnp.einsum("hmc,cn->hmn", t1, jnp.asarray(oh_c), precision=hi)
    t2 = t2.reshape(rpb.shape[0], n_pat, q_rows, k_rows, GRID_W, GRID_W)
    valid = row_ok[None, :, :, :, None, None] & col_ok[None, None, None, None, :, :]
    bias = jnp.where(valid, t2, NEG_INF).transpose(1, 0, 2, 4, 3, 5)
    return bias.reshape(n_pat, rpb.shape[0], q_rows * GRID_W, k_rows * GRID_W).astype(BF16)


def _natten(px, pc, rpb, *, bsz, n_tok, ctx_len):
    rows = n_tok // GRID_W
    assert rows >= NAT_KROWS and n_tok % ATT_TQ == 0
    ks, pid, row_ok, dr, col_ok, dc = _natten_tables(rows)
    bias = _natten_bias(rpb, row_ok, dr, col_ok, dc)
    n_pat = bias.shape[0]
    nqb = n_tok // ATT_TQ
    nk = NAT_KROWS * GRID_W
    grid_spec = pltpu.PrefetchScalarGridSpec(
        num_scalar_prefetch=2, grid=(bsz, nqb),
        in_specs=[pl.BlockSpec((ATT_TQ, SEG), lambda b, j, *_: (b * nqb + j, 0)),
                  pl.BlockSpec((n_tok, SEG), lambda b, j, *_: (b, 1)),
                  pl.BlockSpec((n_tok, SEG), lambda b, j, *_: (b, 2)),
                  pl.BlockSpec((ctx_len, SEG), lambda b, j, *_: (b, 1)),
                  pl.BlockSpec((ctx_len, SEG), lambda b, j, *_: (b, 2)),
                  pl.BlockSpec((n_pat, A_HEADS, ATT_TQ, nk), lambda b, j, *_: (0, 0, 0, 0))],
        out_specs=pl.BlockSpec((ATT_TQ, SEG), lambda b, j, *_: (b * nqb + j, 0)))
    return pl.pallas_call(
        _natten_body, out_shape=jax.ShapeDtypeStruct((bsz * n_tok, SEG), BF16), grid_spec=grid_spec,
        compiler_params=_cparams("arbitrary", "arbitrary"), name="natten",
    )(jnp.asarray(ks), jnp.asarray(pid), px, px, px, pc, pc, bias)


def _ctxattn_body(q_ref, k_ref, v_ref, o_ref):
    tq = q_ref.shape[0]
    lane = lax.broadcasted_iota(jnp.int32, (tq, V7X_LANES), 1)
    for hp in range(A_HEADS // 2):
        cs = slice(hp * V7X_LANES, (hp + 1) * V7X_LANES)
        q2 = _stack_halves(q_ref[:, cs] * (A_HEAD_DIM ** -0.5))
        (p,), l = _softmax_parts([_dot_t(q2, k_ref[:, cs])])
        o2 = _dot(p.astype(BF16), v_ref[:, cs]) / l
        o_ref[:, cs] = jnp.where(lane < 64, o2[:tq], o2[tq:]).astype(BF16)


def _ctxattn(pc, *, bsz, ctx_len):
    return pl.pallas_call(
        _ctxattn_body, out_shape=jax.ShapeDtypeStruct((bsz * ctx_len, SEG), BF16), grid=(bsz,),
        in_specs=[pl.BlockSpec((ctx_len, SEG), lambda b: (b, 0)),
                  pl.BlockSpec((ctx_len, SEG), lambda b: (b, 1)),
                  pl.BlockSpec((ctx_len, SEG), lambda b: (b, 2))],
        out_specs=pl.BlockSpec((ctx_len, SEG), lambda b: (b, 0)),
        compiler_params=_cparams("arbitrary"), name="ctxattn",
    )(pc, pc, pc)


def _conv_accumulate(z_ref, w_ref, t0, taps):
    pad = taps // 2
    win = z_ref.at[pl.ds(t0, CONV_CHUNK + 2 * CONV_PAD)]
    acc = None
    for k in range(taps):
        term = win[pl.ds(CONV_PAD - pad + k, CONV_CHUNK), :] * w_ref[k:k + 1, :]
        acc = term if acc is None else acc + term
    return acc


def _conformer_body(u_ref, g_ref, w_ref, b_ref, lg_ref, lb_ref, o_ref, z_ref):
    n = u_ref.shape[0]
    halo = jnp.zeros((CONV_PAD, SEG), F32)
    z_ref[pl.ds(0, CONV_PAD), :] = halo
    z_ref[pl.ds(CONV_PAD + n, CONV_PAD), :] = halo
    z_ref[pl.ds(CONV_PAD, n), :] = u_ref[...].astype(F32) * _sigmoid(g_ref[...].astype(F32))

    def chunk(i, carry):
        t0 = pl.multiple_of(i * CONV_CHUNK, CONV_CHUNK)
        y = _conv_accumulate(z_ref, w_ref, t0, B_CONV) + b_ref[...]
        mu = jnp.mean(y, axis=-1, keepdims=True)
        yc = y - mu
        var = jnp.mean(yc * yc, axis=-1, keepdims=True)
        y = yc * lax.rsqrt(var + NORM_EPS) * lg_ref[...] + lb_ref[...]
        o_ref[pl.ds(t0, CONV_CHUNK), :] = (y * _sigmoid(y)).astype(BF16)
        return carry

    lax.fori_loop(0, n // CONV_CHUNK, chunk, 0)


def _conformer(p, dw, dw_b, ln_g, ln_b, *, bsz, seq):
    vec = lambda a: a.reshape(1, SEG)
    return pl.pallas_call(
        _conformer_body, out_shape=jax.ShapeDtypeStruct((bsz * seq, SEG), BF16), grid=(bsz,),
        in_specs=[pl.BlockSpec((seq, SEG), lambda b: (b, 3)),
                  pl.BlockSpec((seq, SEG), lambda b: (b, 4)),
                  pl.BlockSpec((B_CONV, SEG), lambda b: (0, 0))] + [pl.BlockSpec((1, SEG), lambda b: (0, 0))] * 3,
        out_specs=pl.BlockSpec((seq, SEG), lambda b: (b, 0)),
        scratch_shapes=[pltpu.VMEM((seq + 2 * CONV_PAD, SEG), F32)],
        compiler_params=_cparams("arbitrary"), name="conformer_conv",
    )(p, p, dw, vec(dw_b), vec(ln_g), vec(ln_b))


def _shortconv_body(bg_ref, cg_ref, u_ref, w_ref, o_ref, z_ref):
    n = u_ref.shape[0]
    halo = jnp.zeros((CONV_PAD, SEG), F32)
    z_ref[pl.ds(0, CONV_PAD), :] = halo
    z_ref[pl.ds(CONV_PAD + n, CONV_PAD), :] = halo
    z_ref[pl.ds(CONV_PAD, n), :] = cg_ref[...].astype(F32) * u_ref[...].astype(F32)

    def chunk(i, carry):
        t0 = pl.multiple_of(i * CONV_CHUNK, CONV_CHUNK)
        y = _conv_accumulate(z_ref, w_ref, t0, C_CONV)
        o_ref[pl.ds(t0, CONV_CHUNK), :] = (bg_ref[pl.ds(t0, CONV_CHUNK), :].astype(F32) * y).astype(BF16)
        return carry

    lax.fori_loop(0, n // CONV_CHUNK, chunk, 0)


def _shortconv(p, conv_w, *, bsz, seq):
    return pl.pallas_call(
        _shortconv_body, out_shape=jax.ShapeDtypeStruct((bsz * seq, SEG), BF16), grid=(bsz,),
        in_specs=[pl.BlockSpec((seq, SEG), lambda b: (b, 0)),
                  pl.BlockSpec((seq, SEG), lambda b: (b, 1)),
                  pl.BlockSpec((seq, SEG), lambda b: (b, 2)),
                  pl.BlockSpec((C_CONV, SEG), lambda b: (0, 0))],
        out_specs=pl.BlockSpec((seq, SEG), lambda b: (b, 0)),
        scratch_shapes=[pltpu.VMEM((seq + 2 * CONV_PAD, SEG), F32)],
        compiler_params=_cparams("arbitrary"), name="short_conv",
    )(p, p, p, conv_w)


def _diffattn_body(q_ref, k_ref, v_ref, kc_ref, vc_ref, lam_ref, g_ref, o_ref, *, lam_init):
    tq = q_ref.shape[0]
    lp = lam_ref[...]
    lam = (jnp.exp(jnp.sum(lp[0:1, :] * lp[1:2, :], axis=1, keepdims=True))
           - jnp.exp(jnp.sum(lp[2:3, :] * lp[3:4, :], axis=1, keepdims=True)) + lam_init)
    for h in range(D_HEADS):
        cs = slice(h * V7X_LANES, (h + 1) * V7X_LANES)
        q2 = _stack_halves(q_ref[:, cs])
        (p_c, p_x), l = _softmax_parts([_dot_t(q2, kc_ref[:, cs]), _dot_t(q2, k_ref[:, cs])])
        r = 1.0 / l
        a1 = r[:tq]
        a2 = r[tq:] * lam
        w_c = (p_c[:tq] * a1 - p_c[tq:] * a2).astype(BF16)
        w_x = (p_x[:tq] * a1 - p_x[tq:] * a2).astype(BF16)
        o = _dot(w_c, vc_ref[:, cs]) + _dot(w_x, v_ref[:, cs])
        o = o * lax.rsqrt(jnp.mean(o * o, axis=-1, keepdims=True) + NORM_EPS) * g_ref[...]
        o_ref[:, cs] = (o * (1.0 - lam_init)).astype(BF16)


def _diffattn(px, pc, lam_p, subln, lam_init, *, bsz, n_tok, ctx_len):
    nq = n_tok // ATT_TQ
    body = functools.partial(_diffattn_body, lam_init=lam_init)
    return pl.pallas_call(
        body, out_shape=jax.ShapeDtypeStruct((bsz * n_tok, SEG), BF16), grid=(bsz, nq),
        in_specs=[pl.BlockSpec((ATT_TQ, SEG), lambda b, j: (b * nq + j, 3)),
                  pl.BlockSpec((n_tok, SEG), lambda b, j: (b, 4)),
                  pl.BlockSpec((n_tok, SEG), lambda b, j: (b, 5)),
                  pl.BlockSpec((ctx_len, SEG), lambda b, j: (b, 0)),
                  pl.BlockSpec((ctx_len, SEG), lambda b, j: (b, 1)),
                  pl.BlockSpec((4, D_QK_DIM), lambda b, j: (0, 0)),
                  pl.BlockSpec((1, D_V_DIM), lambda b, j: (0, 0))],
        out_specs=pl.BlockSpec((ATT_TQ, SEG), lambda b, j: (b * nq + j, 0)),
        compiler_params=_cparams("arbitrary", "arbitrary"), name="diff_attn",
    )(px, px, px, pc, pc, lam_p, subln.reshape(1, D_V_DIM))


def _oproj_body(a_ref, b_ref, x_ref, mod_ref, g_ref, wo_ref, wrh_ref, wrl_ref, br_ref, cin_ref,
                x1_ref, h_ref, bucket_ref, rank_ref, cout_ref, carry_ref):
    i = pl.program_id(0)
    tm = x_ref.shape[0]

    @pl.when(i == 0)
    def _():
        carry_ref[...] = cin_ref[...]

    y = _dot(a_ref[...], wo_ref[0:SEG, :]) + _dot(b_ref[...], wo_ref[SEG:2 * SEG, :])
    x1 = x_ref[...] + mod_ref[0, 2:3, :] * y
    x1_ref[...] = x1
    h = _rmsnorm_mod(x1, g_ref[...], mod_ref[0, 4:5, :], mod_ref[0, 3:4, :])
    _store_token_tiles(h_ref, h)

    h_hi, h_lo = _split_bf16(h)
    logits = _dot3(h_hi, h_lo, wrh_ref[...], wrl_ref[...]) + br_ref[...]
    lane = lax.broadcasted_iota(jnp.int32, (tm, V7X_LANES), 1)
    none = jnp.int32(V7X_LANES)
    gl = jnp.where(lane < N_GROUPS, logits, -jnp.inf)
    gm = jnp.max(gl, axis=1, keepdims=True)
    g_sel = jnp.min(jnp.where(gl == gm, lane, none), axis=1, keepdims=True)
    e_lane0 = ROUTER_EXPERT_LANE0 + EXPERTS_PER_GROUP * g_sel
    in_group = (lane >= e_lane0) & (lane < e_lane0 + EXPERTS_PER_GROUP)
    el = jnp.where(in_group, logits, -jnp.inf)
    m1 = jnp.max(el, axis=1, keepdims=True)
    i1 = jnp.min(jnp.where(el == m1, lane, none), axis=1, keepdims=True)
    el2 = jnp.where(lane == i1, -jnp.inf, el)
    m2 = jnp.max(el2, axis=1, keepdims=True)
    i2 = jnp.min(jnp.where(el2 == m2, lane, none), axis=1, keepdims=True)
    lo = jnp.minimum(i1, i2) - e_lane0
    hi = jnp.maximum(i1, i2) - e_lane0
    bucket = g_sel * PAIRS_PER_GROUP + ((lo * (2 * EXPERTS_PER_GROUP - 1 - lo)) >> 1) + (hi - lo - 1)
    onehot = lane == bucket
    r_i = lax.broadcasted_iota(jnp.int32, (tm, tm), 0)
    c_i = lax.broadcasted_iota(jnp.int32, (tm, tm), 1)
    before = jnp.where(r_i > c_i, 1.0, 0.0).astype(BF16)
    oh = jnp.where(onehot, 1.0, 0.0)
    cum = _dot(before, oh.astype(BF16)) + carry_ref[...]
    rank = jnp.sum(jnp.where(onehot, cum, 0.0), axis=1, keepdims=True)
    bucket_ref[...] = bucket
    rank_ref[...] = rank.astype(jnp.int32)
    carry_ref[...] = carry_ref[...] + jnp.sum(oh, axis=0, keepdims=True)

    @pl.when(i == pl.num_programs(0) - 1)
    def _():
        cout_ref[...] = carry_ref[...]


def _oproj(a, b, x, mod, g, wo, wr_hi, wr_lo, br, counts_in, *, seq, mod_row, name="oproj"):
    t, d = x.shape
    tm = min(PROJ_TM, t if seq is None else seq)
    if seq is None:
        mod_map = lambda i: (mod_row, 0, 0)
    else:
        tiles_per_seq = seq // tm
        mod_map = lambda i: (i // tiles_per_seq, 0, 0)
    row = lambda i: (i, 0)
    const = lambda i: (0, 0)
    return pl.pallas_call(
        _oproj_body,
        out_shape=[jax.ShapeDtypeStruct((t, d), F32), jax.ShapeDtypeStruct((t * V7X_SUBLANES, V7X_LANES), F32),
                   jax.ShapeDtypeStruct((t, 1), jnp.int32), jax.ShapeDtypeStruct((t, 1), jnp.int32),
                   jax.ShapeDtypeStruct((1, V7X_LANES), F32)],
        grid=(t // tm,),
        in_specs=[pl.BlockSpec((tm, SEG), row), pl.BlockSpec((tm, SEG), row), pl.BlockSpec((tm, d), row),
                  pl.BlockSpec((1, N_MOD, d), mod_map), pl.BlockSpec((1, d), const),
                  pl.BlockSpec((2 * SEG, d), const), pl.BlockSpec((d, V7X_LANES), const),
                  pl.BlockSpec((d, V7X_LANES), const), pl.BlockSpec((1, V7X_LANES), const),
                  pl.BlockSpec((1, V7X_LANES), const)],
        out_specs=[pl.BlockSpec((tm, d), row), pl.BlockSpec((tm * V7X_SUBLANES, V7X_LANES), row),
                   pl.BlockSpec((tm, 1), row),
                   pl.BlockSpec((tm, 1), row), pl.BlockSpec((1, V7X_LANES), const)],
        scratch_shapes=[pltpu.VMEM((1, V7X_LANES), F32)],
        compiler_params=_cparams("arbitrary"), name=name,
    )(a, b, x, mod, g.reshape(1, d), wo, wr_hi, wr_lo, br, counts_in)


def _router_weights(wg, bg, we, be):
    d = wg.shape[0]
    w = jnp.zeros((d, V7X_LANES), F32)
    w = w.at[:, ROUTER_GROUP_LANE0:ROUTER_GROUP_LANE0 + N_GROUPS].set(wg)
    w = w.at[:, ROUTER_EXPERT_LANE0:ROUTER_EXPERT_LANE0 + N_EXPERTS].set(we)
    b = jnp.zeros((1, V7X_LANES), F32)
    b = b.at[0, ROUTER_GROUP_LANE0:ROUTER_GROUP_LANE0 + N_GROUPS].set(bg)
    b = b.at[0, ROUTER_EXPERT_LANE0:ROUTER_EXPERT_LANE0 + N_EXPERTS].set(be)
    w_hi = w.astype(BF16)
    w_lo = (w - w_hi.astype(F32)).astype(BF16)
    return w_hi, w_lo, b


def _token_tile(ref, t):
    return ref.at[pl.ds(pl.multiple_of(t * V7X_SUBLANES, V7X_SUBLANES), V7X_SUBLANES)]


def _scatter_body(off_ref, bucket_ref, rank_ref, rows_ref, sorted_hbm, out_hbm, sem, *, chunk):
    del sorted_hbm

    def start(t, carry):
        p = off_ref[bucket_ref[t]] + rank_ref[t]
        pltpu.make_async_copy(_token_tile(rows_ref, t), _token_tile(out_hbm, p), sem).start()
        return carry

    lax.fori_loop(0, chunk, start, 0, unroll=8)
    pltpu.make_async_copy(rows_ref, out_hbm.at[pl.ds(0, chunk * V7X_SUBLANES)], sem).wait()


def _gather_body(off_ref, bucket_ref, rank_ref, sorted_hbm, out_ref, sem, *, chunk):
    def start(t, carry):
        p = off_ref[bucket_ref[t]] + rank_ref[t]
        pltpu.make_async_copy(_token_tile(sorted_hbm, p), _token_tile(out_ref, t), sem).start()
        return carry

    lax.fori_loop(0, chunk, start, 0, unroll=8)
    pltpu.make_async_copy(sorted_hbm.at[pl.ds(0, chunk * V7X_SUBLANES)], out_ref, sem).wait()


def _permute(offsets, bucket, rank, rows, sorted_rows, *, n_rows, scatter, name):
    chunk = min(PERM_CHUNK, n_rows)
    smem_blk = pl.BlockSpec((chunk,), lambda i, *_: (i,), memory_space=pltpu.SMEM)
    any_spec = pl.BlockSpec(memory_space=pl.ANY)
    vmem_blk = pl.BlockSpec((chunk * V7X_SUBLANES, V7X_LANES), lambda i, *_: (i, 0))
    if scatter:
        body = functools.partial(_scatter_body, chunk=chunk)
        args = (offsets, bucket, rank, rows, sorted_rows)
        in_specs = [smem_blk, smem_blk, vmem_blk, any_spec]
        out_shape = jax.ShapeDtypeStruct(sorted_rows.shape, sorted_rows.dtype)
        out_specs = any_spec
        aliases = {4: 0}
    else:
        body = functools.partial(_gather_body, chunk=chunk)
        args = (offsets, bucket, rank, sorted_rows)
        in_specs = [smem_blk, smem_blk, any_spec]
        out_shape = jax.ShapeDtypeStruct((n_rows * V7X_SUBLANES, V7X_LANES), sorted_rows.dtype)
        out_specs = vmem_blk
        aliases = {}
    grid_spec = pltpu.PrefetchScalarGridSpec(
        num_scalar_prefetch=1, grid=(n_rows // chunk,), in_specs=in_specs, out_specs=out_specs,
        scratch_shapes=[pltpu.SemaphoreType.DMA])
    return pl.pallas_call(
        body, out_shape=out_shape, grid_spec=grid_spec, input_output_aliases=aliases,
        compiler_params=pltpu.CompilerParams(dimension_semantics=("arbitrary",), has_side_effects=True,
                                             vmem_limit_bytes=V7X_VMEM_LIMIT_BYTES),
        name=name,
    )(*args)


def _moe_body(te1_ref, te2_ref, tval_ref, trow_ref, x_ref, w1a_ref, w1b_ref, w3a_ref, w3b_ref, w2a_ref, w2b_ref,
              wr_ref, br_ref, y_ref):
    i = pl.program_id(0)

    @pl.when(tval_ref[i] == 0)
    def _():
        y_ref[...] = jnp.zeros_like(y_ref)

    @pl.when(tval_ref[i] > 0)
    def _():
        tm = x_ref.shape[0] // V7X_SUBLANES
        x = _load_token_tiles(x_ref).astype(BF16)
        e1 = te1_ref[i]
        e2 = te2_ref[i]
        grp = e1 >> 3
        logits = _dot(x, wr_ref[...]) + br_ref[...]
        lane = lax.broadcasted_iota(jnp.int32, (tm, V7X_LANES), 1)
        pick = lambda idx: jnp.sum(jnp.where(lane == idx, logits, 0.0), axis=1, keepdims=True)
        gl = jnp.where(lane < N_GROUPS, logits, -jnp.inf)
        gm = jnp.max(gl, axis=1, keepdims=True)
        gz = jnp.sum(jnp.exp(gl - gm), axis=1, keepdims=True)
        g_w = jnp.exp(pick(ROUTER_GROUP_LANE0 + grp) - gm) / gz
        l1 = pick(ROUTER_EXPERT_LANE0 + e1)
        l2 = pick(ROUTER_EXPERT_LANE0 + e2)
        m = jnp.maximum(l1, l2)
        p1 = jnp.exp(l1 - m)
        p2 = jnp.exp(l2 - m)
        c1 = g_w * p1 / (p1 + p2)
        c2 = g_w * p2 / (p1 + p2)

        def expert(w1_ref, w3_ref, w2_ref):
            h = _dot(x, w1_ref[0])
            hid = (h * _sigmoid(h)) * _dot(x, w3_ref[0])
            return _dot(hid.astype(BF16), w2_ref[0])

        _store_token_tiles(y_ref, c1 * expert(w1a_ref, w3a_ref, w2a_ref) + c2 * expert(w1b_ref, w3b_ref, w2b_ref))


def _moe_sorted(xs, te1, te2, tval, trow, w1, w3, w2, wr_hi, br):
    d = w1.shape[1]
    n_tiles = xs.shape[0] // (MOE_TM * V7X_SUBLANES)
    tile_blk = (MOE_TM * V7X_SUBLANES, V7X_LANES)
    xmap = lambda i, te1, te2, tval, trow: (trow[i], 0)
    wa = lambda i, te1, te2, tval, trow: (te1[i], 0, 0)
    wb = lambda i, te1, te2, tval, trow: (te2[i], 0, 0)
    const = lambda i, *_: (0, 0)
    grid_spec = pltpu.PrefetchScalarGridSpec(
        num_scalar_prefetch=4, grid=(n_tiles,),
        in_specs=[pl.BlockSpec(tile_blk, xmap),
                  pl.BlockSpec((1, d, D_EXPERT), wa), pl.BlockSpec((1, d, D_EXPERT), wb),
                  pl.BlockSpec((1, d, D_EXPERT), wa), pl.BlockSpec((1, d, D_EXPERT), wb),
                  pl.BlockSpec((1, D_EXPERT, d), wa), pl.BlockSpec((1, D_EXPERT, d), wb),
                  pl.BlockSpec((d, V7X_LANES), const), pl.BlockSpec((1, V7X_LANES), const)],
        out_specs=pl.BlockSpec(tile_blk, lambda i, *_: (i, 0)))
    return pl.pallas_call(
        _moe_body, out_shape=jax.ShapeDtypeStruct(xs.shape, F32), grid_spec=grid_spec,
        compiler_params=_cparams("arbitrary"), name="moe_experts",
    )(te1, te2, tval, trow, xs, w1, w1, w3, w3, w2, w2, wr_hi, br)


def _moe(h_parts, route_parts, counts, w1, w3, w2, wr_hi, br):
    n_rows = [h.shape[0] // V7X_SUBLANES for h in h_parts]
    n_tiles = sum(n_rows) // MOE_TM + N_BUCKETS
    cnt = counts[0, :N_BUCKETS].astype(jnp.int32)
    tiles_b = (cnt + MOE_TM - 1) // MOE_TM
    tile_end = jnp.cumsum(tiles_b)
    n_valid = tile_end[-1]
    offsets = jnp.zeros((V7X_LANES,), jnp.int32).at[:N_BUCKETS].set((tile_end - tiles_b) * MOE_TM)
    tile_id = jnp.arange(n_tiles, dtype=jnp.int32)
    tval = (tile_id < n_valid).astype(jnp.int32)
    trow = jnp.minimum(tile_id, n_valid - 1)
    tbucket = jnp.sum((trow[:, None] >= tile_end[None, :]).astype(jnp.int32), axis=1)
    tbucket = jnp.minimum(tbucket, N_BUCKETS - 1)
    te1 = jnp.asarray(_BUCKET_E1)[tbucket]
    te2 = jnp.asarray(_BUCKET_E2)[tbucket]

    xs = jnp.zeros((n_tiles * MOE_TM * V7X_SUBLANES, V7X_LANES), F32)
    for h, n, (bucket, rank) in zip(h_parts, n_rows, route_parts):
        xs = _permute(offsets, bucket, rank, h, xs, n_rows=n, scatter=True, name="moe_scatter_rows")
    ys = _moe_sorted(xs, te1, te2, tval, trow, w1, w3, w2, wr_hi, br)
    return [_permute(offsets, bucket, rank, None, ys, n_rows=n, scatter=False, name="moe_gather_rows")
            for n, (bucket, rank) in zip(n_rows, route_parts)]


def _final_body(x_ref, y_ref, mod_ref, g_ref, o_ref):
    x = x_ref[...] + mod_ref[0, 5:6, :] * _load_token_tiles(y_ref)
    o_ref[...] = x * lax.rsqrt(jnp.mean(x * x, axis=-1, keepdims=True) + NORM_EPS) * g_ref[...]


def _final(x, y, mod, g, *, seq):
    t, d = x.shape
    tm = min(PROJ_TM, seq)
    tiles_per_seq = seq // tm
    row = lambda i: (i, 0)
    return pl.pallas_call(
        _final_body, out_shape=jax.ShapeDtypeStruct((t, d), F32), grid=(t // tm,),
        in_specs=[pl.BlockSpec((tm, d), row), pl.BlockSpec((tm * V7X_SUBLANES, V7X_LANES), row),
                  pl.BlockSpec((1, N_MOD, d), lambda i: (i // tiles_per_seq, 0, 0)),
                  pl.BlockSpec((1, d), lambda i: (0, 0))],
        out_specs=pl.BlockSpec((tm, d), row),
        compiler_params=_cparams("arbitrary"), name="final_norm",
    )(x, y, mod, g.reshape(1, d))


def kernel(x, c, ctx, c_ctx, w_mod, b_mod, norm_mix, norm_ffn, norm_final, even_w_in, even_w_out, even_rpb,
           even_dw, even_dw_b, even_ln_g, even_ln_b, odd_w_in, odd_w_out, odd_conv, odd_lambda, odd_subln,
           moe_wg, moe_bg, moe_we, moe_be, moe_w1, moe_w3, moe_w2):
    bsz, n_tok, d = x.shape
    ctx_len = ctx.shape[1]
    depth = w_mod.shape[0]
    assert depth == 2 and d == D_MODEL
    xs = x.reshape(bsz * n_tok, d)
    cs = ctx.reshape(bsz * ctx_len, d)

    mod_rows = -(-(bsz + 1) // 8) * 8
    cc = jnp.zeros((mod_rows, d), F32).at[:bsz].set(c).at[bsz].set(c_ctx)
    mod = _modulation(cc, w_mod, b_mod).reshape(depth, mod_rows, N_MOD, d)
    bf = lambda a: a.astype(BF16)
    zero_counts = jnp.zeros((1, V7X_LANES), F32)

    l = 0
    w_in = bf(even_w_in[0])
    w_out = bf(even_w_out[0])
    px = _proj(xs, mod[l], norm_mix[l], w_in, seq=n_tok, mod_row=None, name="proj_even_x")
    pc = _proj(cs, mod[l], norm_mix[l], w_in, seq=None, mod_row=bsz, name="proj_even_c")
    a_x = _natten(px, pc, even_rpb[0], bsz=bsz, n_tok=n_tok, ctx_len=ctx_len)
    a_c = _ctxattn(pc, bsz=bsz, ctx_len=ctx_len)
    b_x = _conformer(px, even_dw[0], even_dw_b[0], even_ln_g[0], even_ln_b[0], bsz=bsz, seq=n_tok)
    b_c = _conformer(pc, even_dw[0], even_dw_b[0], even_ln_g[0], even_ln_b[0], bsz=bsz, seq=ctx_len)
    wr_hi, wr_lo, br = _router_weights(moe_wg[l], moe_bg[l], moe_we[l], moe_be[l])
    x1, hx, bkt_x, rnk_x, counts = _oproj(a_x, b_x, xs, mod[l], norm_ffn[l], w_out, wr_hi, wr_lo, br, zero_counts,
                                          seq=n_tok, mod_row=None, name="oproj_even_x")
    c1, hc, bkt_c, rnk_c, counts = _oproj(a_c, b_c, cs, mod[l], norm_ffn[l], w_out, wr_hi, wr_lo, br, counts,
                                          seq=None, mod_row=bsz, name="oproj_even_c")
    y_x, y_c = _moe([hx, hc], [(bkt_x.reshape(-1), rnk_x.reshape(-1)), (bkt_c.reshape(-1), rnk_c.reshape(-1))],
                    counts, bf(moe_w1[l]), bf(moe_w3[l]), bf(moe_w2[l]), wr_hi, br)

    l = 1
    lam_init = 0.8 - 0.6 * math.exp(-0.3 * l)
    w_in = bf(odd_w_in[0])
    w_out = bf(odd_w_out[0])
    rope = _rope_tables(n_tok)
    px, x2 = _proj(x1, mod[l], norm_mix[l], w_in, seq=n_tok, mod_row=None, y=y_x, res_mod=mod[l - 1], res_idx=5,
                   rope=rope, rope_segs=(3, 4), q_seg=3, write_x=True, name="proj_odd_x")
    pc = _proj(c1, mod[l], norm_mix[l], w_in[:, 4 * SEG:], seq=None, mod_row=bsz, y=y_c, res_mod=mod[l - 1],
               res_idx=5, name="proj_odd_c")
    s_x = _shortconv(px, odd_conv[0], bsz=bsz, seq=n_tok)
    d_x = _diffattn(px, pc, odd_lambda[0], odd_subln[0], lam_init, bsz=bsz, n_tok=n_tok, ctx_len=ctx_len)
    wr_hi, wr_lo, br = _router_weights(moe_wg[l], moe_bg[l], moe_we[l], moe_be[l])
    x3, hx, bkt_x, rnk_x, counts = _oproj(s_x, d_x, x2, mod[l], norm_ffn[l], w_out, wr_hi, wr_lo, br, zero_counts,
                                          seq=n_tok, mod_row=None, name="oproj_odd_x")
    (y_x,) = _moe([hx], [(bkt_x.reshape(-1), rnk_x.reshape(-1))], counts,
                  bf(moe_w1[l]), bf(moe_w3[l]), bf(moe_w2[l]), wr_hi, br)
    out = _final(x3, y_x, mod[l], norm_final, seq=n_tok)
    return out.reshape(bsz, n_tok, d)
```

```python
import functools
import math

import numpy as np
import jax
import jax.numpy as jnp
from jax import lax
from jax.experimental import pallas as pl
from jax.experimental.pallas import tpu as pltpu

F32 = jnp.float32
BF16 = jnp.bfloat16

D_MODEL = 1024
GRID_W = 64
N_MOD = 6
NORM_EPS = 1e-6
NEG_INF = -1e30
ROPE_BASE = 10000.0
SEG = 512
A_HEAD_DIM = 64
A_HEADS = 8
WIN_ROWS_MAX = 8
WIN_COLS = 16
B_CONV = 31
C_CONV = 3
D_QK_DIM = 64
D_V_DIM = 128
D_HEADS = 4
N_GROUPS = 4
EXPERTS_PER_GROUP = 8
N_EXPERTS = 32
D_EXPERT = 256
LOG2E = math.log2(math.e)
QUERY_SCALE = A_HEAD_DIM ** -0.5 * LOG2E
PAIRS_PER_GROUP = EXPERTS_PER_GROUP * (EXPERTS_PER_GROUP - 1) // 2
N_BUCKETS = N_GROUPS * PAIRS_PER_GROUP

V7X_LANES = 128
V7X_SUBLANES = 8
V7X_VMEM_LIMIT_BYTES = 56 * 1024 * 1024

PROJ_TM = 512
ATT_TQ = 256
NAT_KROWS = 12
CONV_CHUNK = 64
CONV_PAD = 16
MOE_TM = 256
PERM_CHUNK = 1024
ROUTER_GROUP_LANE0 = 0
ROUTER_EXPERT_LANE0 = 8

_PAIR_LO = np.array([i for i in range(8) for j in range(i + 1, 8)], np.int32)
_PAIR_HI = np.array([j for i in range(8) for j in range(i + 1, 8)], np.int32)
_BUCKET_E1 = np.concatenate([g * 8 + _PAIR_LO for g in range(N_GROUPS)]).astype(np.int32)
_BUCKET_E2 = np.concatenate([g * 8 + _PAIR_HI for g in range(N_GROUPS)]).astype(np.int32)


def _cparams(*sem):
    return pltpu.CompilerParams(dimension_semantics=tuple(sem), vmem_limit_bytes=V7X_VMEM_LIMIT_BYTES)


def _dot(a, b):
    return jnp.dot(a, b, preferred_element_type=F32)


def _dot_t(a, b):
    return lax.dot_general(a, b, (((1,), (1,)), ((), ())), preferred_element_type=F32)


def _split_bf16(a):
    hi = a.astype(BF16)
    lo = (a - hi.astype(F32)).astype(BF16)
    return hi, lo


def _dot3(a_hi, a_lo, b_hi, b_lo):
    return _dot(a_hi, b_hi) + _dot(a_lo, b_hi) + _dot(a_hi, b_lo)


def _sigmoid(x):
    return 1.0 / (1.0 + jnp.exp(-x))


def _load_token_tiles(ref):
    tm = ref.shape[0] // V7X_SUBLANES
    return jnp.concatenate([ref[pl.ds(c, tm, stride=V7X_SUBLANES), :] for c in range(V7X_SUBLANES)], axis=1)


def _store_token_tiles(ref, v):
    tm = v.shape[0]
    for c in range(V7X_SUBLANES):
        ref[pl.ds(c, tm, stride=V7X_SUBLANES), :] = v[:, c * V7X_LANES:(c + 1) * V7X_LANES]


def _rmsnorm_mod(x, g, scale, shift):
    y = x * lax.rsqrt(jnp.mean(x * x, axis=-1, keepdims=True) + NORM_EPS)
    return (y * g) * (1.0 + scale) + shift


def _mod_body(c_ref, w_ref, b_ref, o_ref):
    c = c_ref[...]
    s = c * _sigmoid(c)
    s_hi, s_lo = _split_bf16(s)
    w_hi, w_lo = _split_bf16(w_ref[0])
    o_ref[0] = _dot3(s_hi, s_lo, w_hi, w_lo) + b_ref[0]


def _modulation(cc, w_mod, b_mod):
    depth, d, n = w_mod.shape
    rows = cc.shape[0]
    tn = 1536
    return pl.pallas_call(
        _mod_body,
        out_shape=jax.ShapeDtypeStruct((depth, rows, n), F32),
        grid=(depth, n // tn),
        in_specs=[pl.BlockSpec((rows, d), lambda l, j: (0, 0)),
                  pl.BlockSpec((1, d, tn), lambda l, j: (l, 0, j)),
                  pl.BlockSpec((1, 1, tn), lambda l, j: (l, 0, j))],
        out_specs=pl.BlockSpec((1, rows, tn), lambda l, j: (l, 0, j)),
        compiler_params=_cparams("arbitrary", "arbitrary"),
        name="modulation",
    )(cc, w_mod, b_mod.reshape(depth, 1, n))


def _rope(v, cos, sin, lane):
    up = pltpu.roll(v, V7X_LANES - 16, axis=1)
    dn = pltpu.roll(v, 16, axis=1)
    sw = jnp.where((lane & 31) < 16, up, dn)
    return v * cos + sw * sin


def _proj_body(*refs, n_seg, res_idx, shift_idx, rope_segs, q_seg, write_x):
    refs = list(refs)
    x_ref = refs.pop(0)
    y_ref = refs.pop(0) if res_idx is not None else None
    rmod_ref = refs.pop(0) if res_idx is not None else None
    mod_ref = refs.pop(0)
    g_ref = refs.pop(0)
    w_ref = refs.pop(0)
    cos_ref = sin_ref = None
    if rope_segs:
        cos_ref = refs.pop(0)
        sin_ref = refs.pop(0)
    out_ref = refs.pop(0)
    xo_ref = refs.pop(0) if write_x else None

    x = x_ref[...]
    if y_ref is not None:
        x = x + rmod_ref[0, res_idx:res_idx + 1, :] * _load_token_tiles(y_ref)
        if write_x:
            xo_ref[...] = x
    h = _rmsnorm_mod(x, g_ref[...], mod_ref[0, shift_idx + 1:shift_idx + 2, :],
                     mod_ref[0, shift_idx:shift_idx + 1, :]).astype(BF16)
    for s in range(n_seg):
        o = _dot(h, w_ref[:, s * SEG:(s + 1) * SEG])
        if s == q_seg:
            o = o * QUERY_SCALE
        if s in rope_segs:
            lane = lax.broadcasted_iota(jnp.int32, (o.shape[0], V7X_LANES), 1)
            cos = cos_ref[...]
            sin = sin_ref[...]
            o = jnp.concatenate(
                [_rope(o[:, c * V7X_LANES:(c + 1) * V7X_LANES], cos, sin, lane) for c in range(SEG // V7X_LANES)],
                axis=1)
        out_ref[:, s * SEG:(s + 1) * SEG] = o.astype(BF16)


def _proj(x, mod, g, w, *, seq, mod_row, y=None, res_mod=None, res_idx=None, shift_idx=0, rope=None, rope_segs=(), q_seg=None,
          write_x=False, name="proj"):
    t, d = x.shape
    n = w.shape[1]
    tm = min(PROJ_TM, t if seq is None else seq)
    tiles_per_seq = None if seq is None else seq // tm
    if seq is None:
        mod_map = lambda i: (mod_row, 0, 0)
    else:
        mod_map = lambda i: (i // tiles_per_seq, 0, 0)
    args = [x]
    specs = [pl.BlockSpec((tm, d), lambda i: (i, 0))]
    if y is not None:
        args += [y, res_mod]
        specs += [pl.BlockSpec((tm * V7X_SUBLANES, V7X_LANES), lambda i: (i, 0)),
                  pl.BlockSpec((1, N_MOD, d), mod_map)]
    args += [mod, g.reshape(1, d), w]
    specs += [pl.BlockSpec((1, N_MOD, d), mod_map), pl.BlockSpec((1, d), lambda i: (0, 0)),
              pl.BlockSpec((d, n), lambda i: (0, 0))]
    if rope_segs:
        args += [rope[0], rope[1]]
        specs += [pl.BlockSpec((tm, V7X_LANES), lambda i: (i % tiles_per_seq, 0))] * 2
    out_shape = [jax.ShapeDtypeStruct((t, n), BF16)]
    out_specs = [pl.BlockSpec((tm, n), lambda i: (i, 0))]
    if write_x:
        out_shape.append(jax.ShapeDtypeStruct((t, d), F32))
        out_specs.append(pl.BlockSpec((tm, d), lambda i: (i, 0)))
    body = functools.partial(_proj_body, n_seg=n // SEG, res_idx=res_idx if y is not None else None,
                             shift_idx=shift_idx, rope_segs=tuple(rope_segs), q_seg=q_seg, write_x=write_x)
    outs = pl.pallas_call(
        body, out_shape=out_shape, grid=(t // tm,), in_specs=specs, out_specs=out_specs,
        compiler_params=_cparams("arbitrary"), name=name,
    )(*args)
    return outs if write_x else outs[0]


def _rope_tables(n_tok):
    quarter = D_QK_DIM // 4
    t = np.arange(n_tok)
    row = (t // GRID_W).astype(np.float32)
    col = (t % GRID_W).astype(np.float32)
    inv = jnp.power(ROPE_BASE, -jnp.arange(quarter, dtype=F32) / quarter)
    ar = jnp.asarray(row)[:, None] * inv
    ac = jnp.asarray(col)[:, None] * inv
    cos64 = jnp.concatenate([jnp.cos(ar), jnp.cos(ar), jnp.cos(ac), jnp.cos(ac)], axis=-1)
    sin64 = jnp.concatenate([-jnp.sin(ar), jnp.sin(ar), -jnp.sin(ac), jnp.sin(ac)], axis=-1)
    return jnp.tile(cos64, (1, V7X_LANES // D_QK_DIM)), jnp.tile(sin64, (1, V7X_LANES // D_QK_DIM))


def _stack_halves(q):
    lane = lax.broadcasted_iota(jnp.int32, q.shape, 1)
    zero = jnp.zeros_like(q)
    return jnp.concatenate([jnp.where(lane < 64, q, zero), jnp.where(lane >= 64, q, zero)], axis=0)


def _attend(scores, values):
    m = functools.reduce(jnp.maximum, [jnp.max(s, axis=1, keepdims=True) for s in scores])
    acc = None
    for s, v in zip(scores, values):
        p = jnp.exp2(s - m).astype(BF16)
        lane = lax.broadcasted_iota(jnp.int32, v.shape, 1)
        ones_col = jnp.where(lane == 0, 1.0, 0.0).astype(BF16)
        o = _dot(p, jnp.concatenate([v, ones_col], axis=1))
        acc = o if acc is None else acc + o
    return acc[:, :V7X_LANES], acc[:, V7X_LANES:V7X_LANES + 1]


def _natten_body(ks_ref, pid_ref, q_ref, k_ref, v_ref, kc_ref, vc_ref, bias_ref, o_ref):
    j = pl.program_id(1)
    k0 = pl.multiple_of(ks_ref[j] * GRID_W, GRID_W)
    p = pid_ref[j]
    nk = NAT_KROWS * GRID_W
    tq = q_ref.shape[0]
    lane = lax.broadcasted_iota(jnp.int32, (tq, V7X_LANES), 1)
    for hp in range(A_HEADS // 2):
        cs = slice(hp * V7X_LANES, (hp + 1) * V7X_LANES)
        q2 = _stack_halves(q_ref[:, cs])
        s_c = _dot_t(q2, kc_ref[:, cs])
        s_l = _dot_t(q2, k_ref[pl.ds(k0, nk), cs])
        bias = jnp.concatenate([bias_ref[p, 2 * hp], bias_ref[p, 2 * hp + 1]], axis=0).astype(F32)
        o2, l = _attend([s_c, s_l + bias], [vc_ref[:, cs], v_ref[pl.ds(k0, nk), cs]])
        o2 = o2 / l
        o_ref[:, cs] = jnp.where(lane < 64, o2[:tq], o2[tq:]).astype(BF16)


def _natten_tables(rows):
    q_rows = ATT_TQ // GRID_W
    nqb = rows // q_rows
    win_r = min(WIN_ROWS_MAX, rows)
    col = np.arange(GRID_W)
    c_start = np.clip(col - WIN_COLS // 2, 0, GRID_W - WIN_COLS)
    col_ok = (col[None, :] >= c_start[:, None]) & (col[None, :] < c_start[:, None] + WIN_COLS)
    dc = np.clip(col[None, :] - col[:, None], -(WIN_COLS - 1), WIN_COLS - 1) + WIN_COLS - 1
    ks_list, pats, pid = [], [], []
    for j in range(nqb):
        r0 = j * q_rows
        ks = int(np.clip(r0 - win_r // 2, 0, rows - NAT_KROWS))
        qr = r0 + np.arange(q_rows)
        rs = np.clip(qr - win_r // 2, 0, rows - win_r)
        kr = ks + np.arange(NAT_KROWS)
        row_ok = (kr[None, :] >= rs[:, None]) & (kr[None, :] < rs[:, None] + win_r)
        dr = np.clip(kr[None, :] - qr[:, None] + WIN_ROWS_MAX - 1, 0, 2 * WIN_ROWS_MAX - 2)
        key = (row_ok.tobytes(), (dr * row_ok).tobytes())
        for n, (k_, *_rest) in enumerate(pats):
            if k_ == key:
                pid.append(n)
                break
        else:
            pid.append(len(pats))
            pats.append((key, row_ok, dr))
        ks_list.append(ks)
    row_ok = np.stack([p_[1] for p_ in pats])
    dr = np.stack([p_[2] for p_ in pats])
    return np.array(ks_list, np.int32), np.array(pid, np.int32), row_ok, dr, col_ok, dc


def _natten_bias(rpb, row_ok, dr, col_ok, dc):
    n_pat, q_rows, k_rows = dr.shape
    n_dr, n_dc = rpb.shape[1], rpb.shape[2]
    oh_r = np.zeros((n_pat * q_rows * k_rows, n_dr), np.float32)
    oh_r[np.arange(oh_r.shape[0]), dr.reshape(-1)] = 1.0
    oh_c = np.zeros((n_dc, GRID_W * GRID_W), np.float32)
    oh_c[dc.reshape(-1), np.arange(GRID_W * GRID_W)] = 1.0
    hi = lax.Precision.HIGHEST
    t1 = jnp.einsum("mr,hrc->hmc", jnp.asarray(oh_r), rpb, precision=hi)
    t2 = jnp.einsum("hmc,cn->hmn", t1, jnp.asarray(oh_c), precision=hi)
    t2 = t2.reshape(rpb.shape[0], n_pat, q_rows, k_rows, GRID_W, GRID_W)
    valid = row_ok[None, :, :, :, None, None] & col_ok[None, None, None, None, :, :]
    bias = jnp.where(valid, t2 * LOG2E, NEG_INF).transpose(1, 0, 2, 4, 3, 5)
    return bias.reshape(n_pat, rpb.shape[0], q_rows * GRID_W, k_rows * GRID_W).astype(BF16)


def _natten(px, pc, rpb, *, bsz, n_tok, ctx_len):
    rows = n_tok // GRID_W
    assert rows >= NAT_KROWS and n_tok % ATT_TQ == 0
    ks, pid, row_ok, dr, col_ok, dc = _natten_tables(rows)
    bias = _natten_bias(rpb, row_ok, dr, col_ok, dc)
    n_pat = bias.shape[0]
    nqb = n_tok // ATT_TQ
    nk = NAT_KROWS * GRID_W
    grid_spec = pltpu.PrefetchScalarGridSpec(
        num_scalar_prefetch=2, grid=(bsz, nqb),
        in_specs=[pl.BlockSpec((ATT_TQ, SEG), lambda b, j, *_: (b * nqb + j, 0)),
                  pl.BlockSpec((n_tok, SEG), lambda b, j, *_: (b, 1)),
                  pl.BlockSpec((n_tok, SEG), lambda b, j, *_: (b, 2)),
                  pl.BlockSpec((ctx_len, SEG), lambda b, j, *_: (b, 1)),
                  pl.BlockSpec((ctx_len, SEG), lambda b, j, *_: (b, 2)),
                  pl.BlockSpec((n_pat, A_HEADS, ATT_TQ, nk), lambda b, j, *_: (0, 0, 0, 0))],
        out_specs=pl.BlockSpec((ATT_TQ, SEG), lambda b, j, *_: (b * nqb + j, 0)))
    return pl.pallas_call(
        _natten_body, out_shape=jax.ShapeDtypeStruct((bsz * n_tok, SEG), BF16), grid_spec=grid_spec,
        compiler_params=_cparams("arbitrary", "arbitrary"), name="natten",
    )(jnp.asarray(ks), jnp.asarray(pid), px, px, px, pc, pc, bias)


def _ctxattn_body(q_ref, k_ref, v_ref, o_ref):
    tq = q_ref.shape[0]
    lane = lax.broadcasted_iota(jnp.int32, (tq, V7X_LANES), 1)
    for hp in range(A_HEADS // 2):
        cs = slice(hp * V7X_LANES, (hp + 1) * V7X_LANES)
        q2 = _stack_halves(q_ref[:, cs])
        o2, l = _attend([_dot_t(q2, k_ref[:, cs])], [v_ref[:, cs]])
        o2 = o2 / l
        o_ref[:, cs] = jnp.where(lane < 64, o2[:tq], o2[tq:]).astype(BF16)


def _ctxattn(pc, *, bsz, ctx_len):
    return pl.pallas_call(
        _ctxattn_body, out_shape=jax.ShapeDtypeStruct((bsz * ctx_len, SEG), BF16), grid=(bsz,),
        in_specs=[pl.BlockSpec((ctx_len, SEG), lambda b: (b, 0)),
                  pl.BlockSpec((ctx_len, SEG), lambda b: (b, 1)),
                  pl.BlockSpec((ctx_len, SEG), lambda b: (b, 2))],
        out_specs=pl.BlockSpec((ctx_len, SEG), lambda b: (b, 0)),
        compiler_params=_cparams("arbitrary"), name="ctxattn",
    )(pc, pc, pc)


def _conv_accumulate(z_ref, w_ref, t0, taps):
    base = CONV_PAD - taps // 2
    win = z_ref[pl.ds(t0, CONV_CHUNK + 2 * CONV_PAD), :]
    acc = None
    for b in range(V7X_SUBLANES):
        ks = [k for k in range(taps) if (base + k) % V7X_SUBLANES == b]
        if not ks:
            continue
        n_rows = V7X_SUBLANES * max((base + k) // V7X_SUBLANES for k in ks) + CONV_CHUNK
        shifted = win[b:b + n_rows, :]
        for k in ks:
            a0 = (base + k) // V7X_SUBLANES * V7X_SUBLANES
            term = shifted[a0:a0 + CONV_CHUNK, :] * w_ref[k:k + 1, :]
            acc = term if acc is None else acc + term
    return acc


def _conformer_body(u_ref, g_ref, w_ref, b_ref, lg_ref, lb_ref, o_ref, z_ref):
    n = u_ref.shape[0]
    halo = jnp.zeros((CONV_PAD, SEG), F32)
    z_ref[pl.ds(0, CONV_PAD), :] = halo
    z_ref[pl.ds(CONV_PAD + n, CONV_PAD), :] = halo
    z_ref[pl.ds(CONV_PAD, n), :] = u_ref[...].astype(F32) * _sigmoid(g_ref[...].astype(F32))

    def chunk(i, carry):
        t0 = pl.multiple_of(i * CONV_CHUNK, CONV_CHUNK)
        y = _conv_accumulate(z_ref, w_ref, t0, B_CONV) + b_ref[...]
        mu = jnp.mean(y, axis=-1, keepdims=True)
        yc = y - mu
        var = jnp.mean(yc * yc, axis=-1, keepdims=True)
        y = yc * lax.rsqrt(var + NORM_EPS) * lg_ref[...] + lb_ref[...]
        o_ref[pl.ds(t0, CONV_CHUNK), :] = (y * _sigmoid(y)).astype(BF16)
        return carry

    lax.fori_loop(0, n // CONV_CHUNK, chunk, 0)


def _conformer(p, dw, dw_b, ln_g, ln_b, *, bsz, seq):
    vec = lambda a: a.reshape(1, SEG)
    return pl.pallas_call(
        _conformer_body, out_shape=jax.ShapeDtypeStruct((bsz * seq, SEG), BF16), grid=(bsz,),
        in_specs=[pl.BlockSpec((seq, SEG), lambda b: (b, 3)),
                  pl.BlockSpec((seq, SEG), lambda b: (b, 4)),
                  pl.BlockSpec((B_CONV, SEG), lambda b: (0, 0))] + [pl.BlockSpec((1, SEG), lambda b: (0, 0))] * 3,
        out_specs=pl.BlockSpec((seq, SEG), lambda b: (b, 0)),
        scratch_shapes=[pltpu.VMEM((seq + 2 * CONV_PAD, SEG), F32)],
        compiler_params=_cparams("arbitrary"), name="conformer_conv",
    )(p, p, dw, vec(dw_b), vec(ln_g), vec(ln_b))


def _shortconv_body(bg_ref, cg_ref, u_ref, w_ref, o_ref, z_ref):
    n = u_ref.shape[0]
    halo = jnp.zeros((CONV_PAD, SEG), F32)
    z_ref[pl.ds(0, CONV_PAD), :] = halo
    z_ref[pl.ds(CONV_PAD + n, CONV_PAD), :] = halo
    z_ref[pl.ds(CONV_PAD, n), :] = cg_ref[...].astype(F32) * u_ref[...].astype(F32)

    def chunk(i, carry):
        t0 = pl.multiple_of(i * CONV_CHUNK, CONV_CHUNK)
        y = _conv_accumulate(z_ref, w_ref, t0, C_CONV)
        o_ref[pl.ds(t0, CONV_CHUNK), :] = (bg_ref[pl.ds(t0, CONV_CHUNK), :].astype(F32) * y).astype(BF16)
        return carry

    lax.fori_loop(0, n // CONV_CHUNK, chunk, 0)


def _shortconv(p, conv_w, *, bsz, seq):
    return pl.pallas_call(
        _shortconv_body, out_shape=jax.ShapeDtypeStruct((bsz * seq, SEG), BF16), grid=(bsz,),
        in_specs=[pl.BlockSpec((seq, SEG), lambda b: (b, 0)),
                  pl.BlockSpec((seq, SEG), lambda b: (b, 1)),
                  pl.BlockSpec((seq, SEG), lambda b: (b, 2)),
                  pl.BlockSpec((C_CONV, SEG), lambda b: (0, 0))],
        out_specs=pl.BlockSpec((seq, SEG), lambda b: (b, 0)),
        scratch_shapes=[pltpu.VMEM((seq + 2 * CONV_PAD, SEG), F32)],
        compiler_params=_cparams("arbitrary"), name="short_conv",
    )(p, p, p, conv_w)


def _diffattn_body(q_ref, k_ref, v_ref, kc_ref, vc_ref, lam_ref, g_ref, o_ref, *, lam_init):
    tq = q_ref.shape[0]
    lp = lam_ref[...]
    lam = (jnp.exp(jnp.sum(lp[0:1, :] * lp[1:2, :], axis=1, keepdims=True))
           - jnp.exp(jnp.sum(lp[2:3, :] * lp[3:4, :], axis=1, keepdims=True)) + lam_init)
    for h in range(D_HEADS):
        cs = slice(h * V7X_LANES, (h + 1) * V7X_LANES)
        q2 = _stack_halves(q_ref[:, cs])
        o2, l = _attend([_dot_t(q2, kc_ref[:, cs]), _dot_t(q2, k_ref[:, cs])], [vc_ref[:, cs], v_ref[:, cs]])
        o2 = o2 / l
        o = o2[:tq] - lam * o2[tq:]
        o = o * lax.rsqrt(jnp.mean(o * o, axis=-1, keepdims=True) + NORM_EPS) * g_ref[...]
        o_ref[:, cs] = (o * (1.0 - lam_init)).astype(BF16)


def _diffattn(px, pc, lam_p, subln, lam_init, *, bsz, n_tok, ctx_len):
    nq = n_tok // ATT_TQ
    body = functools.partial(_diffattn_body, lam_init=lam_init)
    return pl.pallas_call(
        body, out_shape=jax.ShapeDtypeStruct((bsz * n_tok, SEG), BF16), grid=(bsz, nq),
        in_specs=[pl.BlockSpec((ATT_TQ, SEG), lambda b, j: (b * nq + j, 3)),
                  pl.BlockSpec((n_tok, SEG), lambda b, j: (b, 4)),
                  pl.BlockSpec((n_tok, SEG), lambda b, j: (b, 5)),
                  pl.BlockSpec((ctx_len, SEG), lambda b, j: (b, 0)),
                  pl.BlockSpec((ctx_len, SEG), lambda b, j: (b, 1)),
                  pl.BlockSpec((4, D_QK_DIM), lambda b, j: (0, 0)),
                  pl.BlockSpec((1, D_V_DIM), lambda b, j: (0, 0))],
        out_specs=pl.BlockSpec((ATT_TQ, SEG), lambda b, j: (b * nq + j, 0)),
        compiler_params=_cparams("arbitrary", "arbitrary"), name="diff_attn",
    )(px, px, px, pc, pc, lam_p, subln.reshape(1, D_V_DIM))


def _oproj_body(a_ref, b_ref, x_ref, mod_ref, g_ref, wo_ref, wrh_ref, wrl_ref, br_ref, cin_ref,
                x1_ref, h_ref, bucket_ref, rank_ref, cout_ref, carry_ref):
    i = pl.program_id(0)
    tm = x_ref.shape[0]

    @pl.when(i == 0)
    def _():
        carry_ref[...] = cin_ref[...]

    y = _dot(a_ref[...], wo_ref[0:SEG, :]) + _dot(b_ref[...], wo_ref[SEG:2 * SEG, :])
    x1 = x_ref[...] + mod_ref[0, 2:3, :] * y
    x1_ref[...] = x1
    h = _rmsnorm_mod(x1, g_ref[...], mod_ref[0, 4:5, :], mod_ref[0, 3:4, :])
    _store_token_tiles(h_ref, h)

    h_hi, h_lo = _split_bf16(h)
    logits = _dot3(h_hi, h_lo, wrh_ref[...], wrl_ref[...]) + br_ref[...]
    lane = lax.broadcasted_iota(jnp.int32, (tm, V7X_LANES), 1)
    none = jnp.int32(V7X_LANES)
    gl = jnp.where(lane < N_GROUPS, logits, -jnp.inf)
    gm = jnp.max(gl, axis=1, keepdims=True)
    g_sel = jnp.min(jnp.where(gl == gm, lane, none), axis=1, keepdims=True)
    e_lane0 = ROUTER_EXPERT_LANE0 + EXPERTS_PER_GROUP * g_sel
    in_group = (lane >= e_lane0) & (lane < e_lane0 + EXPERTS_PER_GROUP)
    el = jnp.where(in_group, logits, -jnp.inf)
    m1 = jnp.max(el, axis=1, keepdims=True)
    i1 = jnp.min(jnp.where(el == m1, lane, none), axis=1, keepdims=True)
    el2 = jnp.where(lane == i1, -jnp.inf, el)
    m2 = jnp.max(el2, axis=1, keepdims=True)
    i2 = jnp.min(jnp.where(el2 == m2, lane, none), axis=1, keepdims=True)
    lo = jnp.minimum(i1, i2) - e_lane0
    hi = jnp.maximum(i1, i2) - e_lane0
    bucket = g_sel * PAIRS_PER_GROUP + ((lo * (2 * EXPERTS_PER_GROUP - 1 - lo)) >> 1) + (hi - lo - 1)
    onehot = lane == bucket
    r_i = lax.broadcasted_iota(jnp.int32, (tm, tm), 0)
    c_i = lax.broadcasted_iota(jnp.int32, (tm, tm), 1)
    before = jnp.where(r_i > c_i, 1.0, 0.0).astype(BF16)
    oh = jnp.where(onehot, 1.0, 0.0)
    cum = _dot(before, oh.astype(BF16)) + carry_ref[...]
    rank = jnp.sum(jnp.where(onehot, cum, 0.0), axis=1, keepdims=True)
    bucket_ref[...] = bucket
    rank_ref[...] = rank.astype(jnp.int32)
    carry_ref[...] = carry_ref[...] + jnp.sum(oh, axis=0, keepdims=True)

    @pl.when(i == pl.num_programs(0) - 1)
    def _():
        cout_ref[...] = carry_ref[...]


def _oproj(a, b, x, mod, g, wo, wr_hi, wr_lo, br, counts_in, *, seq, mod_row, name="oproj"):
    t, d = x.shape
    tm = min(PROJ_TM, t if seq is None else seq)
    if seq is None:
        mod_map = lambda i: (mod_row, 0, 0)
    else:
        tiles_per_seq = seq // tm
        mod_map = lambda i: (i // tiles_per_seq, 0, 0)
    row = lambda i: (i, 0)
    const = lambda i: (0, 0)
    return pl.pallas_call(
        _oproj_body,
        out_shape=[jax.ShapeDtypeStruct((t, d), F32), jax.ShapeDtypeStruct((t * V7X_SUBLANES, V7X_LANES), F32),
                   jax.ShapeDtypeStruct((t, 1), jnp.int32), jax.ShapeDtypeStruct((t, 1), jnp.int32),
                   jax.ShapeDtypeStruct((1, V7X_LANES), F32)],
        grid=(t // tm,),
        in_specs=[pl.BlockSpec((tm, SEG), row), pl.BlockSpec((tm, SEG), row), pl.BlockSpec((tm, d), row),
                  pl.BlockSpec((1, N_MOD, d), mod_map), pl.BlockSpec((1, d), const),
                  pl.BlockSpec((2 * SEG, d), const), pl.BlockSpec((d, V7X_LANES), const),
                  pl.BlockSpec((d, V7X_LANES), const), pl.BlockSpec((1, V7X_LANES), const),
                  pl.BlockSpec((1, V7X_LANES), const)],
        out_specs=[pl.BlockSpec((tm, d), row), pl.BlockSpec((tm * V7X_SUBLANES, V7X_LANES), row),
                   pl.BlockSpec((tm, 1), row),
                   pl.BlockSpec((tm, 1), row), pl.BlockSpec((1, V7X_LANES), const)],
        scratch_shapes=[pltpu.VMEM((1, V7X_LANES), F32)],
        compiler_params=_cparams("arbitrary"), name=name,
    )(a, b, x, mod, g.reshape(1, d), wo, wr_hi, wr_lo, br, counts_in)


def _router_weights(wg, bg, we, be):
    d = wg.shape[0]
    w = jnp.zeros((d, V7X_LANES), F32)
    w = w.at[:, ROUTER_GROUP_LANE0:ROUTER_GROUP_LANE0 + N_GROUPS].set(wg)
    w = w.at[:, ROUTER_EXPERT_LANE0:ROUTER_EXPERT_LANE0 + N_EXPERTS].set(we)
    b = jnp.zeros((1, V7X_LANES), F32)
    b = b.at[0, ROUTER_GROUP_LANE0:ROUTER_GROUP_LANE0 + N_GROUPS].set(bg)
    b = b.at[0, ROUTER_EXPERT_LANE0:ROUTER_EXPERT_LANE0 + N_EXPERTS].set(be)
    w_hi = w.astype(BF16)
    w_lo = (w - w_hi.astype(F32)).astype(BF16)
    return w_hi, w_lo, b


def _token_tile(ref, t):
    return ref.at[pl.ds(pl.multiple_of(t * V7X_SUBLANES, V7X_SUBLANES), V7X_SUBLANES)]


def _scatter_body(off_ref, bucket_ref, rank_ref, rows_ref, sorted_hbm, out_hbm, sem, *, chunk):
    del sorted_hbm

    def start(t, carry):
        p = off_ref[bucket_ref[t]] + rank_ref[t]
        pltpu.make_async_copy(_token_tile(rows_ref, t), _token_tile(out_hbm, p), sem).start()
        return carry

    lax.fori_loop(0, chunk, start, 0, unroll=8)
    pltpu.make_async_copy(rows_ref, out_hbm.at[pl.ds(0, chunk * V7X_SUBLANES)], sem).wait()


def _gather_body(off_ref, bucket_ref, rank_ref, sorted_hbm, out_ref, sem, *, chunk):
    def start(t, carry):
        p = off_ref[bucket_ref[t]] + rank_ref[t]
        pltpu.make_async_copy(_token_tile(sorted_hbm, p), _token_tile(out_ref, t), sem).start()
        return carry

    lax.fori_loop(0, chunk, start, 0, unroll=8)
    pltpu.make_async_copy(sorted_hbm.at[pl.ds(0, chunk * V7X_SUBLANES)], out_ref, sem).wait()


def _permute(offsets, bucket, rank, rows, sorted_rows, *, n_rows, scatter, name):
    chunk = min(PERM_CHUNK, n_rows)
    smem_blk = pl.BlockSpec((chunk,), lambda i, *_: (i,), memory_space=pltpu.SMEM)
    any_spec = pl.BlockSpec(memory_space=pl.ANY)
    vmem_blk = pl.BlockSpec((chunk * V7X_SUBLANES, V7X_LANES), lambda i, *_: (i, 0))
    if scatter:
        body = functools.partial(_scatter_body, chunk=chunk)
        args = (offsets, bucket, rank, rows, sorted_rows)
        in_specs = [smem_blk, smem_blk, vmem_blk, any_spec]
        out_shape = jax.ShapeDtypeStruct(sorted_rows.shape, sorted_rows.dtype)
        out_specs = any_spec
        aliases = {4: 0}
    else:
        body = functools.partial(_gather_body, chunk=chunk)
        args = (offsets, bucket, rank, sorted_rows)
        in_specs = [smem_blk, smem_blk, any_spec]
        out_shape = jax.ShapeDtypeStruct((n_rows * V7X_SUBLANES, V7X_LANES), sorted_rows.dtype)
        out_specs = vmem_blk
        aliases = {}
    grid_spec = pltpu.PrefetchScalarGridSpec(
        num_scalar_prefetch=1, grid=(n_rows // chunk,), in_specs=in_specs, out_specs=out_specs,
        scratch_shapes=[pltpu.SemaphoreType.DMA])
    return pl.pallas_call(
        body, out_shape=out_shape, grid_spec=grid_spec, input_output_aliases=aliases,
        compiler_params=pltpu.CompilerParams(dimension_semantics=("arbitrary",), has_side_effects=True,
                                             vmem_limit_bytes=V7X_VMEM_LIMIT_BYTES),
        name=name,
    )(*args)


def _moe_body(te1_ref, te2_ref, tval_ref, trow_ref, x_ref, w1a_ref, w1b_ref, w3a_ref, w3b_ref, w2a_ref, w2b_ref,
              wr_ref, br_ref, y_ref):
    i = pl.program_id(0)

    @pl.when(tval_ref[i] == 0)
    def _():
        y_ref[...] = jnp.zeros_like(y_ref)

    @pl.when(tval_ref[i] > 0)
    def _():
        tm = x_ref.shape[0] // V7X_SUBLANES
        x = _load_token_tiles(x_ref).astype(BF16)
        e1 = te1_ref[i]
        e2 = te2_ref[i]
        grp = e1 >> 3
        logits = _dot(x, wr_ref[...]) + br_ref[...]
        lane = lax.broadcasted_iota(jnp.int32, (tm, V7X_LANES), 1)
        pick = lambda idx: jnp.sum(jnp.where(lane == idx, logits, 0.0), axis=1, keepdims=True)
        gl = jnp.where(lane < N_GROUPS, logits, -jnp.inf)
        gm = jnp.max(gl, axis=1, keepdims=True)
        gz = jnp.sum(jnp.exp(gl - gm), axis=1, keepdims=True)
        g_w = jnp.exp(pick(ROUTER_GROUP_LANE0 + grp) - gm) / gz
        l1 = pick(ROUTER_EXPERT_LANE0 + e1)
        l2 = pick(ROUTER_EXPERT_LANE0 + e2)
        m = jnp.maximum(l1, l2)
        p1 = jnp.exp(l1 - m)
        p2 = jnp.exp(l2 - m)
        c1 = g_w * p1 / (p1 + p2)
        c2 = g_w * p2 / (p1 + p2)

        def expert(w1_ref, w3_ref, w2_ref):
            h = _dot(x, w1_ref[0])
            hid = (h * _sigmoid(h)) * _dot(x, w3_ref[0])
            return _dot(hid.astype(BF16), w2_ref[0])

        _store_token_tiles(y_ref, c1 * expert(w1a_ref, w3a_ref, w2a_ref) + c2 * expert(w1b_ref, w3b_ref, w2b_ref))


def _moe_sorted(xs, te1, te2, tval, trow, w1, w3, w2, wr_hi, br):
    d = w1.shape[1]
    n_tiles = xs.shape[0] // (MOE_TM * V7X_SUBLANES)
    tile_blk = (MOE_TM * V7X_SUBLANES, V7X_LANES)
    xmap = lambda i, te1, te2, tval, trow: (trow[i], 0)
    wa = lambda i, te1, te2, tval, trow: (te1[i], 0, 0)
    wb = lambda i, te1, te2, tval, trow: (te2[i], 0, 0)
    const = lambda i, *_: (0, 0)
    grid_spec = pltpu.PrefetchScalarGridSpec(
        num_scalar_prefetch=4, grid=(n_tiles,),
        in_specs=[pl.BlockSpec(tile_blk, xmap),
                  pl.BlockSpec((1, d, D_EXPERT), wa), pl.BlockSpec((1, d, D_EXPERT), wb),
                  pl.BlockSpec((1, d, D_EXPERT), wa), pl.BlockSpec((1, d, D_EXPERT), wb),
                  pl.BlockSpec((1, D_EXPERT, d), wa), pl.BlockSpec((1, D_EXPERT, d), wb),
                  pl.BlockSpec((d, V7X_LANES), const), pl.BlockSpec((1, V7X_LANES), const)],
        out_specs=pl.BlockSpec(tile_blk, lambda i, *_: (i, 0)))
    return pl.pallas_call(
        _moe_body, out_shape=jax.ShapeDtypeStruct(xs.shape, F32), grid_spec=grid_spec,
        compiler_params=_cparams("arbitrary"), name="moe_experts",
    )(te1, te2, tval, trow, xs, w1, w1, w3, w3, w2, w2, wr_hi, br)


def _moe(h_parts, route_parts, counts, w1, w3, w2, wr_hi, br):
    n_rows = [h.shape[0] // V7X_SUBLANES for h in h_parts]
    n_tiles = sum(n_rows) // MOE_TM + N_BUCKETS
    cnt = counts[0, :N_BUCKETS].astype(jnp.int32)
    tiles_b = (cnt + MOE_TM - 1) // MOE_TM
    tile_end = jnp.cumsum(tiles_b)
    n_valid = tile_end[-1]
    offsets = jnp.zeros((V7X_LANES,), jnp.int32).at[:N_BUCKETS].set((tile_end - tiles_b) * MOE_TM)
    tile_id = jnp.arange(n_tiles, dtype=jnp.int32)
    tval = (tile_id < n_valid).astype(jnp.int32)
    trow = jnp.minimum(tile_id, n_valid - 1)
    tbucket = jnp.sum((trow[:, None] >= tile_end[None, :]).astype(jnp.int32), axis=1)
    tbucket = jnp.minimum(tbucket, N_BUCKETS - 1)
    te1 = jnp.asarray(_BUCKET_E1)[tbucket]
    te2 = jnp.asarray(_BUCKET_E2)[tbucket]

    xs = jnp.zeros((n_tiles * MOE_TM * V7X_SUBLANES, V7X_LANES), F32)
    for h, n, (bucket, rank) in zip(h_parts, n_rows, route_parts):
        xs = _permute(offsets, bucket, rank, h, xs, n_rows=n, scatter=True, name="moe_scatter_rows")
    ys = _moe_sorted(xs, te1, te2, tval, trow, w1, w3, w2, wr_hi, br)
    return [_permute(offsets, bucket, rank, None, ys, n_rows=n, scatter=False, name="moe_gather_rows")
            for n, (bucket, rank) in zip(n_rows, route_parts)]


def _final_body(x_ref, y_ref, mod_ref, g_ref, o_ref):
    x = x_ref[...] + mod_ref[0, 5:6, :] * _load_token_tiles(y_ref)
    o_ref[...] = x * lax.rsqrt(jnp.mean(x * x, axis=-1, keepdims=True) + NORM_EPS) * g_ref[...]


def _final(x, y, mod, g, *, seq):
    t, d = x.shape
    tm = min(PROJ_TM, seq)
    tiles_per_seq = seq // tm
    row = lambda i: (i, 0)
    return pl.pallas_call(
        _final_body, out_shape=jax.ShapeDtypeStruct((t, d), F32), grid=(t // tm,),
        in_specs=[pl.BlockSpec((tm, d), row), pl.BlockSpec((tm * V7X_SUBLANES, V7X_LANES), row),
                  pl.BlockSpec((1, N_MOD, d), lambda i: (i // tiles_per_seq, 0, 0)),
                  pl.BlockSpec((1, d), lambda i: (0, 0))],
        out_specs=pl.BlockSpec((tm, d), row),
        compiler_params=_cparams("arbitrary"), name="final_norm",
    )(x, y, mod, g.reshape(1, d))


def kernel(x, c, ctx, c_ctx, w_mod, b_mod, norm_mix, norm_ffn, norm_final, even_w_in, even_w_out, even_rpb,
           even_dw, even_dw_b, even_ln_g, even_ln_b, odd_w_in, odd_w_out, odd_conv, odd_lambda, odd_subln,
           moe_wg, moe_bg, moe_we, moe_be, moe_w1, moe_w3, moe_w2):
    bsz, n_tok, d = x.shape
    ctx_len = ctx.shape[1]
    depth = w_mod.shape[0]
    assert depth == 2 and d == D_MODEL
    xs = x.reshape(bsz * n_tok, d)
    cs = ctx.reshape(bsz * ctx_len, d)

    mod_rows = -(-(bsz + 1) // 8) * 8
    cc = jnp.zeros((mod_rows, d), F32).at[:bsz].set(c).at[bsz].set(c_ctx)
    mod = _modulation(cc, w_mod, b_mod).reshape(depth, mod_rows, N_MOD, d)
    bf = lambda a: a.astype(BF16)
    zero_counts = jnp.zeros((1, V7X_LANES), F32)

    l = 0
    w_in = bf(even_w_in[0])
    w_out = bf(even_w_out[0])
    px = _proj(xs, mod[l], norm_mix[l], w_in, seq=n_tok, mod_row=None, q_seg=0, name="proj_even_x")
    pc = _proj(cs, mod[l], norm_mix[l], w_in, seq=None, mod_row=bsz, q_seg=0, name="proj_even_c")
    a_x = _natten(px, pc, even_rpb[0], bsz=bsz, n_tok=n_tok, ctx_len=ctx_len)
    a_c = _ctxattn(pc, bsz=bsz, ctx_len=ctx_len)
    b_x = _conformer(px, even_dw[0], even_dw_b[0], even_ln_g[0], even_ln_b[0], bsz=bsz, seq=n_tok)
    b_c = _conformer(pc, even_dw[0], even_dw_b[0], even_ln_g[0], even_ln_b[0], bsz=bsz, seq=ctx_len)
    wr_hi, wr_lo, br = _router_weights(moe_wg[l], moe_bg[l], moe_we[l], moe_be[l])
    x1, hx, bkt_x, rnk_x, counts = _oproj(a_x, b_x, xs, mod[l], norm_ffn[l], w_out, wr_hi, wr_lo, br, zero_counts,
                                          seq=n_tok, mod_row=None, name="oproj_even_x")
    c1, hc, bkt_c, rnk_c, counts = _oproj(a_c, b_c, cs, mod[l], norm_ffn[l], w_out, wr_hi, wr_lo, br, counts,
                                          seq=None, mod_row=bsz, name="oproj_even_c")
    y_x, y_c = _moe([hx, hc], [(bkt_x.reshape(-1), rnk_x.reshape(-1)), (bkt_c.reshape(-1), rnk_c.reshape(-1))],
                    counts, bf(moe_w1[l]), bf(moe_w3[l]), bf(moe_w2[l]), wr_hi, br)

    l = 1
    lam_init = 0.8 - 0.6 * math.exp(-0.3 * l)
    w_in = bf(odd_w_in[0])
    w_out = bf(odd_w_out[0])
    rope = _rope_tables(n_tok)
    px, x2 = _proj(x1, mod[l], norm_mix[l], w_in, seq=n_tok, mod_row=None, y=y_x, res_mod=mod[l - 1], res_idx=5,
                   rope=rope, rope_segs=(3, 4), q_seg=3, write_x=True, name="proj_odd_x")
    pc = _proj(c1, mod[l], norm_mix[l], w_in[:, 4 * SEG:], seq=None, mod_row=bsz, y=y_c, res_mod=mod[l - 1],
               res_idx=5, name="proj_odd_c")
    s_x = _shortconv(px, odd_conv[0], bsz=bsz, seq=n_tok)
    d_x = _diffattn(px, pc, odd_lambda[0], odd_subln[0], lam_init, bsz=bsz, n_tok=n_tok, ctx_len=ctx_len)
    wr_hi, wr_lo, br = _router_weights(moe_wg[l], moe_bg[l], moe_we[l], moe_be[l])
    x3, hx, bkt_x, rnk_x, counts = _oproj(s_x, d_x, x2, mod[l], norm_ffn[l], w_out, wr_hi, wr_lo, br, zero_counts,
                                          seq=n_tok, mod_row=None, name="oproj_odd_x")
    (y_x,) = _moe([hx], [(bkt_x.reshape(-1), rnk_x.reshape(-1))], counts,
                  bf(moe_w1[l]), bf(moe_w3[l]), bf(moe_w2[l]), wr_hi, br)
    out = _final(x3, y_x, mod[l], norm_final, seq=n_tok)
    return out.reshape(bsz, n_tok, d)
```

```python
import functools
import math

import numpy as np
import jax
import jax.numpy as jnp
from jax import lax
from jax.experimental import pallas as pl
from jax.experimental.pallas import tpu as pltpu

F32 = jnp.float32
BF16 = jnp.bfloat16

D_MODEL = 1024
GRID_W = 64
N_MOD = 6
NORM_EPS = 1e-6
NEG_INF = -1e30
ROPE_BASE = 10000.0
SEG = 512
A_HEAD_DIM = 64
A_HEADS = 8
WIN_ROWS_MAX = 8
WIN_COLS = 16
B_CONV = 31
C_CONV = 3
D_QK_DIM = 64
D_V_DIM = 128
D_HEADS = 4
N_GROUPS = 4
EXPERTS_PER_GROUP = 8
N_EXPERTS = 32
D_EXPERT = 256
LOG2E = math.log2(math.e)
QUERY_SCALE = A_HEAD_DIM ** -0.5 * LOG2E
PAIRS_PER_GROUP = EXPERTS_PER_GROUP * (EXPERTS_PER_GROUP - 1) // 2
N_BUCKETS = N_GROUPS * PAIRS_PER_GROUP

V7X_LANES = 128
V7X_SUBLANES = 8
V7X_VMEM_LIMIT_BYTES = 56 * 1024 * 1024

PROJ_TM = 512
ATT_TQ = 256
NAT_KROWS = 12
CONV_CHUNK = 64
CONV_PAD = 16
MOE_TM = 256
PERM_CHUNK = 1024
ROUTER_GROUP_LANE0 = 0
ROUTER_EXPERT_LANE0 = 8

_PAIR_LO = np.array([i for i in range(8) for j in range(i + 1, 8)], np.int32)
_PAIR_HI = np.array([j for i in range(8) for j in range(i + 1, 8)], np.int32)
_BUCKET_E1 = np.concatenate([g * 8 + _PAIR_LO for g in range(N_GROUPS)]).astype(np.int32)
_BUCKET_E2 = np.concatenate([g * 8 + _PAIR_HI for g in range(N_GROUPS)]).astype(np.int32)


def _cparams(*sem):
    return pltpu.CompilerParams(dimension_semantics=tuple(sem), vmem_limit_bytes=V7X_VMEM_LIMIT_BYTES)


def _dot(a, b):
    return jnp.dot(a, b, preferred_element_type=F32)


def _dot_t(a, b):
    return lax.dot_general(a, b, (((1,), (1,)), ((), ())), preferred_element_type=F32)


def _split_bf16(a):
    hi = a.astype(BF16)
    lo = (a - hi.astype(F32)).astype(BF16)
    return hi, lo


def _dot3(a_hi, a_lo, b_hi, b_lo):
    return _dot(a_hi, b_hi) + _dot(a_lo, b_hi) + _dot(a_hi, b_lo)


def _sigmoid(x):
    return 1.0 / (1.0 + jnp.exp(-x))


def _load_token_tiles(ref):
    tm = ref.shape[0] // V7X_SUBLANES
    return jnp.concatenate([ref[pl.ds(c, tm, stride=V7X_SUBLANES), :] for c in range(V7X_SUBLANES)], axis=1)


def _store_token_tiles(ref, v):
    tm = v.shape[0]
    for c in range(V7X_SUBLANES):
        ref[pl.ds(c, tm, stride=V7X_SUBLANES), :] = v[:, c * V7X_LANES:(c + 1) * V7X_LANES]


def _rmsnorm_mod(x, g, scale, shift):
    y = x * lax.rsqrt(jnp.mean(x * x, axis=-1, keepdims=True) + NORM_EPS)
    return (y * g) * (1.0 + scale) + shift


def _mod_body(c_ref, w_ref, b_ref, o_ref):
    c = c_ref[...]
    s = c * _sigmoid(c)
    s_hi, s_lo = _split_bf16(s)
    w_hi, w_lo = _split_bf16(w_ref[0])
    o_ref[0] = _dot3(s_hi, s_lo, w_hi, w_lo) + b_ref[0]


def _modulation(cc, w_mod, b_mod):
    depth, d, n = w_mod.shape
    rows = cc.shape[0]
    tn = 1536
    return pl.pallas_call(
        _mod_body,
        out_shape=jax.ShapeDtypeStruct((depth, rows, n), F32),
        grid=(depth, n // tn),
        in_specs=[pl.BlockSpec((rows, d), lambda l, j: (0, 0)),
                  pl.BlockSpec((1, d, tn), lambda l, j: (l, 0, j)),
                  pl.BlockSpec((1, 1, tn), lambda l, j: (l, 0, j))],
        out_specs=pl.BlockSpec((1, rows, tn), lambda l, j: (l, 0, j)),
        compiler_params=_cparams("arbitrary", "arbitrary"),
        name="modulation",
    )(cc, w_mod, b_mod.reshape(depth, 1, n))


def _rope(v, cos, sin, lane):
    up = pltpu.roll(v, V7X_LANES - 16, axis=1)
    dn = pltpu.roll(v, 16, axis=1)
    sw = jnp.where((lane & 31) < 16, up, dn)
    return v * cos + sw * sin


def _proj_body(*refs, n_seg, res_idx, shift_idx, rope_segs, q_seg, write_x):
    refs = list(refs)
    x_ref = refs.pop(0)
    y_ref = refs.pop(0) if res_idx is not None else None
    rmod_ref = refs.pop(0) if res_idx is not None else None
    mod_ref = refs.pop(0)
    g_ref = refs.pop(0)
    w_ref = refs.pop(0)
    cos_ref = sin_ref = None
    if rope_segs:
        cos_ref = refs.pop(0)
        sin_ref = refs.pop(0)
    out_ref = refs.pop(0)
    xo_ref = refs.pop(0) if write_x else None

    x = x_ref[...]
    if y_ref is not None:
        x = x + rmod_ref[0, res_idx:res_idx + 1, :] * _load_token_tiles(y_ref)
        if write_x:
            xo_ref[...] = x
    h = _rmsnorm_mod(x, g_ref[...], mod_ref[0, shift_idx + 1:shift_idx + 2, :],
                     mod_ref[0, shift_idx:shift_idx + 1, :]).astype(BF16)
    for s in range(n_seg):
        o = _dot(h, w_ref[:, s * SEG:(s + 1) * SEG])
        if s == q_seg:
            o = o * QUERY_SCALE
        if s in rope_segs:
            lane = lax.broadcasted_iota(jnp.int32, (o.shape[0], V7X_LANES), 1)
            cos = cos_ref[...]
            sin = sin_ref[...]
            o = jnp.concatenate(
                [_rope(o[:, c * V7X_LANES:(c + 1) * V7X_LANES], cos, sin, lane) for c in range(SEG // V7X_LANES)],
                axis=1)
        out_ref[:, s * SEG:(s + 1) * SEG] = o.astype(BF16)


def _proj(x, mod, g, w, *, seq, mod_row, y=None, res_mod=None, res_idx=None, shift_idx=0, rope=None, rope_segs=(), q_seg=None,
          write_x=False, name="proj"):
    t, d = x.shape
    n = w.shape[1]
    tm = min(PROJ_TM, t if seq is None else seq)
    tiles_per_seq = None if seq is None else seq // tm
    if seq is None:
        mod_map = lambda i: (mod_row, 0, 0)
    else:
        mod_map = lambda i: (i // tiles_per_seq, 0, 0)
    args = [x]
    specs = [pl.BlockSpec((tm, d), lambda i: (i, 0))]
    if y is not None:
        args += [y, res_mod]
        specs += [pl.BlockSpec((tm * V7X_SUBLANES, V7X_LANES), lambda i: (i, 0)),
                  pl.BlockSpec((1, N_MOD, d), mod_map)]
    args += [mod, g.reshape(1, d), w]
    specs += [pl.BlockSpec((1, N_MOD, d), mod_map), pl.BlockSpec((1, d), lambda i: (0, 0)),
              pl.BlockSpec((d, n), lambda i: (0, 0))]
    if rope_segs:
        args += [rope[0], rope[1]]
        specs += [pl.BlockSpec((tm, V7X_LANES), lambda i: (i % tiles_per_seq, 0))] * 2
    out_shape = [jax.ShapeDtypeStruct((t, n), BF16)]
    out_specs = [pl.BlockSpec((tm, n), lambda i: (i, 0))]
    if write_x:
        out_shape.append(jax.ShapeDtypeStruct((t, d), F32))
        out_specs.append(pl.BlockSpec((tm, d), lambda i: (i, 0)))
    body = functools.partial(_proj_body, n_seg=n // SEG, res_idx=res_idx if y is not None else None,
                             shift_idx=shift_idx, rope_segs=tuple(rope_segs), q_seg=q_seg, write_x=write_x)
    outs = pl.pallas_call(
        body, out_shape=out_shape, grid=(t // tm,), in_specs=specs, out_specs=out_specs,
        compiler_params=_cparams("arbitrary"), name=name,
    )(*args)
    return outs if write_x else outs[0]


def _rope_tables(n_tok):
    quarter = D_QK_DIM // 4
    t = np.arange(n_tok)
    row = (t // GRID_W).astype(np.float32)
    col = (t % GRID_W).astype(np.float32)
    inv = jnp.power(ROPE_BASE, -jnp.arange(quarter, dtype=F32) / quarter)
    ar = jnp.asarray(row)[:, None] * inv
    ac = jnp.asarray(col)[:, None] * inv
    cos64 = jnp.concatenate([jnp.cos(ar), jnp.cos(ar), jnp.cos(ac), jnp.cos(ac)], axis=-1)
    sin64 = jnp.concatenate([-jnp.sin(ar), jnp.sin(ar), -jnp.sin(ac), jnp.sin(ac)], axis=-1)
    return jnp.tile(cos64, (1, V7X_LANES // D_QK_DIM)), jnp.tile(sin64, (1, V7X_LANES // D_QK_DIM))


def _stack_halves(q):
    lane = lax.broadcasted_iota(jnp.int32, q.shape, 1)
    zero = jnp.zeros_like(q)
    return jnp.concatenate([jnp.where(lane < 64, q, zero), jnp.where(lane >= 64, q, zero)], axis=0)


def _attend(scores, values):
    m = functools.reduce(jnp.maximum, [jnp.max(s, axis=1, keepdims=True) for s in scores])
    acc = None
    for s, v in zip(scores, values):
        p = jnp.exp2(s - m).astype(BF16)
        lane = lax.broadcasted_iota(jnp.int32, v.shape, 1)
        ones_col = jnp.where(lane == 0, 1.0, 0.0).astype(BF16)
        o = _dot(p, jnp.concatenate([v, ones_col], axis=1))
        acc = o if acc is None else acc + o
    return acc[:, :V7X_LANES], acc[:, V7X_LANES:V7X_LANES + 1]


def _natten_body(ks_ref, pid_ref, q_ref, k_ref, v_ref, kc_ref, vc_ref, bias_ref, o_ref):
    j = pl.program_id(1)
    k0 = pl.multiple_of(ks_ref[j] * GRID_W, GRID_W)
    p = pid_ref[j]
    nk = NAT_KROWS * GRID_W
    tq = q_ref.shape[0]
    lane = lax.broadcasted_iota(jnp.int32, (tq, V7X_LANES), 1)
    for hp in range(A_HEADS // 2):
        cs = slice(hp * V7X_LANES, (hp + 1) * V7X_LANES)
        q2 = _stack_halves(q_ref[:, cs])
        s_c = _dot_t(q2, kc_ref[:, cs])
        s_l = _dot_t(q2, k_ref[pl.ds(k0, nk), cs])
        bias = jnp.concatenate([bias_ref[p, 2 * hp], bias_ref[p, 2 * hp + 1]], axis=0).astype(F32)
        o2, l = _attend([s_c, s_l + bias], [vc_ref[:, cs], v_ref[pl.ds(k0, nk), cs]])
        o2 = o2 / l
        o_ref[:, cs] = jnp.where(lane < 64, o2[:tq], o2[tq:]).astype(BF16)


def _natten_tables(rows):
    q_rows = ATT_TQ // GRID_W
    nqb = rows // q_rows
    win_r = min(WIN_ROWS_MAX, rows)
    col = np.arange(GRID_W)
    c_start = np.clip(col - WIN_COLS // 2, 0, GRID_W - WIN_COLS)
    col_ok = (col[None, :] >= c_start[:, None]) & (col[None, :] < c_start[:, None] + WIN_COLS)
    dc = np.clip(col[None, :] - col[:, None], -(WIN_COLS - 1), WIN_COLS - 1) + WIN_COLS - 1
    ks_list, pats, pid = [], [], []
    for j in range(nqb):
        r0 = j * q_rows
        ks = int(np.clip(r0 - win_r // 2, 0, rows - NAT_KROWS))
        qr = r0 + np.arange(q_rows)
        rs = np.clip(qr - win_r // 2, 0, rows - win_r)
        kr = ks + np.arange(NAT_KROWS)
        row_ok = (kr[None, :] >= rs[:, None]) & (kr[None, :] < rs[:, None] + win_r)
        dr = np.clip(kr[None, :] - qr[:, None] + WIN_ROWS_MAX - 1, 0, 2 * WIN_ROWS_MAX - 2)
        key = (row_ok.tobytes(), (dr * row_ok).tobytes())
        for n, (k_, *_rest) in enumerate(pats):
            if k_ == key:
                pid.append(n)
                break
        else:
            pid.append(len(pats))
            pats.append((key, row_ok, dr))
        ks_list.append(ks)
    row_ok = np.stack([p_[1] for p_ in pats])
    dr = np.stack([p_[2] for p_ in pats])
    return np.array(ks_list, np.int32), np.array(pid, np.int32), row_ok, dr, col_ok, dc


def _natten_bias(rpb, row_ok, dr, col_ok, dc):
    n_pat, q_rows, k_rows = dr.shape
    n_dr, n_dc = rpb.shape[1], rpb.shape[2]
    oh_r = np.zeros((n_pat * q_rows * k_rows, n_dr), np.float32)
    oh_r[np.arange(oh_r.shape[0]), dr.reshape(-1)] = 1.0
    oh_c = np.zeros((n_dc, GRID_W * GRID_W), np.float32)
    oh_c[dc.reshape(-1), np.arange(GRID_W * GRID_W)] = 1.0
    hi = lax.Precision.HIGHEST
    t1 = jnp.einsum("mr,hrc->hmc", jnp.asarray(oh_r), rpb, precision=hi)
    t2 = jnp.einsum("hmc,cn->hmn", t1, jnp.asarray(oh_c), precision=hi)
    t2 = t2.reshape(rpb.shape[0], n_pat, q_rows, k_rows, GRID_W, GRID_W)
    valid = row_ok[None, :, :, :, None, None] & col_ok[None, None, None, None, :, :]
    bias = jnp.where(valid, t2 * LOG2E, NEG_INF).transpose(1, 0, 2, 4, 3, 5)
    return bias.reshape(n_pat, rpb.shape[0], q_rows * GRID_W, k_rows * GRID_W).astype(BF16)


def _natten(px, pc, rpb, *, bsz, n_tok, ctx_len):
    rows = n_tok // GRID_W
    assert rows >= NAT_KROWS and n_tok % ATT_TQ == 0
    ks, pid, row_ok, dr, col_ok, dc = _natten_tables(rows)
    bias = _natten_bias(rpb, row_ok, dr, col_ok, dc)
    n_pat = bias.shape[0]
    nqb = n_tok // ATT_TQ
    nk = NAT_KROWS * GRID_W
    grid_spec = pltpu.PrefetchScalarGridSpec(
        num_scalar_prefetch=2, grid=(bsz, nqb),
        in_specs=[pl.BlockSpec((ATT_TQ, SEG), lambda b, j, *_: (b * nqb + j, 0)),
                  pl.BlockSpec((n_tok, SEG), lambda b, j, *_: (b, 1)),
                  pl.BlockSpec((n_tok, SEG), lambda b, j, *_: (b, 2)),
                  pl.BlockSpec((ctx_len, SEG), lambda b, j, *_: (b, 1)),
                  pl.BlockSpec((ctx_len, SEG), lambda b, j, *_: (b, 2)),
                  pl.BlockSpec((n_pat, A_HEADS, ATT_TQ, nk), lambda b, j, *_: (0, 0, 0, 0))],
        out_specs=pl.BlockSpec((ATT_TQ, SEG), lambda b, j, *_: (b * nqb + j, 0)))
    return pl.pallas_call(
        _natten_body, out_shape=jax.ShapeDtypeStruct((bsz * n_tok, SEG), BF16), grid_spec=grid_spec,
        compiler_params=_cparams("arbitrary", "arbitrary"), name="natten",
    )(jnp.asarray(ks), jnp.asarray(pid), px, px, px, pc, pc, bias)


def _ctxattn_body(q_ref, k_ref, v_ref, o_ref):
    tq = q_ref.shape[0]
    lane = lax.broadcasted_iota(jnp.int32, (tq, V7X_LANES), 1)
    for hp in range(A_HEADS // 2):
        cs = slice(hp * V7X_LANES, (hp + 1) * V7X_LANES)
        q2 = _stack_halves(q_ref[:, cs])
        o2, l = _attend([_dot_t(q2, k_ref[:, cs])], [v_ref[:, cs]])
        o2 = o2 / l
        o_ref[:, cs] = jnp.where(lane < 64, o2[:tq], o2[tq:]).astype(BF16)


def _ctxattn(pc, *, bsz, ctx_len):
    return pl.pallas_call(
        _ctxattn_body, out_shape=jax.ShapeDtypeStruct((bsz * ctx_len, SEG), BF16), grid=(bsz,),
        in_specs=[pl.BlockSpec((ctx_len, SEG), lambda b: (b, 0)),
                  pl.BlockSpec((ctx_len, SEG), lambda b: (b, 1)),
                  pl.BlockSpec((ctx_len, SEG), lambda b: (b, 2))],
        out_specs=pl.BlockSpec((ctx_len, SEG), lambda b: (b, 0)),
        compiler_params=_cparams("arbitrary"), name="ctxattn",
    )(pc, pc, pc)


def _conv_accumulate(z_ref, w_ref, t0, taps):
    base = CONV_PAD - taps // 2
    blocks = []
    for c in range(SEG // V7X_LANES):
        cs = slice(c * V7X_LANES, (c + 1) * V7X_LANES)
        win = z_ref[pl.ds(t0, CONV_CHUNK + 2 * CONV_PAD), cs]
        acc = None
        for b in range(V7X_SUBLANES):
            ks = [k for k in range(taps) if (base + k) % V7X_SUBLANES == b]
            if not ks:
                continue
            shifted = win if b == 0 else pltpu.roll(win, win.shape[0] - b, axis=0)
            for k in ks:
                a0 = (base + k) // V7X_SUBLANES * V7X_SUBLANES
                term = shifted[a0:a0 + CONV_CHUNK, :] * w_ref[k:k + 1, cs]
                acc = term if acc is None else acc + term
        blocks.append(acc)
    return jnp.concatenate(blocks, axis=1)


def _conformer_body(u_ref, g_ref, w_ref, b_ref, lg_ref, lb_ref, o_ref, z_ref):
    n = u_ref.shape[0]
    halo = jnp.zeros((CONV_PAD, SEG), F32)
    z_ref[pl.ds(0, CONV_PAD), :] = halo
    z_ref[pl.ds(CONV_PAD + n, CONV_PAD), :] = halo
    z_ref[pl.ds(CONV_PAD, n), :] = u_ref[...].astype(F32) * _sigmoid(g_ref[...].astype(F32))

    def chunk(i, carry):
        t0 = pl.multiple_of(i * CONV_CHUNK, CONV_CHUNK)
        y = _conv_accumulate(z_ref, w_ref, t0, B_CONV) + b_ref[...]
        mu = jnp.mean(y, axis=-1, keepdims=True)
        yc = y - mu
        var = jnp.mean(yc * yc, axis=-1, keepdims=True)
        y = yc * lax.rsqrt(var + NORM_EPS) * lg_ref[...] + lb_ref[...]
        o_ref[pl.ds(t0, CONV_CHUNK), :] = (y * _sigmoid(y)).astype(BF16)
        return carry

    lax.fori_loop(0, n // CONV_CHUNK, chunk, 0)


def _conformer(p, dw, dw_b, ln_g, ln_b, *, bsz, seq):
    vec = lambda a: a.reshape(1, SEG)
    return pl.pallas_call(
        _conformer_body, out_shape=jax.ShapeDtypeStruct((bsz * seq, SEG), BF16), grid=(bsz,),
        in_specs=[pl.BlockSpec((seq, SEG), lambda b: (b, 3)),
                  pl.BlockSpec((seq, SEG), lambda b: (b, 4)),
                  pl.BlockSpec((B_CONV, SEG), lambda b: (0, 0))] + [pl.BlockSpec((1, SEG), lambda b: (0, 0))] * 3,
        out_specs=pl.BlockSpec((seq, SEG), lambda b: (b, 0)),
        scratch_shapes=[pltpu.VMEM((seq + 2 * CONV_PAD, SEG), F32)],
        compiler_params=_cparams("arbitrary"), name="conformer_conv",
    )(p, p, dw, vec(dw_b), vec(ln_g), vec(ln_b))


def _shortconv_body(bg_ref, cg_ref, u_ref, w_ref, o_ref, z_ref):
    n = u_ref.shape[0]
    halo = jnp.zeros((CONV_PAD, SEG), F32)
    z_ref[pl.ds(0, CONV_PAD), :] = halo
    z_ref[pl.ds(CONV_PAD + n, CONV_PAD), :] = halo
    z_ref[pl.ds(CONV_PAD, n), :] = cg_ref[...].astype(F32) * u_ref[...].astype(F32)

    def chunk(i, carry):
        t0 = pl.multiple_of(i * CONV_CHUNK, CONV_CHUNK)
        y = _conv_accumulate(z_ref, w_ref, t0, C_CONV)
        o_ref[pl.ds(t0, CONV_CHUNK), :] = (bg_ref[pl.ds(t0, CONV_CHUNK), :].astype(F32) * y).astype(BF16)
        return carry

    lax.fori_loop(0, n // CONV_CHUNK, chunk, 0)


def _shortconv(p, conv_w, *, bsz, seq):
    return pl.pallas_call(
        _shortconv_body, out_shape=jax.ShapeDtypeStruct((bsz * seq, SEG), BF16), grid=(bsz,),
        in_specs=[pl.BlockSpec((seq, SEG), lambda b: (b, 0)),
                  pl.BlockSpec((seq, SEG), lambda b: (b, 1)),
                  pl.BlockSpec((seq, SEG), lambda b: (b, 2)),
                  pl.BlockSpec((C_CONV, SEG), lambda b: (0, 0))],
        out_specs=pl.BlockSpec((seq, SEG), lambda b: (b, 0)),
        scratch_shapes=[pltpu.VMEM((seq + 2 * CONV_PAD, SEG), F32)],
        compiler_params=_cparams("arbitrary"), name="short_conv",
    )(p, p, p, conv_w)


def _diffattn_body(q_ref, k_ref, v_ref, kc_ref, vc_ref, lam_ref, g_ref, o_ref, *, lam_init):
    tq = q_ref.shape[0]
    lp = lam_ref[...]
    lam = (jnp.exp(jnp.sum(lp[0:1, :] * lp[1:2, :], axis=1, keepdims=True))
           - jnp.exp(jnp.sum(lp[2:3, :] * lp[3:4, :], axis=1, keepdims=True)) + lam_init)
    for h in range(D_HEADS):
        cs = slice(h * V7X_LANES, (h + 1) * V7X_LANES)
        q2 = _stack_halves(q_ref[:, cs])
        o2, l = _attend([_dot_t(q2, kc_ref[:, cs]), _dot_t(q2, k_ref[:, cs])], [vc_ref[:, cs], v_ref[:, cs]])
        o2 = o2 / l
        o = o2[:tq] - lam * o2[tq:]
        o = o * lax.rsqrt(jnp.mean(o * o, axis=-1, keepdims=True) + NORM_EPS) * g_ref[...]
        o_ref[:, cs] = (o * (1.0 - lam_init)).astype(BF16)


def _diffattn(px, pc, lam_p, subln, lam_init, *, bsz, n_tok, ctx_len):
    nq = n_tok // ATT_TQ
    body = functools.partial(_diffattn_body, lam_init=lam_init)
    return pl.pallas_call(
        body, out_shape=jax.ShapeDtypeStruct((bsz * n_tok, SEG), BF16), grid=(bsz, nq),
        in_specs=[pl.BlockSpec((ATT_TQ, SEG), lambda b, j: (b * nq + j, 3)),
                  pl.BlockSpec((n_tok, SEG), lambda b, j: (b, 4)),
                  pl.BlockSpec((n_tok, SEG), lambda b, j: (b, 5)),
                  pl.BlockSpec((ctx_len, SEG), lambda b, j: (b, 0)),
                  pl.BlockSpec((ctx_len, SEG), lambda b, j: (b, 1)),
                  pl.BlockSpec((4, D_QK_DIM), lambda b, j: (0, 0)),
                  pl.BlockSpec((1, D_V_DIM), lambda b, j: (0, 0))],
        out_specs=pl.BlockSpec((ATT_TQ, SEG), lambda b, j: (b * nq + j, 0)),
        compiler_params=_cparams("arbitrary", "arbitrary"), name="diff_attn",
    )(px, px, px, pc, pc, lam_p, subln.reshape(1, D_V_DIM))


def _oproj_body(a_ref, b_ref, x_ref, mod_ref, g_ref, wo_ref, wrh_ref, wrl_ref, br_ref, cin_ref,
                x1_ref, h_ref, bucket_ref, rank_ref, cout_ref, carry_ref):
    i = pl.program_id(0)
    tm = x_ref.shape[0]

    @pl.when(i == 0)
    def _():
        carry_ref[...] = cin_ref[...]

    y = _dot(a_ref[...], wo_ref[0:SEG, :]) + _dot(b_ref[...], wo_ref[SEG:2 * SEG, :])
    x1 = x_ref[...] + mod_ref[0, 2:3, :] * y
    x1_ref[...] = x1
    h = _rmsnorm_mod(x1, g_ref[...], mod_ref[0, 4:5, :], mod_ref[0, 3:4, :])
    _store_token_tiles(h_ref, h)

    h_hi, h_lo = _split_bf16(h)
    logits = _dot3(h_hi, h_lo, wrh_ref[...], wrl_ref[...]) + br_ref[...]
    lane = lax.broadcasted_iota(jnp.int32, (tm, V7X_LANES), 1)
    first_max = lambda v: jnp.argmax(v, axis=1, keepdims=True).astype(jnp.int32)
    g_sel = first_max(jnp.where(lane < N_GROUPS, logits, -jnp.inf))
    e_lane0 = ROUTER_EXPERT_LANE0 + EXPERTS_PER_GROUP * g_sel
    in_group = (lane >= e_lane0) & (lane < e_lane0 + EXPERTS_PER_GROUP)
    el = jnp.where(in_group, logits, -jnp.inf)
    i1 = first_max(el)
    i2 = first_max(jnp.where(lane == i1, -jnp.inf, el))
    lo = jnp.minimum(i1, i2) - e_lane0
    hi = jnp.maximum(i1, i2) - e_lane0
    bucket = g_sel * PAIRS_PER_GROUP + ((lo * (2 * EXPERTS_PER_GROUP - 1 - lo)) >> 1) + (hi - lo - 1)
    onehot = lane == bucket
    r_i = lax.broadcasted_iota(jnp.int32, (tm, tm), 0)
    c_i = lax.broadcasted_iota(jnp.int32, (tm, tm), 1)
    before = jnp.where(r_i > c_i, 1.0, 0.0).astype(BF16)
    oh = jnp.where(onehot, 1.0, 0.0)
    cum = _dot(before, oh.astype(BF16)) + carry_ref[...]
    rank = jnp.sum(jnp.where(onehot, cum, 0.0), axis=1, keepdims=True)
    bucket_ref[...] = bucket
    rank_ref[...] = rank.astype(jnp.int32)
    carry_ref[...] = carry_ref[...] + jnp.sum(oh, axis=0, keepdims=True)

    @pl.when(i == pl.num_programs(0) - 1)
    def _():
        cout_ref[...] = carry_ref[...]


def _oproj(a, b, x, mod, g, wo, wr_hi, wr_lo, br, counts_in, *, seq, mod_row, name="oproj"):
    t, d = x.shape
    tm = min(PROJ_TM, t if seq is None else seq)
    if seq is None:
        mod_map = lambda i: (mod_row, 0, 0)
    else:
        tiles_per_seq = seq // tm
        mod_map = lambda i: (i // tiles_per_seq, 0, 0)
    row = lambda i: (i, 0)
    const = lambda i: (0, 0)
    return pl.pallas_call(
        _oproj_body,
        out_shape=[jax.ShapeDtypeStruct((t, d), F32), jax.ShapeDtypeStruct((t * V7X_SUBLANES, V7X_LANES), F32),
                   jax.ShapeDtypeStruct((t, 1), jnp.int32), jax.ShapeDtypeStruct((t, 1), jnp.int32),
                   jax.ShapeDtypeStruct((1, V7X_LANES), F32)],
        grid=(t // tm,),
        in_specs=[pl.BlockSpec((tm, SEG), row), pl.BlockSpec((tm, SEG), row), pl.BlockSpec((tm, d), row),
                  pl.BlockSpec((1, N_MOD, d), mod_map), pl.BlockSpec((1, d), const),
                  pl.BlockSpec((2 * SEG, d), const), pl.BlockSpec((d, V7X_LANES), const),
                  pl.BlockSpec((d, V7X_LANES), const), pl.BlockSpec((1, V7X_LANES), const),
                  pl.BlockSpec((1, V7X_LANES), const)],
        out_specs=[pl.BlockSpec((tm, d), row), pl.BlockSpec((tm * V7X_SUBLANES, V7X_LANES), row),
                   pl.BlockSpec((tm, 1), row),
                   pl.BlockSpec((tm, 1), row), pl.BlockSpec((1, V7X_LANES), const)],
        scratch_shapes=[pltpu.VMEM((1, V7X_LANES), F32)],
        compiler_params=_cparams("arbitrary"), name=name,
    )(a, b, x, mod, g.reshape(1, d), wo, wr_hi, wr_lo, br, counts_in)


def _router_weights(wg, bg, we, be):
    d = wg.shape[0]
    w = jnp.zeros((d, V7X_LANES), F32)
    w = w.at[:, ROUTER_GROUP_LANE0:ROUTER_GROUP_LANE0 + N_GROUPS].set(wg)
    w = w.at[:, ROUTER_EXPERT_LANE0:ROUTER_EXPERT_LANE0 + N_EXPERTS].set(we)
    b = jnp.zeros((1, V7X_LANES), F32)
    b = b.at[0, ROUTER_GROUP_LANE0:ROUTER_GROUP_LANE0 + N_GROUPS].set(bg)
    b = b.at[0, ROUTER_EXPERT_LANE0:ROUTER_EXPERT_LANE0 + N_EXPERTS].set(be)
    w_hi = w.astype(BF16)
    w_lo = (w - w_hi.astype(F32)).astype(BF16)
    return w_hi, w_lo, b


def _token_tile(ref, t):
    return ref.at[pl.ds(pl.multiple_of(t * V7X_SUBLANES, V7X_SUBLANES), V7X_SUBLANES)]


def _scatter_body(off_ref, bucket_ref, rank_ref, rows_ref, sorted_hbm, out_hbm, sem, *, chunk):
    del sorted_hbm

    def start(t, carry):
        p = off_ref[bucket_ref[t]] + rank_ref[t]
        pltpu.make_async_copy(_token_tile(rows_ref, t), _token_tile(out_hbm, p), sem).start()
        return carry

    lax.fori_loop(0, chunk, start, 0, unroll=8)
    pltpu.make_async_copy(rows_ref, out_hbm.at[pl.ds(0, chunk * V7X_SUBLANES)], sem).wait()


def _gather_body(off_ref, bucket_ref, rank_ref, sorted_hbm, out_ref, sem, *, chunk):
    def start(t, carry):
        p = off_ref[bucket_ref[t]] + rank_ref[t]
        pltpu.make_async_copy(_token_tile(sorted_hbm, p), _token_tile(out_ref, t), sem).start()
        return carry

    lax.fori_loop(0, chunk, start, 0, unroll=8)
    pltpu.make_async_copy(sorted_hbm.at[pl.ds(0, chunk * V7X_SUBLANES)], out_ref, sem).wait()


def _permute(offsets, bucket, rank, rows, sorted_rows, *, n_rows, scatter, name):
    chunk = min(PERM_CHUNK, n_rows)
    smem_blk = pl.BlockSpec((chunk,), lambda i, *_: (i,), memory_space=pltpu.SMEM)
    any_spec = pl.BlockSpec(memory_space=pl.ANY)
    vmem_blk = pl.BlockSpec((chunk * V7X_SUBLANES, V7X_LANES), lambda i, *_: (i, 0))
    if scatter:
        body = functools.partial(_scatter_body, chunk=chunk)
        args = (offsets, bucket, rank, rows, sorted_rows)
        in_specs = [smem_blk, smem_blk, vmem_blk, any_spec]
        out_shape = jax.ShapeDtypeStruct(sorted_rows.shape, sorted_rows.dtype)
        out_specs = any_spec
        aliases = {4: 0}
    else:
        body = functools.partial(_gather_body, chunk=chunk)
        args = (offsets, bucket, rank, sorted_rows)
        in_specs = [smem_blk, smem_blk, any_spec]
        out_shape = jax.ShapeDtypeStruct((n_rows * V7X_SUBLANES, V7X_LANES), sorted_rows.dtype)
        out_specs = vmem_blk
        aliases = {}
    grid_spec = pltpu.PrefetchScalarGridSpec(
        num_scalar_prefetch=1, grid=(n_rows // chunk,), in_specs=in_specs, out_specs=out_specs,
        scratch_shapes=[pltpu.SemaphoreType.DMA])
    return pl.pallas_call(
        body, out_shape=out_shape, grid_spec=grid_spec, input_output_aliases=aliases,
        compiler_params=pltpu.CompilerParams(dimension_semantics=("arbitrary",), has_side_effects=True,
                                             vmem_limit_bytes=V7X_VMEM_LIMIT_BYTES),
        name=name,
    )(*args)


def _moe_body(te1_ref, te2_ref, tval_ref, trow_ref, x_ref, w1a_ref, w1b_ref, w3a_ref, w3b_ref, w2a_ref, w2b_ref,
              wr_ref, br_ref, y_ref):
    i = pl.program_id(0)

    @pl.when(tval_ref[i] == 0)
    def _():
        y_ref[...] = jnp.zeros_like(y_ref)

    @pl.when(tval_ref[i] > 0)
    def _():
        tm = x_ref.shape[0] // V7X_SUBLANES
        x = _load_token_tiles(x_ref).astype(BF16)
        e1 = te1_ref[i]
        e2 = te2_ref[i]
        grp = e1 >> 3
        logits = _dot(x, wr_ref[...]) + br_ref[...]
        lane = lax.broadcasted_iota(jnp.int32, (tm, V7X_LANES), 1)
        pick = lambda idx: jnp.sum(jnp.where(lane == idx, logits, 0.0), axis=1, keepdims=True)
        gl = jnp.where(lane < N_GROUPS, logits, -jnp.inf)
        gm = jnp.max(gl, axis=1, keepdims=True)
        gz = jnp.sum(jnp.exp(gl - gm), axis=1, keepdims=True)
        g_w = jnp.exp(pick(ROUTER_GROUP_LANE0 + grp) - gm) / gz
        l1 = pick(ROUTER_EXPERT_LANE0 + e1)
        l2 = pick(ROUTER_EXPERT_LANE0 + e2)
        m = jnp.maximum(l1, l2)
        p1 = jnp.exp(l1 - m)
        p2 = jnp.exp(l2 - m)
        c1 = g_w * p1 / (p1 + p2)
        c2 = g_w * p2 / (p1 + p2)

        def expert(w1_ref, w3_ref, w2_ref):
            h = _dot(x, w1_ref[0])
            hid = (h * _sigmoid(h)) * _dot(x, w3_ref[0])
            return _dot(hid.astype(BF16), w2_ref[0])

        _store_token_tiles(y_ref, c1 * expert(w1a_ref, w3a_ref, w2a_ref) + c2 * expert(w1b_ref, w3b_ref, w2b_ref))


def _moe_sorted(xs, te1, te2, tval, trow, w1, w3, w2, wr_hi, br):
    d = w1.shape[1]
    n_tiles = xs.shape[0] // (MOE_TM * V7X_SUBLANES)
    tile_blk = (MOE_TM * V7X_SUBLANES, V7X_LANES)
    xmap = lambda i, te1, te2, tval, trow: (trow[i], 0)
    wa = lambda i, te1, te2, tval, trow: (te1[i], 0, 0)
    wb = lambda i, te1, te2, tval, trow: (te2[i], 0, 0)
    const = lambda i, *_: (0, 0)
    grid_spec = pltpu.PrefetchScalarGridSpec(
        num_scalar_prefetch=4, grid=(n_tiles,),
        in_specs=[pl.BlockSpec(tile_blk, xmap),
                  pl.BlockSpec((1, d, D_EXPERT), wa), pl.BlockSpec((1, d, D_EXPERT), wb),
                  pl.BlockSpec((1, d, D_EXPERT), wa), pl.BlockSpec((1, d, D_EXPERT), wb),
                  pl.BlockSpec((1, D_EXPERT, d), wa), pl.BlockSpec((1, D_EXPERT, d), wb),
                  pl.BlockSpec((d, V7X_LANES), const), pl.BlockSpec((1, V7X_LANES), const)],
        out_specs=pl.BlockSpec(tile_blk, lambda i, *_: (i, 0)))
    return pl.pallas_call(
        _moe_body, out_shape=jax.ShapeDtypeStruct(xs.shape, F32), grid_spec=grid_spec,
        compiler_params=_cparams("arbitrary"), name="moe_experts",
    )(te1, te2, tval, trow, xs, w1, w1, w3, w3, w2, w2, wr_hi, br)


def _moe(h_parts, route_parts, counts, w1, w3, w2, wr_hi, br):
    n_rows = [h.shape[0] // V7X_SUBLANES for h in h_parts]
    n_tiles = sum(n_rows) // MOE_TM + N_BUCKETS
    cnt = counts[0, :N_BUCKETS].astype(jnp.int32)
    tiles_b = (cnt + MOE_TM - 1) // MOE_TM
    tile_end = jnp.cumsum(tiles_b)
    n_valid = tile_end[-1]
    offsets = jnp.zeros((V7X_LANES,), jnp.int32).at[:N_BUCKETS].set((tile_end - tiles_b) * MOE_TM)
    tile_id = jnp.arange(n_tiles, dtype=jnp.int32)
    tval = (tile_id < n_valid).astype(jnp.int32)
    trow = jnp.minimum(tile_id, n_valid - 1)
    tbucket = jnp.sum((trow[:, None] >= tile_end[None, :]).astype(jnp.int32), axis=1)
    tbucket = jnp.minimum(tbucket, N_BUCKETS - 1)
    te1 = jnp.asarray(_BUCKET_E1)[tbucket]
    te2 = jnp.asarray(_BUCKET_E2)[tbucket]

    xs = jnp.zeros((n_tiles * MOE_TM * V7X_SUBLANES, V7X_LANES), F32)
    for h, n, (bucket, rank) in zip(h_parts, n_rows, route_parts):
        xs = _permute(offsets, bucket, rank, h, xs, n_rows=n, scatter=True, name="moe_scatter_rows")
    ys = _moe_sorted(xs, te1, te2, tval, trow, w1, w3, w2, wr_hi, br)
    return [_permute(offsets, bucket, rank, None, ys, n_rows=n, scatter=False, name="moe_gather_rows")
            for n, (bucket, rank) in zip(n_rows, route_parts)]


def _final_body(x_ref, y_ref, mod_ref, g_ref, o_ref):
    x = x_ref[...] + mod_ref[0, 5:6, :] * _load_token_tiles(y_ref)
    o_ref[...] = x * lax.rsqrt(jnp.mean(x * x, axis=-1, keepdims=True) + NORM_EPS) * g_ref[...]


def _final(x, y, mod, g, *, seq):
    t, d = x.shape
    tm = min(PROJ_TM, seq)
    tiles_per_seq = seq // tm
    row = lambda i: (i, 0)
    return pl.pallas_call(
        _final_body, out_shape=jax.ShapeDtypeStruct((t, d), F32), grid=(t // tm,),
        in_specs=[pl.BlockSpec((tm, d), row), pl.BlockSpec((tm * V7X_SUBLANES, V7X_LANES), row),
                  pl.BlockSpec((1, N_MOD, d), lambda i: (i // tiles_per_seq, 0, 0)),
                  pl.BlockSpec((1, d), lambda i: (0, 0))],
        out_specs=pl.BlockSpec((tm, d), row),
        compiler_params=_cparams("arbitrary"), name="final_norm",
    )(x, y, mod, g.reshape(1, d))


def kernel(x, c, ctx, c_ctx, w_mod, b_mod, norm_mix, norm_ffn, norm_final, even_w_in, even_w_out, even_rpb,
           even_dw, even_dw_b, even_ln_g, even_ln_b, odd_w_in, odd_w_out, odd_conv, odd_lambda, odd_subln,
           moe_wg, moe_bg, moe_we, moe_be, moe_w1, moe_w3, moe_w2):
    bsz, n_tok, d = x.shape
    ctx_len = ctx.shape[1]
    depth = w_mod.shape[0]
    assert depth == 2 and d == D_MODEL
    xs = x.reshape(bsz * n_tok, d)
    cs = ctx.reshape(bsz * ctx_len, d)

    mod_rows = -(-(bsz + 1) // 8) * 8
    cc = jnp.zeros((mod_rows, d), F32).at[:bsz].set(c).at[bsz].set(c_ctx)
    mod = _modulation(cc, w_mod, b_mod).reshape(depth, mod_rows, N_MOD, d)
    bf = lambda a: a.astype(BF16)
    zero_counts = jnp.zeros((1, V7X_LANES), F32)

    l = 0
    w_in = bf(even_w_in[0])
    w_out = bf(even_w_out[0])
    px = _proj(xs, mod[l], norm_mix[l], w_in, seq=n_tok, mod_row=None, q_seg=0, name="proj_even_x")
    pc = _proj(cs, mod[l], norm_mix[l], w_in, seq=None, mod_row=bsz, q_seg=0, name="proj_even_c")
    a_x = _natten(px, pc, even_rpb[0], bsz=bsz, n_tok=n_tok, ctx_len=ctx_len)
    a_c = _ctxattn(pc, bsz=bsz, ctx_len=ctx_len)
    b_x = _conformer(px, even_dw[0], even_dw_b[0], even_ln_g[0], even_ln_b[0], bsz=bsz, seq=n_tok)
    b_c = _conformer(pc, even_dw[0], even_dw_b[0], even_ln_g[0], even_ln_b[0], bsz=bsz, seq=ctx_len)
    wr_hi, wr_lo, br = _router_weights(moe_wg[l], moe_bg[l], moe_we[l], moe_be[l])
    x1, hx, bkt_x, rnk_x, counts = _oproj(a_x, b_x, xs, mod[l], norm_ffn[l], w_out, wr_hi, wr_lo, br, zero_counts,
                                          seq=n_tok, mod_row=None, name="oproj_even_x")
    c1, hc, bkt_c, rnk_c, counts = _oproj(a_c, b_c, cs, mod[l], norm_ffn[l], w_out, wr_hi, wr_lo, br, counts,
                                          seq=None, mod_row=bsz, name="oproj_even_c")
    y_x, y_c = _moe([hx, hc], [(bkt_x.reshape(-1), rnk_x.reshape(-1)), (bkt_c.reshape(-1), rnk_c.reshape(-1))],
                    counts, bf(moe_w1[l]), bf(moe_w3[l]), bf(moe_w2[l]), wr_hi, br)

    l = 1
    lam_init = 0.8 - 0.6 * math.exp(-0.3 * l)
    w_in = bf(odd_w_in[0])
    w_out = bf(odd_w_out[0])
    rope = _rope_tables(n_tok)
    px, x2 = _proj(x1, mod[l], norm_mix[l], w_in, seq=n_tok, mod_row=None, y=y_x, res_mod=mod[l - 1], res_idx=5,
                   rope=rope, rope_segs=(3, 4), q_seg=3, write_x=True, name="proj_odd_x")
    pc = _proj(c1, mod[l], norm_mix[l], w_in[:, 4 * SEG:], seq=None, mod_row=bsz, y=y_c, res_mod=mod[l - 1],
               res_idx=5, name="proj_odd_c")
    s_x = _shortconv(px, odd_conv[0], bsz=bsz, seq=n_tok)
    d_x = _diffattn(px, pc, odd_lambda[0], odd_subln[0], lam_init, bsz=bsz, n_tok=n_tok, ctx_len=ctx_len)
    wr_hi, wr_lo, br = _router_weights(moe_wg[l], moe_bg[l], moe_we[l], moe_be[l])
    x3, hx, bkt_x, rnk_x, counts = _oproj(s_x, d_x, x2, mod[l], norm_ffn[l], w_out, wr_hi, wr_lo, br, zero_counts,
                                          seq=n_tok, mod_row=None, name="oproj_odd_x")
    (y_x,) = _moe([hx], [(bkt_x.reshape(-1), rnk_x.reshape(-1))], counts,
                  bf(moe_w1[l]), bf(moe_w3[l]), bf(moe_w2[l]), wr_hi, br)
    out = _final(x3, y_x, mod[l], norm_final, seq=n_tok)
    return out.reshape(bsz, n_tok, d)
```

```python
import functools
import math

import numpy as np
import jax
import jax.numpy as jnp
from jax import lax
from jax.experimental import pallas as pl
from jax.experimental.pallas import tpu as pltpu

F32 = jnp.float32
BF16 = jnp.bfloat16

D_MODEL = 1024
GRID_W = 64
N_MOD = 6
NORM_EPS = 1e-6
NEG_INF = -1e30
ROPE_BASE = 10000.0
SEG = 512
A_HEAD_DIM = 64
A_HEADS = 8
WIN_ROWS_MAX = 8
WIN_COLS = 16
B_CONV = 31
C_CONV = 3
D_QK_DIM = 64
D_V_DIM = 128
D_HEADS = 4
N_GROUPS = 4
EXPERTS_PER_GROUP = 8
N_EXPERTS = 32
D_EXPERT = 256
LOG2E = math.log2(math.e)
QUERY_SCALE = A_HEAD_DIM ** -0.5 * LOG2E
PAIRS_PER_GROUP = EXPERTS_PER_GROUP * (EXPERTS_PER_GROUP - 1) // 2
N_BUCKETS = N_GROUPS * PAIRS_PER_GROUP

V7X_LANES = 128
V7X_SUBLANES = 8
V7X_VMEM_LIMIT_BYTES = 56 * 1024 * 1024

PROJ_TM = 512
ATT_TQ = 256
DIFF_TQ = 512
NAT_KROWS = 12
CONV_CHUNK = 64
CONV_PAD = 16
ROUTE_BLOCKS = 1
MOE_TM = 256
PERM_CHUNK = 1024
ROUTER_GROUP_LANE0 = 0
ROUTER_EXPERT_LANE0 = 8

_PAIR_LO = np.array([i for i in range(8) for j in range(i + 1, 8)], np.int32)
_PAIR_HI = np.array([j for i in range(8) for j in range(i + 1, 8)], np.int32)
_BUCKET_E1 = np.concatenate([g * 8 + _PAIR_LO for g in range(N_GROUPS)]).astype(np.int32)
_BUCKET_E2 = np.concatenate([g * 8 + _PAIR_HI for g in range(N_GROUPS)]).astype(np.int32)


def _cparams(*sem):
    return pltpu.CompilerParams(dimension_semantics=tuple(sem), vmem_limit_bytes=V7X_VMEM_LIMIT_BYTES)


def _dot(a, b):
    return jnp.dot(a, b, preferred_element_type=F32)


def _dot_t(a, b):
    return lax.dot_general(a, b, (((1,), (1,)), ((), ())), preferred_element_type=F32)


def _split_bf16(a):
    hi = a.astype(BF16)
    lo = (a - hi.astype(F32)).astype(BF16)
    return hi, lo


def _dot3(a_hi, a_lo, b_hi, b_lo):
    return _dot(a_hi, b_hi) + _dot(a_lo, b_hi) + _dot(a_hi, b_lo)


def _sigmoid(x):
    return 1.0 / (1.0 + jnp.exp(-x))


def _load_token_tiles(ref):
    tm = ref.shape[0] // V7X_SUBLANES
    return jnp.concatenate([ref[pl.ds(c, tm, stride=V7X_SUBLANES), :] for c in range(V7X_SUBLANES)], axis=1)


def _store_token_tiles(ref, v):
    tm = v.shape[0]
    for c in range(V7X_SUBLANES):
        ref[pl.ds(c, tm, stride=V7X_SUBLANES), :] = v[:, c * V7X_LANES:(c + 1) * V7X_LANES]


def _rmsnorm_mod(x, g, scale, shift):
    y = x * lax.rsqrt(jnp.mean(x * x, axis=-1, keepdims=True) + NORM_EPS)
    return (y * g) * (1.0 + scale) + shift


def _cast_body(w_ref, o_ref):
    o_ref[...] = w_ref[0].astype(BF16)


def _expert_weights_bf16(w, layer):
    _, n_exp, k, n = w.shape
    return pl.pallas_call(
        _cast_body, out_shape=jax.ShapeDtypeStruct((n_exp, k, n), BF16), grid=(n_exp,),
        in_specs=[pl.BlockSpec((1, 1, k, n), lambda e: (layer, e, 0, 0))],
        out_specs=pl.BlockSpec((1, k, n), lambda e: (e, 0, 0)),
        compiler_params=_cparams("arbitrary"), name="expert_weights_bf16",
    )(w)


def _mod_body(c_ref, w_ref, b_ref, o_ref):
    c = c_ref[...]
    s = c * _sigmoid(c)
    s_hi, s_lo = _split_bf16(s)
    w_hi, w_lo = _split_bf16(w_ref[0])
    o_ref[0] = _dot3(s_hi, s_lo, w_hi, w_lo) + b_ref[0]


def _modulation(cc, w_mod, b_mod):
    depth, d, n = w_mod.shape
    rows = cc.shape[0]
    tn = 1536
    return pl.pallas_call(
        _mod_body,
        out_shape=jax.ShapeDtypeStruct((depth, rows, n), F32),
        grid=(depth, n // tn),
        in_specs=[pl.BlockSpec((rows, d), lambda l, j: (0, 0)),
                  pl.BlockSpec((1, d, tn), lambda l, j: (l, 0, j)),
                  pl.BlockSpec((1, 1, tn), lambda l, j: (l, 0, j))],
        out_specs=pl.BlockSpec((1, rows, tn), lambda l, j: (l, 0, j)),
        compiler_params=_cparams("arbitrary", "arbitrary"),
        name="modulation",
    )(cc, w_mod, b_mod.reshape(depth, 1, n))


def _rope(v, cos, sin, lane):
    up = pltpu.roll(v, V7X_LANES - 16, axis=1)
    dn = pltpu.roll(v, 16, axis=1)
    sw = jnp.where((lane & 31) < 16, up, dn)
    return v * cos + sw * sin


def _proj_body(*refs, n_seg, res_idx, shift_idx, rope_segs, q_seg, write_x):
    refs = list(refs)
    x_ref = refs.pop(0)
    y_ref = refs.pop(0) if res_idx is not None else None
    rmod_ref = refs.pop(0) if res_idx is not None else None
    mod_ref = refs.pop(0)
    g_ref = refs.pop(0)
    w_ref = refs.pop(0)
    cos_ref = sin_ref = None
    if rope_segs:
        cos_ref = refs.pop(0)
        sin_ref = refs.pop(0)
    out_ref = refs.pop(0)
    xo_ref = refs.pop(0) if write_x else None

    x = x_ref[...]
    if y_ref is not None:
        x = x + rmod_ref[0, res_idx:res_idx + 1, :] * _load_token_tiles(y_ref)
        if write_x:
            xo_ref[...] = x
    h = _rmsnorm_mod(x, g_ref[...], mod_ref[0, shift_idx + 1:shift_idx + 2, :],
                     mod_ref[0, shift_idx:shift_idx + 1, :]).astype(BF16)
    for s in range(n_seg):
        o = _dot(h, w_ref[:, s * SEG:(s + 1) * SEG])
        if s == q_seg:
            o = o * QUERY_SCALE
        if s in rope_segs:
            lane = lax.broadcasted_iota(jnp.int32, (o.shape[0], V7X_LANES), 1)
            cos = cos_ref[...]
            sin = sin_ref[...]
            o = jnp.concatenate(
                [_rope(o[:, c * V7X_LANES:(c + 1) * V7X_LANES], cos, sin, lane) for c in range(SEG // V7X_LANES)],
                axis=1)
        out_ref[:, s * SEG:(s + 1) * SEG] = o.astype(BF16)


def _proj(x, mod, g, w, *, seq, mod_row, y=None, res_mod=None, res_idx=None, shift_idx=0, rope=None, rope_segs=(), q_seg=None,
          write_x=False, name="proj"):
    t, d = x.shape
    n = w.shape[1]
    tm = min(PROJ_TM, t if seq is None else seq)
    tiles_per_seq = None if seq is None else seq // tm
    if seq is None:
        mod_map = lambda i: (mod_row, 0, 0)
    else:
        mod_map = lambda i: (i // tiles_per_seq, 0, 0)
    args = [x]
    specs = [pl.BlockSpec((tm, d), lambda i: (i, 0))]
    if y is not None:
        args += [y, res_mod]
        specs += [pl.BlockSpec((tm * V7X_SUBLANES, V7X_LANES), lambda i: (i, 0)),
                  pl.BlockSpec((1, N_MOD, d), mod_map)]
    args += [mod, g.reshape(1, d), w]
    specs += [pl.BlockSpec((1, N_MOD, d), mod_map), pl.BlockSpec((1, d), lambda i: (0, 0)),
              pl.BlockSpec((d, n), lambda i: (0, 0))]
    if rope_segs:
        args += [rope[0], rope[1]]
        specs += [pl.BlockSpec((tm, V7X_LANES), lambda i: (i % tiles_per_seq, 0))] * 2
    out_shape = [jax.ShapeDtypeStruct((t, n), BF16)]
    out_specs = [pl.BlockSpec((tm, n), lambda i: (i, 0))]
    if write_x:
        out_shape.append(jax.ShapeDtypeStruct((t, d), F32))
        out_specs.append(pl.BlockSpec((tm, d), lambda i: (i, 0)))
    body = functools.partial(_proj_body, n_seg=n // SEG, res_idx=res_idx if y is not None else None,
                             shift_idx=shift_idx, rope_segs=tuple(rope_segs), q_seg=q_seg, write_x=write_x)
    outs = pl.pallas_call(
        body, out_shape=out_shape, grid=(t // tm,), in_specs=specs, out_specs=out_specs,
        compiler_params=_cparams("arbitrary"), name=name,
    )(*args)
    return outs if write_x else outs[0]


def _rope_tables(n_tok):
    quarter = D_QK_DIM // 4
    t = np.arange(n_tok)
    row = (t // GRID_W).astype(np.float32)
    col = (t % GRID_W).astype(np.float32)
    inv = jnp.power(ROPE_BASE, -jnp.arange(quarter, dtype=F32) / quarter)
    ar = jnp.asarray(row)[:, None] * inv
    ac = jnp.asarray(col)[:, None] * inv
    cos64 = jnp.concatenate([jnp.cos(ar), jnp.cos(ar), jnp.cos(ac), jnp.cos(ac)], axis=-1)
    sin64 = jnp.concatenate([-jnp.sin(ar), jnp.sin(ar), -jnp.sin(ac), jnp.sin(ac)], axis=-1)
    return jnp.tile(cos64, (1, V7X_LANES // D_QK_DIM)), jnp.tile(sin64, (1, V7X_LANES // D_QK_DIM))


def _stack_halves(q):
    lane = lax.broadcasted_iota(jnp.int32, q.shape, 1)
    zero = jnp.zeros_like(q)
    return jnp.concatenate([jnp.where(lane < 64, q, zero), jnp.where(lane >= 64, q, zero)], axis=0)


def _attend(scores, values):
    m = functools.reduce(jnp.maximum, [jnp.max(s, axis=1, keepdims=True) for s in scores])
    acc = None
    for s, v in zip(scores, values):
        p = jnp.exp2(s - m).astype(BF16)
        lane = lax.broadcasted_iota(jnp.int32, v.shape, 1)
        ones_col = jnp.where(lane == 0, 1.0, 0.0).astype(BF16)
        o = _dot(p, jnp.concatenate([v, ones_col], axis=1))
        acc = o if acc is None else acc + o
    return acc[:, :V7X_LANES], acc[:, V7X_LANES:V7X_LANES + 1]


def _natten_body(ks_ref, pid_ref, q_ref, k_ref, v_ref, kc_ref, vc_ref, bias_ref, o_ref):
    j = pl.program_id(1)
    k0 = pl.multiple_of(ks_ref[j] * GRID_W, GRID_W)
    p = pid_ref[j]
    nk = NAT_KROWS * GRID_W
    tq = q_ref.shape[0]
    lane = lax.broadcasted_iota(jnp.int32, (tq, V7X_LANES), 1)
    for hp in range(A_HEADS // 2):
        cs = slice(hp * V7X_LANES, (hp + 1) * V7X_LANES)
        q2 = _stack_halves(q_ref[:, cs])
        s_c = _dot_t(q2, kc_ref[:, cs])
        s_l = _dot_t(q2, k_ref[pl.ds(k0, nk), cs])
        bias = jnp.concatenate([bias_ref[p, 2 * hp], bias_ref[p, 2 * hp + 1]], axis=0).astype(F32)
        o2, l = _attend([s_c, s_l + bias], [vc_ref[:, cs], v_ref[pl.ds(k0, nk), cs]])
        o2 = o2 / l
        o_ref[:, cs] = jnp.where(lane < 64, o2[:tq], o2[tq:]).astype(BF16)


def _natten_tables(rows):
    q_rows = ATT_TQ // GRID_W
    nqb = rows // q_rows
    win_r = min(WIN_ROWS_MAX, rows)
    col = np.arange(GRID_W)
    c_start = np.clip(col - WIN_COLS // 2, 0, GRID_W - WIN_COLS)
    col_ok = (col[None, :] >= c_start[:, None]) & (col[None, :] < c_start[:, None] + WIN_COLS)
    dc = np.clip(col[None, :] - col[:, None], -(WIN_COLS - 1), WIN_COLS - 1) + WIN_COLS - 1
    ks_list, pats, pid = [], [], []
    for j in range(nqb):
        r0 = j * q_rows
        ks = int(np.clip(r0 - win_r // 2, 0, rows - NAT_KROWS))
        qr = r0 + np.arange(q_rows)
        rs = np.clip(qr - win_r // 2, 0, rows - win_r)
        kr = ks + np.arange(NAT_KROWS)
        row_ok = (kr[None, :] >= rs[:, None]) & (kr[None, :] < rs[:, None] + win_r)
        dr = np.clip(kr[None, :] - qr[:, None] + WIN_ROWS_MAX - 1, 0, 2 * WIN_ROWS_MAX - 2)
        key = (row_ok.tobytes(), (dr * row_ok).tobytes())
        for n, (k_, *_rest) in enumerate(pats):
            if k_ == key:
                pid.append(n)
                break
        else:
            pid.append(len(pats))
            pats.append((key, row_ok, dr))
        ks_list.append(ks)
    row_ok = np.stack([p_[1] for p_ in pats])
    dr = np.stack([p_[2] for p_ in pats])
    return np.array(ks_list, np.int32), np.array(pid, np.int32), row_ok, dr, col_ok, dc


def _natten_bias(rpb, row_ok, dr, col_ok, dc):
    n_pat, q_rows, k_rows = dr.shape
    n_dr, n_dc = rpb.shape[1], rpb.shape[2]
    oh_r = np.zeros((n_pat * q_rows * k_rows, n_dr), np.float32)
    oh_r[np.arange(oh_r.shape[0]), dr.reshape(-1)] = 1.0
    oh_c = np.zeros((n_dc, GRID_W * GRID_W), np.float32)
    oh_c[dc.reshape(-1), np.arange(GRID_W * GRID_W)] = 1.0
    hi = lax.Precision.HIGHEST
    t1 = jnp.einsum("mr,hrc->hmc", jnp.asarray(oh_r), rpb, precision=hi)
    t2 = jnp.einsum("hmc,cn->hmn", t1, jnp.asarray(oh_c), precision=hi)
    t2 = t2.reshape(rpb.shape[0], n_pat, q_rows, k_rows, GRID_W, GRID_W)
    valid = row_ok[None, :, :, :, None, None] & col_ok[None, None, None, None, :, :]
    bias = jnp.where(valid, t2 * LOG2E, NEG_INF).transpose(1, 0, 2, 4, 3, 5)
    return bias.reshape(n_pat, rpb.shape[0], q_rows * GRID_W, k_rows * GRID_W).astype(BF16)


def _natten(px, pc, rpb, *, bsz, n_tok, ctx_len):
    rows = n_tok // GRID_W
    assert rows >= NAT_KROWS and n_tok % ATT_TQ == 0
    ks, pid, row_ok, dr, col_ok, dc = _natten_tables(rows)
    bias = _natten_bias(rpb, row_ok, dr, col_ok, dc)
    n_pat = bias.shape[0]
    nqb = n_tok // ATT_TQ
    nk = NAT_KROWS * GRID_W
    grid_spec = pltpu.PrefetchScalarGridSpec(
        num_scalar_prefetch=2, grid=(bsz, nqb),
        in_specs=[pl.BlockSpec((ATT_TQ, SEG), lambda b, j, *_: (b * nqb + j, 0)),
                  pl.BlockSpec((n_tok, SEG), lambda b, j, *_: (b, 1)),
                  pl.BlockSpec((n_tok, SEG), lambda b, j, *_: (b, 2)),
                  pl.BlockSpec((ctx_len, SEG), lambda b, j, *_: (b, 1)),
                  pl.BlockSpec((ctx_len, SEG), lambda b, j, *_: (b, 2)),
                  pl.BlockSpec((n_pat, A_HEADS, ATT_TQ, nk), lambda b, j, *_: (0, 0, 0, 0))],
        out_specs=pl.BlockSpec((ATT_TQ, SEG), lambda b, j, *_: (b * nqb + j, 0)))
    return pl.pallas_call(
        _natten_body, out_shape=jax.ShapeDtypeStruct((bsz * n_tok, SEG), BF16), grid_spec=grid_spec,
        compiler_params=_cparams("arbitrary", "arbitrary"), name="natten",
    )(jnp.asarray(ks), jnp.asarray(pid), px, px, px, pc, pc, bias)


def _ctxattn_body(q_ref, k_ref, v_ref, o_ref):
    tq = q_ref.shape[0]
    lane = lax.broadcasted_iota(jnp.int32, (tq, V7X_LANES), 1)
    for hp in range(A_HEADS // 2):
        cs = slice(hp * V7X_LANES, (hp + 1) * V7X_LANES)
        q2 = _stack_halves(q_ref[:, cs])
        o2, l = _attend([_dot_t(q2, k_ref[:, cs])], [v_ref[:, cs]])
        o2 = o2 / l
        o_ref[:, cs] = jnp.where(lane < 64, o2[:tq], o2[tq:]).astype(BF16)


def _ctxattn(pc, *, bsz, ctx_len):
    return pl.pallas_call(
        _ctxattn_body, out_shape=jax.ShapeDtypeStruct((bsz * ctx_len, SEG), BF16), grid=(bsz,),
        in_specs=[pl.BlockSpec((ctx_len, SEG), lambda b: (b, 0)),
                  pl.BlockSpec((ctx_len, SEG), lambda b: (b, 1)),
                  pl.BlockSpec((ctx_len, SEG), lambda b: (b, 2))],
        out_specs=pl.BlockSpec((ctx_len, SEG), lambda b: (b, 0)),
        compiler_params=_cparams("arbitrary"), name="ctxattn",
    )(pc, pc, pc)


def _conv_accumulate(z_ref, w_ref, t0, taps):
    base = CONV_PAD - taps // 2
    blocks = []
    for c in range(SEG // V7X_LANES):
        cs = slice(c * V7X_LANES, (c + 1) * V7X_LANES)
        win = z_ref[pl.ds(t0, CONV_CHUNK + 2 * CONV_PAD), cs]
        acc = None
        for b in range(V7X_SUBLANES):
            ks = [k for k in range(taps) if (base + k) % V7X_SUBLANES == b]
            if not ks:
                continue
            shifted = win if b == 0 else pltpu.roll(win, win.shape[0] - b, axis=0)
            for k in ks:
                a0 = (base + k) // V7X_SUBLANES * V7X_SUBLANES
                term = shifted[a0:a0 + CONV_CHUNK, :] * w_ref[k:k + 1, cs]
                acc = term if acc is None else acc + term
        blocks.append(acc)
    return jnp.concatenate(blocks, axis=1)


def _conformer_body(u_ref, g_ref, w_ref, b_ref, lg_ref, lb_ref, o_ref, z_ref):
    n = u_ref.shape[0]
    halo = jnp.zeros((CONV_PAD, SEG), F32)
    z_ref[pl.ds(0, CONV_PAD), :] = halo
    z_ref[pl.ds(CONV_PAD + n, CONV_PAD), :] = halo
    z_ref[pl.ds(CONV_PAD, n), :] = u_ref[...].astype(F32) * _sigmoid(g_ref[...].astype(F32))

    def chunk(i, carry):
        t0 = pl.multiple_of(i * CONV_CHUNK, CONV_CHUNK)
        y = _conv_accumulate(z_ref, w_ref, t0, B_CONV) + b_ref[...]
        mu = jnp.mean(y, axis=-1, keepdims=True)
        yc = y - mu
        var = jnp.mean(yc * yc, axis=-1, keepdims=True)
        y = yc * lax.rsqrt(var + NORM_EPS) * lg_ref[...] + lb_ref[...]
        o_ref[pl.ds(t0, CONV_CHUNK), :] = (y * _sigmoid(y)).astype(BF16)
        return carry

    lax.fori_loop(0, n // CONV_CHUNK, chunk, 0)


def _conformer(p, dw, dw_b, ln_g, ln_b, *, bsz, seq):
    vec = lambda a: a.reshape(1, SEG)
    return pl.pallas_call(
        _conformer_body, out_shape=jax.ShapeDtypeStruct((bsz * seq, SEG), BF16), grid=(bsz,),
        in_specs=[pl.BlockSpec((seq, SEG), lambda b: (b, 3)),
                  pl.BlockSpec((seq, SEG), lambda b: (b, 4)),
                  pl.BlockSpec((B_CONV, SEG), lambda b: (0, 0))] + [pl.BlockSpec((1, SEG), lambda b: (0, 0))] * 3,
        out_specs=pl.BlockSpec((seq, SEG), lambda b: (b, 0)),
        scratch_shapes=[pltpu.VMEM((seq + 2 * CONV_PAD, SEG), F32)],
        compiler_params=_cparams("arbitrary"), name="conformer_conv",
    )(p, p, dw, vec(dw_b), vec(ln_g), vec(ln_b))


def _shortconv_body(bg_ref, cg_ref, u_ref, w_ref, o_ref, z_ref):
    n = u_ref.shape[0]
    halo = jnp.zeros((CONV_PAD, SEG), F32)
    z_ref[pl.ds(0, CONV_PAD), :] = halo
    z_ref[pl.ds(CONV_PAD + n, CONV_PAD), :] = halo
    z_ref[pl.ds(CONV_PAD, n), :] = cg_ref[...].astype(F32) * u_ref[...].astype(F32)

    def chunk(i, carry):
        t0 = pl.multiple_of(i * CONV_CHUNK, CONV_CHUNK)
        y = _conv_accumulate(z_ref, w_ref, t0, C_CONV)
        o_ref[pl.ds(t0, CONV_CHUNK), :] = (bg_ref[pl.ds(t0, CONV_CHUNK), :].astype(F32) * y).astype(BF16)
        return carry

    lax.fori_loop(0, n // CONV_CHUNK, chunk, 0)


def _shortconv(p, conv_w, *, bsz, seq):
    return pl.pallas_call(
        _shortconv_body, out_shape=jax.ShapeDtypeStruct((bsz * seq, SEG), BF16), grid=(bsz,),
        in_specs=[pl.BlockSpec((seq, SEG), lambda b: (b, 0)),
                  pl.BlockSpec((seq, SEG), lambda b: (b, 1)),
                  pl.BlockSpec((seq, SEG), lambda b: (b, 2)),
                  pl.BlockSpec((C_CONV, SEG), lambda b: (0, 0))],
        out_specs=pl.BlockSpec((seq, SEG), lambda b: (b, 0)),
        scratch_shapes=[pltpu.VMEM((seq + 2 * CONV_PAD, SEG), F32)],
        compiler_params=_cparams("arbitrary"), name="short_conv",
    )(p, p, p, conv_w)


def _diffattn_body(q_ref, k_ref, v_ref, kc_ref, vc_ref, lam_ref, g_ref, o_ref, *, lam_init):
    tq = q_ref.shape[0]
    lp = lam_ref[...]
    lam = (jnp.exp(jnp.sum(lp[0:1, :] * lp[1:2, :], axis=1, keepdims=True))
           - jnp.exp(jnp.sum(lp[2:3, :] * lp[3:4, :], axis=1, keepdims=True)) + lam_init)
    for h in range(D_HEADS):
        cs = slice(h * V7X_LANES, (h + 1) * V7X_LANES)
        q2 = _stack_halves(q_ref[:, cs])
        o2, l = _attend([_dot_t(q2, kc_ref[:, cs]), _dot_t(q2, k_ref[:, cs])], [vc_ref[:, cs], v_ref[:, cs]])
        o2 = o2 / l
        o = o2[:tq] - lam * o2[tq:]
        o = o * lax.rsqrt(jnp.mean(o * o, axis=-1, keepdims=True) + NORM_EPS) * g_ref[...]
        o_ref[:, cs] = (o * (1.0 - lam_init)).astype(BF16)


def _diffattn(px, pc, lam_p, subln, lam_init, *, bsz, n_tok, ctx_len):
    nq = n_tok // DIFF_TQ
    body = functools.partial(_diffattn_body, lam_init=lam_init)
    return pl.pallas_call(
        body, out_shape=jax.ShapeDtypeStruct((bsz * n_tok, SEG), BF16), grid=(bsz, nq),
        in_specs=[pl.BlockSpec((DIFF_TQ, SEG), lambda b, j: (b * nq + j, 3)),
                  pl.BlockSpec((n_tok, SEG), lambda b, j: (b, 4)),
                  pl.BlockSpec((n_tok, SEG), lambda b, j: (b, 5)),
                  pl.BlockSpec((ctx_len, SEG), lambda b, j: (b, 0)),
                  pl.BlockSpec((ctx_len, SEG), lambda b, j: (b, 1)),
                  pl.BlockSpec((4, D_QK_DIM), lambda b, j: (0, 0)),
                  pl.BlockSpec((1, D_V_DIM), lambda b, j: (0, 0))],
        out_specs=pl.BlockSpec((DIFF_TQ, SEG), lambda b, j: (b * nq + j, 0)),
        compiler_params=_cparams("arbitrary", "arbitrary"), name="diff_attn",
    )(px, px, px, pc, pc, lam_p, subln.reshape(1, D_V_DIM))


def _oproj_body(a_ref, b_ref, x_ref, mod_ref, g_ref, wo_ref, wrh_ref, wrl_ref, br_ref, cin_ref,
                x1_ref, h_ref, bucket_ref, rank_ref, cout_ref, carry_ref, before_ref):
    i = pl.program_id(0)
    tm = x_ref.shape[0]

    @pl.when(i == 0)
    def _():
        carry_ref[...] = cin_ref[...]
        r_i = lax.broadcasted_iota(jnp.int32, (tm, tm), 0)
        c_i = lax.broadcasted_iota(jnp.int32, (tm, tm), 1)
        before_ref[...] = jnp.where(r_i > c_i, 1.0, 0.0).astype(BF16)

    def first_max(v, ln):
        m = jnp.max(v, axis=1, keepdims=True)
        return jnp.min(jnp.where(v == m, ln, jnp.int32(V7X_LANES)), axis=1, keepdims=True)

    def route(lg, ln):
        g_sel = first_max(jnp.where(ln < N_GROUPS, lg, -jnp.inf), ln)
        e_lane0 = ROUTER_EXPERT_LANE0 + EXPERTS_PER_GROUP * g_sel
        in_group = (ln >= e_lane0) & (ln < e_lane0 + EXPERTS_PER_GROUP)
        el = jnp.where(in_group, lg, -jnp.inf)
        i1 = first_max(el, ln)
        i2 = first_max(jnp.where(ln == i1, -jnp.inf, el), ln)
        lo = jnp.minimum(i1, i2) - e_lane0
        hi = jnp.maximum(i1, i2) - e_lane0
        return g_sel * PAIRS_PER_GROUP + ((lo * (2 * EXPERTS_PER_GROUP - 1 - lo)) >> 1) + (hi - lo - 1)

    rb = tm // ROUTE_BLOCKS
    lane_b = lax.broadcasted_iota(jnp.int32, (rb, V7X_LANES), 1)
    oh_blocks = []
    for r in range(ROUTE_BLOCKS):
        rows = pl.ds(r * rb, rb)
        y = _dot(a_ref[rows, :], wo_ref[0:SEG, :]) + _dot(b_ref[rows, :], wo_ref[SEG:2 * SEG, :])
        x1 = x_ref[rows, :] + mod_ref[0, 2:3, :] * y
        x1_ref[rows, :] = x1
        h = _rmsnorm_mod(x1, g_ref[...], mod_ref[0, 4:5, :], mod_ref[0, 3:4, :])
        _store_token_tiles(h_ref.at[pl.ds(r * rb * V7X_SUBLANES, rb * V7X_SUBLANES)], h)
        h_hi, h_lo = _split_bf16(h)
        logits = _dot3(h_hi, h_lo, wrh_ref[...], wrl_ref[...]) + br_ref[...]
        bucket = route(logits, lane_b)
        bucket_ref[rows, :] = bucket
        oh_blocks.append(jnp.where(lane_b == bucket, 1.0, 0.0))
    oh = jnp.concatenate(oh_blocks, axis=0)
    cum = _dot(before_ref[...], oh.astype(BF16)) + carry_ref[...]
    rank = jnp.sum(oh * cum, axis=1, keepdims=True)
    rank_ref[...] = rank.astype(jnp.int32)
    carry_ref[...] = carry_ref[...] + jnp.sum(oh, axis=0, keepdims=True)

    @pl.when(i == pl.num_programs(0) - 1)
    def _():
        cout_ref[...] = carry_ref[...]


def _oproj(a, b, x, mod, g, wo, wr_hi, wr_lo, br, counts_in, *, seq, mod_row, name="oproj"):
    t, d = x.shape
    tm = min(PROJ_TM, t if seq is None else seq)
    if seq is None:
        mod_map = lambda i: (mod_row, 0, 0)
    else:
        tiles_per_seq = seq // tm
        mod_map = lambda i: (i // tiles_per_seq, 0, 0)
    row = lambda i: (i, 0)
    const = lambda i: (0, 0)
    return pl.pallas_call(
        _oproj_body,
        out_shape=[jax.ShapeDtypeStruct((t, d), F32), jax.ShapeDtypeStruct((t * V7X_SUBLANES, V7X_LANES), F32),
                   jax.ShapeDtypeStruct((t, 1), jnp.int32), jax.ShapeDtypeStruct((t, 1), jnp.int32),
                   jax.ShapeDtypeStruct((1, V7X_LANES), F32)],
        grid=(t // tm,),
        in_specs=[pl.BlockSpec((tm, SEG), row), pl.BlockSpec((tm, SEG), row), pl.BlockSpec((tm, d), row),
                  pl.BlockSpec((1, N_MOD, d), mod_map), pl.BlockSpec((1, d), const),
                  pl.BlockSpec((2 * SEG, d), const), pl.BlockSpec((d, V7X_LANES), const),
                  pl.BlockSpec((d, V7X_LANES), const), pl.BlockSpec((1, V7X_LANES), const),
                  pl.BlockSpec((1, V7X_LANES), const)],
        out_specs=[pl.BlockSpec((tm, d), row), pl.BlockSpec((tm * V7X_SUBLANES, V7X_LANES), row),
                   pl.BlockSpec((tm, 1), row),
                   pl.BlockSpec((tm, 1), row), pl.BlockSpec((1, V7X_LANES), const)],
        scratch_shapes=[pltpu.VMEM((1, V7X_LANES), F32), pltpu.VMEM((tm, tm), BF16)],
        compiler_params=_cparams("arbitrary"), name=name,
    )(a, b, x, mod, g.reshape(1, d), wo, wr_hi, wr_lo, br, counts_in)


def _router_weights(wg, bg, we, be):
    d = wg.shape[0]
    w = jnp.zeros((d, V7X_LANES), F32)
    w = w.at[:, ROUTER_GROUP_LANE0:ROUTER_GROUP_LANE0 + N_GROUPS].set(wg)
    w = w.at[:, ROUTER_EXPERT_LANE0:ROUTER_EXPERT_LANE0 + N_EXPERTS].set(we)
    b = jnp.zeros((1, V7X_LANES), F32)
    b = b.at[0, ROUTER_GROUP_LANE0:ROUTER_GROUP_LANE0 + N_GROUPS].set(bg)
    b = b.at[0, ROUTER_EXPERT_LANE0:ROUTER_EXPERT_LANE0 + N_EXPERTS].set(be)
    w_hi = w.astype(BF16)
    w_lo = (w - w_hi.astype(F32)).astype(BF16)
    return w_hi, w_lo, b


def _token_tile(ref, t):
    return ref.at[pl.ds(pl.multiple_of(t * V7X_SUBLANES, V7X_SUBLANES), V7X_SUBLANES)]


def _scatter_body(off_ref, bucket_ref, rank_ref, rows_ref, sorted_hbm, out_hbm, sem, *, chunk):
    del sorted_hbm

    def start(t, carry):
        p = off_ref[bucket_ref[t]] + rank_ref[t]
        pltpu.make_async_copy(_token_tile(rows_ref, t), _token_tile(out_hbm, p), sem).start()
        return carry

    lax.fori_loop(0, chunk, start, 0, unroll=8)
    pltpu.make_async_copy(rows_ref, out_hbm.at[pl.ds(0, chunk * V7X_SUBLANES)], sem).wait()


def _gather_body(off_ref, bucket_ref, rank_ref, sorted_hbm, out_ref, sem, *, chunk):
    def start(t, carry):
        p = off_ref[bucket_ref[t]] + rank_ref[t]
        pltpu.make_async_copy(_token_tile(sorted_hbm, p), _token_tile(out_ref, t), sem).start()
        return carry

    lax.fori_loop(0, chunk, start, 0, unroll=8)
    pltpu.make_async_copy(sorted_hbm.at[pl.ds(0, chunk * V7X_SUBLANES)], out_ref, sem).wait()


def _permute(offsets, bucket, rank, rows, sorted_rows, *, n_rows, scatter, name):
    chunk = min(PERM_CHUNK, n_rows)
    smem_blk = pl.BlockSpec((chunk,), lambda i, *_: (i,), memory_space=pltpu.SMEM)
    any_spec = pl.BlockSpec(memory_space=pl.ANY)
    vmem_blk = pl.BlockSpec((chunk * V7X_SUBLANES, V7X_LANES), lambda i, *_: (i, 0))
    if scatter:
        body = functools.partial(_scatter_body, chunk=chunk)
        args = (offsets, bucket, rank, rows, sorted_rows)
        in_specs = [smem_blk, smem_blk, vmem_blk, any_spec]
        out_shape = jax.ShapeDtypeStruct(sorted_rows.shape, sorted_rows.dtype)
        out_specs = any_spec
        aliases = {4: 0}
    else:
        body = functools.partial(_gather_body, chunk=chunk)
        args = (offsets, bucket, rank, sorted_rows)
        in_specs = [smem_blk, smem_blk, any_spec]
        out_shape = jax.ShapeDtypeStruct((n_rows * V7X_SUBLANES, V7X_LANES), sorted_rows.dtype)
        out_specs = vmem_blk
        aliases = {}
    grid_spec = pltpu.PrefetchScalarGridSpec(
        num_scalar_prefetch=1, grid=(n_rows // chunk,), in_specs=in_specs, out_specs=out_specs,
        scratch_shapes=[pltpu.SemaphoreType.DMA])
    return pl.pallas_call(
        body, out_shape=out_shape, grid_spec=grid_spec, input_output_aliases=aliases,
        compiler_params=pltpu.CompilerParams(dimension_semantics=("arbitrary",), has_side_effects=True,
                                             vmem_limit_bytes=V7X_VMEM_LIMIT_BYTES),
        name=name,
    )(*args)


def _moe_body(te1_ref, te2_ref, tval_ref, trow_ref, x_ref, w1a_ref, w1b_ref, w3a_ref, w3b_ref, w2a_ref, w2b_ref,
              wr_ref, br_ref, y_ref):
    i = pl.program_id(0)

    @pl.when(tval_ref[i] == 0)
    def _():
        y_ref[...] = jnp.zeros_like(y_ref)

    @pl.when(tval_ref[i] > 0)
    def _():
        tm = x_ref.shape[0] // V7X_SUBLANES
        x = _load_token_tiles(x_ref).astype(BF16)
        e1 = te1_ref[i]
        e2 = te2_ref[i]
        grp = e1 >> 3
        logits = _dot(x, wr_ref[...]) + br_ref[...]
        lane = lax.broadcasted_iota(jnp.int32, (tm, V7X_LANES), 1)
        pick = lambda idx: jnp.sum(jnp.where(lane == idx, logits, 0.0), axis=1, keepdims=True)
        gl = jnp.where(lane < N_GROUPS, logits, -jnp.inf)
        gm = jnp.max(gl, axis=1, keepdims=True)
        gz = jnp.sum(jnp.exp(gl - gm), axis=1, keepdims=True)
        g_w = jnp.exp(pick(ROUTER_GROUP_LANE0 + grp) - gm) / gz
        l1 = pick(ROUTER_EXPERT_LANE0 + e1)
        l2 = pick(ROUTER_EXPERT_LANE0 + e2)
        m = jnp.maximum(l1, l2)
        p1 = jnp.exp(l1 - m)
        p2 = jnp.exp(l2 - m)
        c1 = g_w * p1 / (p1 + p2)
        c2 = g_w * p2 / (p1 + p2)

        def expert(w1_ref, w3_ref, w2_ref):
            h = _dot(x, w1_ref[0])
            hid = (h * _sigmoid(h)) * _dot(x, w3_ref[0])
            return _dot(hid.astype(BF16), w2_ref[0])

        _store_token_tiles(y_ref, c1 * expert(w1a_ref, w3a_ref, w2a_ref) + c2 * expert(w1b_ref, w3b_ref, w2b_ref))


def _moe_sorted(xs, te1, te2, tval, trow, w1, w3, w2, wr_hi, br):
    d = w1.shape[1]
    n_tiles = xs.shape[0] // (MOE_TM * V7X_SUBLANES)
    tile_blk = (MOE_TM * V7X_SUBLANES, V7X_LANES)
    xmap = lambda i, te1, te2, tval, trow: (trow[i], 0)
    wa = lambda i, te1, te2, tval, trow: (te1[i], 0, 0)
    wb = lambda i, te1, te2, tval, trow: (te2[i], 0, 0)
    const = lambda i, *_: (0, 0)
    grid_spec = pltpu.PrefetchScalarGridSpec(
        num_scalar_prefetch=4, grid=(n_tiles,),
        in_specs=[pl.BlockSpec(tile_blk, xmap),
                  pl.BlockSpec((1, d, D_EXPERT), wa), pl.BlockSpec((1, d, D_EXPERT), wb),
                  pl.BlockSpec((1, d, D_EXPERT), wa), pl.BlockSpec((1, d, D_EXPERT), wb),
                  pl.BlockSpec((1, D_EXPERT, d), wa), pl.BlockSpec((1, D_EXPERT, d), wb),
                  pl.BlockSpec((d, V7X_LANES), const), pl.BlockSpec((1, V7X_LANES), const)],
        out_specs=pl.BlockSpec(tile_blk, lambda i, *_: (i, 0)))
    return pl.pallas_call(
        _moe_body, out_shape=jax.ShapeDtypeStruct(xs.shape, F32), grid_spec=grid_spec,
        compiler_params=_cparams("arbitrary"), name="moe_experts",
    )(te1, te2, tval, trow, xs, w1, w1, w3, w3, w2, w2, wr_hi, br)


def _moe(h_parts, route_parts, counts, w1, w3, w2, wr_hi, br):
    n_rows = [h.shape[0] // V7X_SUBLANES for h in h_parts]
    n_tiles = sum(n_rows) // MOE_TM + N_BUCKETS
    cnt = counts[0, :N_BUCKETS].astype(jnp.int32)
    tiles_b = (cnt + MOE_TM - 1) // MOE_TM
    tile_end = jnp.cumsum(tiles_b)
    n_valid = tile_end[-1]
    offsets = jnp.zeros((V7X_LANES,), jnp.int32).at[:N_BUCKETS].set((tile_end - tiles_b) * MOE_TM)
    tile_id = jnp.arange(n_tiles, dtype=jnp.int32)
    tval = (tile_id < n_valid).astype(jnp.int32)
    trow = jnp.minimum(tile_id, n_valid - 1)
    tbucket = jnp.sum((trow[:, None] >= tile_end[None, :]).astype(jnp.int32), axis=1)
    tbucket = jnp.minimum(tbucket, N_BUCKETS - 1)
    te1 = jnp.asarray(_BUCKET_E1)[tbucket]
    te2 = jnp.asarray(_BUCKET_E2)[tbucket]

    xs = jnp.zeros((n_tiles * MOE_TM * V7X_SUBLANES, V7X_LANES), F32)
    for h, n, (bucket, rank) in zip(h_parts, n_rows, route_parts):
        xs = _permute(offsets, bucket, rank, h, xs, n_rows=n, scatter=True, name="moe_scatter_rows")
    ys = _moe_sorted(xs, te1, te2, tval, trow, w1, w3, w2, wr_hi, br)
    return [_permute(offsets, bucket, rank, None, ys, n_rows=n, scatter=False, name="moe_gather_rows")
            for n, (bucket, rank) in zip(n_rows, route_parts)]


def _final_body(x_ref, y_ref, mod_ref, g_ref, o_ref):
    x = x_ref[...] + mod_ref[0, 5:6, :] * _load_token_tiles(y_ref)
    o_ref[...] = x * lax.rsqrt(jnp.mean(x * x, axis=-1, keepdims=True) + NORM_EPS) * g_ref[...]


def _final(x, y, mod, g, *, seq):
    t, d = x.shape
    tm = min(PROJ_TM, seq)
    tiles_per_seq = seq // tm
    row = lambda i: (i, 0)
    return pl.pallas_call(
        _final_body, out_shape=jax.ShapeDtypeStruct((t, d), F32), grid=(t // tm,),
        in_specs=[pl.BlockSpec((tm, d), row), pl.BlockSpec((tm * V7X_SUBLANES, V7X_LANES), row),
                  pl.BlockSpec((1, N_MOD, d), lambda i: (i // tiles_per_seq, 0, 0)),
                  pl.BlockSpec((1, d), lambda i: (0, 0))],
        out_specs=pl.BlockSpec((tm, d), row),
        compiler_params=_cparams("arbitrary"), name="final_norm",
    )(x, y, mod, g.reshape(1, d))


def kernel(x, c, ctx, c_ctx, w_mod, b_mod, norm_mix, norm_ffn, norm_final, even_w_in, even_w_out, even_rpb,
           even_dw, even_dw_b, even_ln_g, even_ln_b, odd_w_in, odd_w_out, odd_conv, odd_lambda, odd_subln,
           moe_wg, moe_bg, moe_we, moe_be, moe_w1, moe_w3, moe_w2):
    bsz, n_tok, d = x.shape
    ctx_len = ctx.shape[1]
    depth = w_mod.shape[0]
    assert depth == 2 and d == D_MODEL
    xs = x.reshape(bsz * n_tok, d)
    cs = ctx.reshape(bsz * ctx_len, d)

    mod_rows = -(-(bsz + 1) // 8) * 8
    cc = jnp.zeros((mod_rows, d), F32).at[:bsz].set(c).at[bsz].set(c_ctx)
    mod = _modulation(cc, w_mod, b_mod).reshape(depth, mod_rows, N_MOD, d)
    bf = lambda a: a.astype(BF16)
    ew = _expert_weights_bf16
    zero_counts = jnp.zeros((1, V7X_LANES), F32)

    l = 0
    w_in = bf(even_w_in[0])
    w_out = bf(even_w_out[0])
    px = _proj(xs, mod[l], norm_mix[l], w_in, seq=n_tok, mod_row=None, q_seg=0, name="proj_even_x")
    pc = _proj(cs, mod[l], norm_mix[l], w_in, seq=None, mod_row=bsz, q_seg=0, name="proj_even_c")
    a_x = _natten(px, pc, even_rpb[0], bsz=bsz, n_tok=n_tok, ctx_len=ctx_len)
    a_c = _ctxattn(pc, bsz=bsz, ctx_len=ctx_len)
    b_x = _conformer(px, even_dw[0], even_dw_b[0], even_ln_g[0], even_ln_b[0], bsz=bsz, seq=n_tok)
    b_c = _conformer(pc, even_dw[0], even_dw_b[0], even_ln_g[0], even_ln_b[0], bsz=bsz, seq=ctx_len)
    wr_hi, wr_lo, br = _router_weights(moe_wg[l], moe_bg[l], moe_we[l], moe_be[l])
    x1, hx, bkt_x, rnk_x, counts = _oproj(a_x, b_x, xs, mod[l], norm_ffn[l], w_out, wr_hi, wr_lo, br, zero_counts,
                                          seq=n_tok, mod_row=None, name="oproj_even_x")
    c1, hc, bkt_c, rnk_c, counts = _oproj(a_c, b_c, cs, mod[l], norm_ffn[l], w_out, wr_hi, wr_lo, br, counts,
                                          seq=None, mod_row=bsz, name="oproj_even_c")
    y_x, y_c = _moe([hx, hc], [(bkt_x.reshape(-1), rnk_x.reshape(-1)), (bkt_c.reshape(-1), rnk_c.reshape(-1))],
                    counts, ew(moe_w1, l), ew(moe_w3, l), ew(moe_w2, l), wr_hi, br)

    l = 1
    lam_init = 0.8 - 0.6 * math.exp(-0.3 * l)
    w_in = bf(odd_w_in[0])
    w_out = bf(odd_w_out[0])
    rope = _rope_tables(n_tok)
    px, x2 = _proj(x1, mod[l], norm_mix[l], w_in, seq=n_tok, mod_row=None, y=y_x, res_mod=mod[l - 1], res_idx=5,
                   rope=rope, rope_segs=(3, 4), q_seg=3, write_x=True, name="proj_odd_x")
    pc = _proj(c1, mod[l], norm_mix[l], w_in[:, 4 * SEG:], seq=None, mod_row=bsz, y=y_c, res_mod=mod[l - 1],
               res_idx=5, name="proj_odd_c")
    s_x = _shortconv(px, odd_conv[0], bsz=bsz, seq=n_tok)
    d_x = _diffattn(px, pc, odd_lambda[0], odd_subln[0], lam_init, bsz=bsz, n_tok=n_tok, ctx_len=ctx_len)
    wr_hi, wr_lo, br = _router_weights(moe_wg[l], moe_bg[l], moe_we[l], moe_be[l])
    x3, hx, bkt_x, rnk_x, counts = _oproj(s_x, d_x, x2, mod[l], norm_ffn[l], w_out, wr_hi, wr_lo, br, zero_counts,
                                          seq=n_tok, mod_row=None, name="oproj_odd_x")
    (y_x,) = _moe([hx], [(bkt_x.reshape(-1), rnk_x.reshape(-1))], counts,
                  ew(moe_w1, l), ew(moe_w3, l), ew(moe_w2, l), wr_hi, br)
    out = _final(x3, y_x, mod[l], norm_final, seq=n_tok)
    return out.reshape(bsz, n_tok, d)
```

```python
import functools
import math

import numpy as np
import jax
import jax.numpy as jnp
from jax import lax
from jax.experimental import pallas as pl
from jax.experimental.pallas import tpu as pltpu

F32 = jnp.float32
BF16 = jnp.bfloat16

D_MODEL = 1024
GRID_W = 64
N_MOD = 6
NORM_EPS = 1e-6
NEG_INF = -1e30
ROPE_BASE = 10000.0
SEG = 512
A_HEAD_DIM = 64
A_HEADS = 8
WIN_ROWS_MAX = 8
WIN_COLS = 16
B_CONV = 31
C_CONV = 3
D_QK_DIM = 64
D_V_DIM = 128
D_HEADS = 4
N_GROUPS = 4
EXPERTS_PER_GROUP = 8
N_EXPERTS = 32
D_EXPERT = 256
LOG2E = math.log2(math.e)
QUERY_SCALE = A_HEAD_DIM ** -0.5 * LOG2E
PAIRS_PER_GROUP = EXPERTS_PER_GROUP * (EXPERTS_PER_GROUP - 1) // 2
N_BUCKETS = N_GROUPS * PAIRS_PER_GROUP

V7X_LANES = 128
V7X_SUBLANES = 8
V7X_VMEM_LIMIT_BYTES = 56 * 1024 * 1024

PROJ_TM = 512
ATT_TQ = 256
NAT_TILES_PER_STEP = 4
DIFF_TQ = 1024
NAT_KROWS = 12
CONV_CHUNK = 64
CONV_PAD = 16
ROUTE_BLOCKS = 1
MOE_TM = 256
PERM_CHUNK = 1024
ROUTER_GROUP_LANE0 = 0
ROUTER_EXPERT_LANE0 = 8

_PAIR_LO = np.array([i for i in range(8) for j in range(i + 1, 8)], np.int32)
_PAIR_HI = np.array([j for i in range(8) for j in range(i + 1, 8)], np.int32)
_BUCKET_E1 = np.concatenate([g * 8 + _PAIR_LO for g in range(N_GROUPS)]).astype(np.int32)
_BUCKET_E2 = np.concatenate([g * 8 + _PAIR_HI for g in range(N_GROUPS)]).astype(np.int32)


def _cparams(*sem):
    return pltpu.CompilerParams(dimension_semantics=tuple(sem), vmem_limit_bytes=V7X_VMEM_LIMIT_BYTES)


def _dot(a, b):
    return jnp.dot(a, b, preferred_element_type=F32)


def _dot_t(a, b):
    return lax.dot_general(a, b, (((1,), (1,)), ((), ())), preferred_element_type=F32)


def _split_bf16(a):
    hi = a.astype(BF16)
    lo = (a - hi.astype(F32)).astype(BF16)
    return hi, lo


def _dot3(a_hi, a_lo, b_hi, b_lo):
    return _dot(a_hi, b_hi) + _dot(a_lo, b_hi) + _dot(a_hi, b_lo)


def _sigmoid(x):
    return 1.0 / (1.0 + jnp.exp(-x))


def _load_token_tiles(ref):
    tm = ref.shape[0] // V7X_SUBLANES
    return jnp.concatenate([ref[pl.ds(c, tm, stride=V7X_SUBLANES), :] for c in range(V7X_SUBLANES)], axis=1)


def _store_token_tiles(ref, v):
    tm = v.shape[0]
    for c in range(V7X_SUBLANES):
        ref[pl.ds(c, tm, stride=V7X_SUBLANES), :] = v[:, c * V7X_LANES:(c + 1) * V7X_LANES]


def _rmsnorm_mod(x, g, scale, shift):
    y = x * lax.rsqrt(jnp.mean(x * x, axis=-1, keepdims=True) + NORM_EPS)
    return (y * g) * (1.0 + scale) + shift


def _cast_body(w_ref, o_ref):
    o_ref[...] = w_ref[0].astype(BF16)


def _expert_weights_bf16(w, layer):
    _, n_exp, k, n = w.shape
    eb = 4
    return pl.pallas_call(
        _cast_body, out_shape=jax.ShapeDtypeStruct((n_exp, k, n), BF16), grid=(n_exp // eb,),
        in_specs=[pl.BlockSpec((1, eb, k, n), lambda e: (layer, e, 0, 0))],
        out_specs=pl.BlockSpec((eb, k, n), lambda e: (e, 0, 0)),
        compiler_params=_cparams("arbitrary"), name="expert_weights_bf16",
    )(w)


def _mod_body(c_ref, w_ref, b_ref, o_ref):
    c = c_ref[...]
    s = c * _sigmoid(c)
    s_hi, s_lo = _split_bf16(s)
    w_hi, w_lo = _split_bf16(w_ref[0])
    o_ref[0] = _dot3(s_hi, s_lo, w_hi, w_lo) + b_ref[0]


def _modulation(cc, w_mod, b_mod):
    depth, d, n = w_mod.shape
    rows = cc.shape[0]
    tn = 1536
    return pl.pallas_call(
        _mod_body,
        out_shape=jax.ShapeDtypeStruct((depth, rows, n), F32),
        grid=(depth, n // tn),
        in_specs=[pl.BlockSpec((rows, d), lambda l, j: (0, 0)),
                  pl.BlockSpec((1, d, tn), lambda l, j: (l, 0, j)),
                  pl.BlockSpec((1, 1, tn), lambda l, j: (l, 0, j))],
        out_specs=pl.BlockSpec((1, rows, tn), lambda l, j: (l, 0, j)),
        compiler_params=_cparams("arbitrary", "arbitrary"),
        name="modulation",
    )(cc, w_mod, b_mod.reshape(depth, 1, n))


def _rope(v, cos, sin, lane):
    up = pltpu.roll(v, V7X_LANES - 16, axis=1)
    dn = pltpu.roll(v, 16, axis=1)
    sw = jnp.where((lane & 31) < 16, up, dn)
    return v * cos + sw * sin


def _proj_body(*refs, n_seg, res_idx, shift_idx, rope_segs, q_seg, write_x):
    refs = list(refs)
    x_ref = refs.pop(0)
    y_ref = refs.pop(0) if res_idx is not None else None
    rmod_ref = refs.pop(0) if res_idx is not None else None
    mod_ref = refs.pop(0)
    g_ref = refs.pop(0)
    w_ref = refs.pop(0)
    cos_ref = sin_ref = None
    if rope_segs:
        cos_ref = refs.pop(0)
        sin_ref = refs.pop(0)
    out_ref = refs.pop(0)
    xo_ref = refs.pop(0) if write_x else None

    x = x_ref[...]
    if y_ref is not None:
        x = x + rmod_ref[0, res_idx:res_idx + 1, :] * _load_token_tiles(y_ref)
        if write_x:
            xo_ref[...] = x
    h = _rmsnorm_mod(x, g_ref[...], mod_ref[0, shift_idx + 1:shift_idx + 2, :],
                     mod_ref[0, shift_idx:shift_idx + 1, :]).astype(BF16)
    for s in range(n_seg):
        o = _dot(h, w_ref[:, s * SEG:(s + 1) * SEG])
        if s == q_seg:
            o = o * QUERY_SCALE
        if s in rope_segs:
            lane = lax.broadcasted_iota(jnp.int32, (o.shape[0], V7X_LANES), 1)
            cos = cos_ref[...]
            sin = sin_ref[...]
            o = jnp.concatenate(
                [_rope(o[:, c * V7X_LANES:(c + 1) * V7X_LANES], cos, sin, lane) for c in range(SEG // V7X_LANES)],
                axis=1)
        out_ref[:, s * SEG:(s + 1) * SEG] = o.astype(BF16)


def _proj(x, mod, g, w, *, seq, mod_row, y=None, res_mod=None, res_idx=None, shift_idx=0, rope=None, rope_segs=(), q_seg=None,
          write_x=False, name="proj"):
    t, d = x.shape
    n = w.shape[1]
    tm = min(PROJ_TM, t if seq is None else seq)
    tiles_per_seq = None if seq is None else seq // tm
    if seq is None:
        mod_map = lambda i: (mod_row, 0, 0)
    else:
        mod_map = lambda i: (i // tiles_per_seq, 0, 0)
    args = [x]
    specs = [pl.BlockSpec((tm, d), lambda i: (i, 0))]
    if y is not None:
        args += [y, res_mod]
        specs += [pl.BlockSpec((tm * V7X_SUBLANES, V7X_LANES), lambda i: (i, 0)),
                  pl.BlockSpec((1, N_MOD, d), mod_map)]
    args += [mod, g.reshape(1, d), w]
    specs += [pl.BlockSpec((1, N_MOD, d), mod_map), pl.BlockSpec((1, d), lambda i: (0, 0)),
              pl.BlockSpec((d, n), lambda i: (0, 0))]
    if rope_segs:
        args += [rope[0], rope[1]]
        specs += [pl.BlockSpec((tm, V7X_LANES), lambda i: (i % tiles_per_seq, 0))] * 2
    out_shape = [jax.ShapeDtypeStruct((t, n), BF16)]
    out_specs = [pl.BlockSpec((tm, n), lambda i: (i, 0))]
    if write_x:
        out_shape.append(jax.ShapeDtypeStruct((t, d), F32))
        out_specs.append(pl.BlockSpec((tm, d), lambda i: (i, 0)))
    body = functools.partial(_proj_body, n_seg=n // SEG, res_idx=res_idx if y is not None else None,
                             shift_idx=shift_idx, rope_segs=tuple(rope_segs), q_seg=q_seg, write_x=write_x)
    outs = pl.pallas_call(
        body, out_shape=out_shape, grid=(t // tm,), in_specs=specs, out_specs=out_specs,
        compiler_params=_cparams("arbitrary"), name=name,
    )(*args)
    return outs if write_x else outs[0]


def _rope_tables(n_tok):
    quarter = D_QK_DIM // 4
    t = np.arange(n_tok)
    row = (t // GRID_W).astype(np.float32)
    col = (t % GRID_W).astype(np.float32)
    inv = jnp.power(ROPE_BASE, -jnp.arange(quarter, dtype=F32) / quarter)
    ar = jnp.asarray(row)[:, None] * inv
    ac = jnp.asarray(col)[:, None] * inv
    cos64 = jnp.concatenate([jnp.cos(ar), jnp.cos(ar), jnp.cos(ac), jnp.cos(ac)], axis=-1)
    sin64 = jnp.concatenate([-jnp.sin(ar), jnp.sin(ar), -jnp.sin(ac), jnp.sin(ac)], axis=-1)
    return jnp.tile(cos64, (1, V7X_LANES // D_QK_DIM)), jnp.tile(sin64, (1, V7X_LANES // D_QK_DIM))


def _stack_halves(q):
    lane = lax.broadcasted_iota(jnp.int32, q.shape, 1)
    zero = jnp.zeros_like(q)
    return jnp.concatenate([jnp.where(lane < 64, q, zero), jnp.where(lane >= 64, q, zero)], axis=0)


def _attend(scores, values):
    m = functools.reduce(jnp.maximum, [jnp.max(s, axis=1, keepdims=True) for s in scores])
    acc = None
    for s, v in zip(scores, values):
        p = jnp.exp2(s - m).astype(BF16)
        lane = lax.broadcasted_iota(jnp.int32, v.shape, 1)
        ones_col = jnp.where(lane == 0, 1.0, 0.0).astype(BF16)
        o = _dot(p, jnp.concatenate([v, ones_col], axis=1))
        acc = o if acc is None else acc + o
    return acc[:, :V7X_LANES], acc[:, V7X_LANES:V7X_LANES + 1]


def _natten_body(ks_ref, pid_ref, q_ref, k_ref, v_ref, kc_ref, vc_ref, bias_ref, o_ref):
    nk = NAT_KROWS * GRID_W
    tq = ATT_TQ
    lane = lax.broadcasted_iota(jnp.int32, (tq, V7X_LANES), 1)
    for s in range(q_ref.shape[0] // tq):
        j = pl.program_id(1) * (q_ref.shape[0] // tq) + s
        k0 = pl.multiple_of(ks_ref[j] * GRID_W, GRID_W)
        p = pid_ref[j]
        rows = pl.ds(s * tq, tq)
        for hp in range(A_HEADS // 2):
            cs = slice(hp * V7X_LANES, (hp + 1) * V7X_LANES)
            q2 = _stack_halves(q_ref[rows, cs])
            s_c = _dot_t(q2, kc_ref[:, cs])
            s_l = _dot_t(q2, k_ref[pl.ds(k0, nk), cs])
            bias = jnp.concatenate([bias_ref[p, 2 * hp], bias_ref[p, 2 * hp + 1]], axis=0).astype(F32)
            o2, l = _attend([s_c, s_l + bias], [vc_ref[:, cs], v_ref[pl.ds(k0, nk), cs]])
            o2 = o2 / l
            o_ref[rows, cs] = jnp.where(lane < 64, o2[:tq], o2[tq:]).astype(BF16)


def _natten_tables(rows):
    q_rows = ATT_TQ // GRID_W
    nqb = rows // q_rows
    win_r = min(WIN_ROWS_MAX, rows)
    col = np.arange(GRID_W)
    c_start = np.clip(col - WIN_COLS // 2, 0, GRID_W - WIN_COLS)
    col_ok = (col[None, :] >= c_start[:, None]) & (col[None, :] < c_start[:, None] + WIN_COLS)
    dc = np.clip(col[None, :] - col[:, None], -(WIN_COLS - 1), WIN_COLS - 1) + WIN_COLS - 1
    ks_list, pats, pid = [], [], []
    for j in range(nqb):
        r0 = j * q_rows
        ks = int(np.clip(r0 - win_r // 2, 0, rows - NAT_KROWS))
        qr = r0 + np.arange(q_rows)
        rs = np.clip(qr - win_r // 2, 0, rows - win_r)
        kr = ks + np.arange(NAT_KROWS)
        row_ok = (kr[None, :] >= rs[:, None]) & (kr[None, :] < rs[:, None] + win_r)
        dr = np.clip(kr[None, :] - qr[:, None] + WIN_ROWS_MAX - 1, 0, 2 * WIN_ROWS_MAX - 2)
        key = (row_ok.tobytes(), (dr * row_ok).tobytes())
        for n, (k_, *_rest) in enumerate(pats):
            if k_ == key:
                pid.append(n)
                break
        else:
            pid.append(len(pats))
            pats.append((key, row_ok, dr))
        ks_list.append(ks)
    row_ok = np.stack([p_[1] for p_ in pats])
    dr = np.stack([p_[2] for p_ in pats])
    return np.array(ks_list, np.int32), np.array(pid, np.int32), row_ok, dr, col_ok, dc


def _natten_bias(rpb, row_ok, dr, col_ok, dc):
    n_pat, q_rows, k_rows = dr.shape
    n_dr, n_dc = rpb.shape[1], rpb.shape[2]
    oh_r = np.zeros((n_pat * q_rows * k_rows, n_dr), np.float32)
    oh_r[np.arange(oh_r.shape[0]), dr.reshape(-1)] = 1.0
    oh_c = np.zeros((n_dc, GRID_W * GRID_W), np.float32)
    oh_c[dc.reshape(-1), np.arange(GRID_W * GRID_W)] = 1.0
    hi = lax.Precision.HIGHEST
    t1 = jnp.einsum("mr,hrc->hmc", jnp.asarray(oh_r), rpb, precision=hi)
    t2 = jnp.einsum("hmc,cn->hmn", t1, jnp.asarray(oh_c), precision=hi)
    t2 = t2.reshape(rpb.shape[0], n_pat, q_rows, k_rows, GRID_W, GRID_W)
    valid = row_ok[None, :, :, :, None, None] & col_ok[None, None, None, None, :, :]
    bias = jnp.where(valid, t2 * LOG2E, NEG_INF).transpose(1, 0, 2, 4, 3, 5)
    return bias.reshape(n_pat, rpb.shape[0], q_rows * GRID_W, k_rows * GRID_W).astype(BF16)


def _natten(px, pc, rpb, *, bsz, n_tok, ctx_len):
    rows = n_tok // GRID_W
    assert rows >= NAT_KROWS and n_tok % ATT_TQ == 0
    ks, pid, row_ok, dr, col_ok, dc = _natten_tables(rows)
    bias = _natten_bias(rpb, row_ok, dr, col_ok, dc)
    n_pat = bias.shape[0]
    q_blk = ATT_TQ * NAT_TILES_PER_STEP
    assert n_tok % q_blk == 0
    nqb = n_tok // q_blk
    nk = NAT_KROWS * GRID_W
    grid_spec = pltpu.PrefetchScalarGridSpec(
        num_scalar_prefetch=2, grid=(bsz, nqb),
        in_specs=[pl.BlockSpec((q_blk, SEG), lambda b, j, *_: (b * nqb + j, 0)),
                  pl.BlockSpec((n_tok, SEG), lambda b, j, *_: (b, 1)),
                  pl.BlockSpec((n_tok, SEG), lambda b, j, *_: (b, 2)),
                  pl.BlockSpec((ctx_len, SEG), lambda b, j, *_: (b, 1)),
                  pl.BlockSpec((ctx_len, SEG), lambda b, j, *_: (b, 2)),
                  pl.BlockSpec((n_pat, A_HEADS, ATT_TQ, nk), lambda b, j, *_: (0, 0, 0, 0))],
        out_specs=pl.BlockSpec((q_blk, SEG), lambda b, j, *_: (b * nqb + j, 0)))
    return pl.pallas_call(
        _natten_body, out_shape=jax.ShapeDtypeStruct((bsz * n_tok, SEG), BF16), grid_spec=grid_spec,
        compiler_params=_cparams("arbitrary", "arbitrary"), name="natten",
    )(jnp.asarray(ks), jnp.asarray(pid), px, px, px, pc, pc, bias)


def _ctxattn_body(q_ref, k_ref, v_ref, o_ref):
    tq = q_ref.shape[0]
    lane = lax.broadcasted_iota(jnp.int32, (tq, V7X_LANES), 1)
    for hp in range(A_HEADS // 2):
        cs = slice(hp * V7X_LANES, (hp + 1) * V7X_LANES)
        q2 = _stack_halves(q_ref[:, cs])
        o2, l = _attend([_dot_t(q2, k_ref[:, cs])], [v_ref[:, cs]])
        o2 = o2 / l
        o_ref[:, cs] = jnp.where(lane < 64, o2[:tq], o2[tq:]).astype(BF16)


def _ctxattn(pc, *, bsz, ctx_len):
    return pl.pallas_call(
        _ctxattn_body, out_shape=jax.ShapeDtypeStruct((bsz * ctx_len, SEG), BF16), grid=(bsz,),
        in_specs=[pl.BlockSpec((ctx_len, SEG), lambda b: (b, 0)),
                  pl.BlockSpec((ctx_len, SEG), lambda b: (b, 1)),
                  pl.BlockSpec((ctx_len, SEG), lambda b: (b, 2))],
        out_specs=pl.BlockSpec((ctx_len, SEG), lambda b: (b, 0)),
        compiler_params=_cparams("arbitrary"), name="ctxattn",
    )(pc, pc, pc)


def _conv_accumulate(z_ref, w_ref, t0, taps):
    base = CONV_PAD - taps // 2
    blocks = []
    for c in range(SEG // V7X_LANES):
        cs = slice(c * V7X_LANES, (c + 1) * V7X_LANES)
        win = z_ref[pl.ds(t0, CONV_CHUNK + 2 * CONV_PAD), cs]
        acc = None
        for b in range(V7X_SUBLANES):
            ks = [k for k in range(taps) if (base + k) % V7X_SUBLANES == b]
            if not ks:
                continue
            shifted = win if b == 0 else pltpu.roll(win, win.shape[0] - b, axis=0)
            for k in ks:
                a0 = (base + k) // V7X_SUBLANES * V7X_SUBLANES
                term = shifted[a0:a0 + CONV_CHUNK, :] * w_ref[k:k + 1, cs]
                acc = term if acc is None else acc + term
        blocks.append(acc)
    return jnp.concatenate(blocks, axis=1)


def _conformer_body(u_ref, g_ref, w_ref, b_ref, lg_ref, lb_ref, o_ref, z_ref):
    n = u_ref.shape[0]
    halo = jnp.zeros((CONV_PAD, SEG), F32)
    z_ref[pl.ds(0, CONV_PAD), :] = halo
    z_ref[pl.ds(CONV_PAD + n, CONV_PAD), :] = halo
    z_ref[pl.ds(CONV_PAD, n), :] = u_ref[...].astype(F32) * _sigmoid(g_ref[...].astype(F32))

    def chunk(i, carry):
        t0 = pl.multiple_of(i * CONV_CHUNK, CONV_CHUNK)
        y = _conv_accumulate(z_ref, w_ref, t0, B_CONV) + b_ref[...]
        mu = jnp.mean(y, axis=-1, keepdims=True)
        yc = y - mu
        var = jnp.mean(yc * yc, axis=-1, keepdims=True)
        y = yc * lax.rsqrt(var + NORM_EPS) * lg_ref[...] + lb_ref[...]
        o_ref[pl.ds(t0, CONV_CHUNK), :] = (y * _sigmoid(y)).astype(BF16)
        return carry

    lax.fori_loop(0, n // CONV_CHUNK, chunk, 0)


def _conformer(p, dw, dw_b, ln_g, ln_b, *, bsz, seq):
    vec = lambda a: a.reshape(1, SEG)
    return pl.pallas_call(
        _conformer_body, out_shape=jax.ShapeDtypeStruct((bsz * seq, SEG), BF16), grid=(bsz,),
        in_specs=[pl.BlockSpec((seq, SEG), lambda b: (b, 3)),
                  pl.BlockSpec((seq, SEG), lambda b: (b, 4)),
                  pl.BlockSpec((B_CONV, SEG), lambda b: (0, 0))] + [pl.BlockSpec((1, SEG), lambda b: (0, 0))] * 3,
        out_specs=pl.BlockSpec((seq, SEG), lambda b: (b, 0)),
        scratch_shapes=[pltpu.VMEM((seq + 2 * CONV_PAD, SEG), F32)],
        compiler_params=_cparams("arbitrary"), name="conformer_conv",
    )(p, p, dw, vec(dw_b), vec(ln_g), vec(ln_b))


def _shortconv_body(bg_ref, cg_ref, u_ref, w_ref, o_ref, z_ref):
    n = u_ref.shape[0]
    halo = jnp.zeros((CONV_PAD, SEG), F32)
    z_ref[pl.ds(0, CONV_PAD), :] = halo
    z_ref[pl.ds(CONV_PAD + n, CONV_PAD), :] = halo
    z_ref[pl.ds(CONV_PAD, n), :] = cg_ref[...].astype(F32) * u_ref[...].astype(F32)

    def chunk(i, carry):
        t0 = pl.multiple_of(i * CONV_CHUNK, CONV_CHUNK)
        y = _conv_accumulate(z_ref, w_ref, t0, C_CONV)
        o_ref[pl.ds(t0, CONV_CHUNK), :] = (bg_ref[pl.ds(t0, CONV_CHUNK), :].astype(F32) * y).astype(BF16)
        return carry

    lax.fori_loop(0, n // CONV_CHUNK, chunk, 0)


def _shortconv(p, conv_w, *, bsz, seq):
    return pl.pallas_call(
        _shortconv_body, out_shape=jax.ShapeDtypeStruct((bsz * seq, SEG), BF16), grid=(bsz,),
        in_specs=[pl.BlockSpec((seq, SEG), lambda b: (b, 0)),
                  pl.BlockSpec((seq, SEG), lambda b: (b, 1)),
                  pl.BlockSpec((seq, SEG), lambda b: (b, 2)),
                  pl.BlockSpec((C_CONV, SEG), lambda b: (0, 0))],
        out_specs=pl.BlockSpec((seq, SEG), lambda b: (b, 0)),
        scratch_shapes=[pltpu.VMEM((seq + 2 * CONV_PAD, SEG), F32)],
        compiler_params=_cparams("arbitrary"), name="short_conv",
    )(p, p, p, conv_w)


def _diffattn_body(q_ref, k_ref, v_ref, kc_ref, vc_ref, lam_ref, g_ref, o_ref, *, lam_init):
    tq = ATT_TQ
    lp = lam_ref[...]
    lam = (jnp.exp(jnp.sum(lp[0:1, :] * lp[1:2, :], axis=1, keepdims=True))
           - jnp.exp(jnp.sum(lp[2:3, :] * lp[3:4, :], axis=1, keepdims=True)) + lam_init)
    for s in range(q_ref.shape[0] // tq):
        rows = pl.ds(s * tq, tq)
        for h in range(D_HEADS):
            cs = slice(h * V7X_LANES, (h + 1) * V7X_LANES)
            q2 = _stack_halves(q_ref[rows, cs])
            o2, l = _attend([_dot_t(q2, kc_ref[:, cs]), _dot_t(q2, k_ref[:, cs])], [vc_ref[:, cs], v_ref[:, cs]])
            o2 = o2 / l
            o = o2[:tq] - lam * o2[tq:]
            o = o * lax.rsqrt(jnp.mean(o * o, axis=-1, keepdims=True) + NORM_EPS) * g_ref[...]
            o_ref[rows, cs] = (o * (1.0 - lam_init)).astype(BF16)


def _diffattn(px, pc, lam_p, subln, lam_init, *, bsz, n_tok, ctx_len):
    nq = n_tok // DIFF_TQ
    body = functools.partial(_diffattn_body, lam_init=lam_init)
    return pl.pallas_call(
        body, out_shape=jax.ShapeDtypeStruct((bsz * n_tok, SEG), BF16), grid=(bsz, nq),
        in_specs=[pl.BlockSpec((DIFF_TQ, SEG), lambda b, j: (b * nq + j, 3)),
                  pl.BlockSpec((n_tok, SEG), lambda b, j: (b, 4)),
                  pl.BlockSpec((n_tok, SEG), lambda b, j: (b, 5)),
                  pl.BlockSpec((ctx_len, SEG), lambda b, j: (b, 0)),
                  pl.BlockSpec((ctx_len, SEG), lambda b, j: (b, 1)),
                  pl.BlockSpec((4, D_QK_DIM), lambda b, j: (0, 0)),
                  pl.BlockSpec((1, D_V_DIM), lambda b, j: (0, 0))],
        out_specs=pl.BlockSpec((DIFF_TQ, SEG), lambda b, j: (b * nq + j, 0)),
        compiler_params=_cparams("arbitrary", "arbitrary"), name="diff_attn",
    )(px, px, px, pc, pc, lam_p, subln.reshape(1, D_V_DIM))


def _oproj_body(a_ref, b_ref, x_ref, mod_ref, g_ref, wo_ref, wrh_ref, wrl_ref, br_ref, cin_ref,
                x1_ref, h_ref, bucket_ref, rank_ref, cout_ref, carry_ref, before_ref):
    i = pl.program_id(0)
    tm = x_ref.shape[0]

    @pl.when(i == 0)
    def _():
        carry_ref[...] = cin_ref[...]
        r_i = lax.broadcasted_iota(jnp.int32, (tm, tm), 0)
        c_i = lax.broadcasted_iota(jnp.int32, (tm, tm), 1)
        before_ref[...] = jnp.where(r_i > c_i, 1.0, 0.0).astype(BF16)

    def first_max(v, ln):
        m = jnp.max(v, axis=1, keepdims=True)
        return jnp.min(jnp.where(v == m, ln, jnp.int32(V7X_LANES)), axis=1, keepdims=True)

    def route(lg, ln):
        g_sel = first_max(jnp.where(ln < N_GROUPS, lg, -jnp.inf), ln)
        e_lane0 = ROUTER_EXPERT_LANE0 + EXPERTS_PER_GROUP * g_sel
        in_group = (ln >= e_lane0) & (ln < e_lane0 + EXPERTS_PER_GROUP)
        el = jnp.where(in_group, lg, -jnp.inf)
        i1 = first_max(el, ln)
        i2 = first_max(jnp.where(ln == i1, -jnp.inf, el), ln)
        lo = jnp.minimum(i1, i2) - e_lane0
        hi = jnp.maximum(i1, i2) - e_lane0
        return g_sel * PAIRS_PER_GROUP + ((lo * (2 * EXPERTS_PER_GROUP - 1 - lo)) >> 1) + (hi - lo - 1)

    rb = tm // ROUTE_BLOCKS
    lane_b = lax.broadcasted_iota(jnp.int32, (rb, V7X_LANES), 1)
    oh_blocks = []
    for r in range(ROUTE_BLOCKS):
        rows = pl.ds(r * rb, rb)
        y = _dot(a_ref[rows, :], wo_ref[0:SEG, :]) + _dot(b_ref[rows, :], wo_ref[SEG:2 * SEG, :])
        x1 = x_ref[rows, :] + mod_ref[0, 2:3, :] * y
        x1_ref[rows, :] = x1
        h = _rmsnorm_mod(x1, g_ref[...], mod_ref[0, 4:5, :], mod_ref[0, 3:4, :])
        _store_token_tiles(h_ref.at[pl.ds(r * rb * V7X_SUBLANES, rb * V7X_SUBLANES)], h)
        h_hi, h_lo = _split_bf16(h)
        logits = _dot3(h_hi, h_lo, wrh_ref[...], wrl_ref[...]) + br_ref[...]
        bucket = route(logits, lane_b)
        bucket_ref[rows, :] = bucket
        oh_blocks.append(jnp.where(lane_b == bucket, 1.0, 0.0))
    oh = jnp.concatenate(oh_blocks, axis=0)
    cum = _dot(before_ref[...], oh.astype(BF16)) + carry_ref[...]
    rank = jnp.sum(oh * cum, axis=1, keepdims=True)
    rank_ref[...] = rank.astype(jnp.int32)
    carry_ref[...] = carry_ref[...] + jnp.sum(oh, axis=0, keepdims=True)

    @pl.when(i == pl.num_programs(0) - 1)
    def _():
        cout_ref[...] = carry_ref[...]


def _oproj(a, b, x, mod, g, wo, wr_hi, wr_lo, br, counts_in, *, seq, mod_row, name="oproj"):
    t, d = x.shape
    tm = min(PROJ_TM, t if seq is None else seq)
    if seq is None:
        mod_map = lambda i: (mod_row, 0, 0)
    else:
        tiles_per_seq = seq // tm
        mod_map = lambda i: (i // tiles_per_seq, 0, 0)
    row = lambda i: (i, 0)
    const = lambda i: (0, 0)
    return pl.pallas_call(
        _oproj_body,
        out_shape=[jax.ShapeDtypeStruct((t, d), F32), jax.ShapeDtypeStruct((t * V7X_SUBLANES, V7X_LANES), F32),
                   jax.ShapeDtypeStruct((t, 1), jnp.int32), jax.ShapeDtypeStruct((t, 1), jnp.int32),
                   jax.ShapeDtypeStruct((1, V7X_LANES), F32)],
        grid=(t // tm,),
        in_specs=[pl.BlockSpec((tm, SEG), row), pl.BlockSpec((tm, SEG), row), pl.BlockSpec((tm, d), row),
                  pl.BlockSpec((1, N_MOD, d), mod_map), pl.BlockSpec((1, d), const),
                  pl.BlockSpec((2 * SEG, d), const), pl.BlockSpec((d, V7X_LANES), const),
                  pl.BlockSpec((d, V7X_LANES), const), pl.BlockSpec((1, V7X_LANES), const),
                  pl.BlockSpec((1, V7X_LANES), const)],
        out_specs=[pl.BlockSpec((tm, d), row), pl.BlockSpec((tm * V7X_SUBLANES, V7X_LANES), row),
                   pl.BlockSpec((tm, 1), row),
                   pl.BlockSpec((tm, 1), row), pl.BlockSpec((1, V7X_LANES), const)],
        scratch_shapes=[pltpu.VMEM((1, V7X_LANES), F32), pltpu.VMEM((tm, tm), BF16)],
        compiler_params=_cparams("arbitrary"), name=name,
    )(a, b, x, mod, g.reshape(1, d), wo, wr_hi, wr_lo, br, counts_in)


def _router_weights(wg, bg, we, be):
    d = wg.shape[0]
    w = jnp.zeros((d, V7X_LANES), F32)
    w = w.at[:, ROUTER_GROUP_LANE0:ROUTER_GROUP_LANE0 + N_GROUPS].set(wg)
    w = w.at[:, ROUTER_EXPERT_LANE0:ROUTER_EXPERT_LANE0 + N_EXPERTS].set(we)
    b = jnp.zeros((1, V7X_LANES), F32)
    b = b.at[0, ROUTER_GROUP_LANE0:ROUTER_GROUP_LANE0 + N_GROUPS].set(bg)
    b = b.at[0, ROUTER_EXPERT_LANE0:ROUTER_EXPERT_LANE0 + N_EXPERTS].set(be)
    w_hi = w.astype(BF16)
    w_lo = (w - w_hi.astype(F32)).astype(BF16)
    return w_hi, w_lo, b


def _token_tile(ref, t):
    return ref.at[pl.ds(pl.multiple_of(t * V7X_SUBLANES, V7X_SUBLANES), V7X_SUBLANES)]


def _scatter_body(off_ref, bucket_ref, rank_ref, rows_ref, sorted_hbm, out_hbm, sem, *, chunk):
    del sorted_hbm

    def start(t, carry):
        p = off_ref[bucket_ref[t]] + rank_ref[t]
        pltpu.make_async_copy(_token_tile(rows_ref, t), _token_tile(out_hbm, p), sem).start()
        return carry

    lax.fori_loop(0, chunk, start, 0, unroll=8)
    pltpu.make_async_copy(rows_ref, out_hbm.at[pl.ds(0, chunk * V7X_SUBLANES)], sem).wait()


def _gather_body(off_ref, bucket_ref, rank_ref, sorted_hbm, out_ref, sem, *, chunk):
    def start(t, carry):
        p = off_ref[bucket_ref[t]] + rank_ref[t]
        pltpu.make_async_copy(_token_tile(sorted_hbm, p), _token_tile(out_ref, t), sem).start()
        return carry

    lax.fori_loop(0, chunk, start, 0, unroll=8)
    pltpu.make_async_copy(sorted_hbm.at[pl.ds(0, chunk * V7X_SUBLANES)], out_ref, sem).wait()


def _permute(offsets, bucket, rank, rows, sorted_rows, *, n_rows, scatter, name):
    chunk = min(PERM_CHUNK, n_rows)
    smem_blk = pl.BlockSpec((chunk,), lambda i, *_: (i,), memory_space=pltpu.SMEM)
    any_spec = pl.BlockSpec(memory_space=pl.ANY)
    vmem_blk = pl.BlockSpec((chunk * V7X_SUBLANES, V7X_LANES), lambda i, *_: (i, 0))
    if scatter:
        body = functools.partial(_scatter_body, chunk=chunk)
        args = (offsets, bucket, rank, rows, sorted_rows)
        in_specs = [smem_blk, smem_blk, vmem_blk, any_spec]
        out_shape = jax.ShapeDtypeStruct(sorted_rows.shape, sorted_rows.dtype)
        out_specs = any_spec
        aliases = {4: 0}
    else:
        body = functools.partial(_gather_body, chunk=chunk)
        args = (offsets, bucket, rank, sorted_rows)
        in_specs = [smem_blk, smem_blk, any_spec]
        out_shape = jax.ShapeDtypeStruct((n_rows * V7X_SUBLANES, V7X_LANES), sorted_rows.dtype)
        out_specs = vmem_blk
        aliases = {}
    grid_spec = pltpu.PrefetchScalarGridSpec(
        num_scalar_prefetch=1, grid=(n_rows // chunk,), in_specs=in_specs, out_specs=out_specs,
        scratch_shapes=[pltpu.SemaphoreType.DMA])
    return pl.pallas_call(
        body, out_shape=out_shape, grid_spec=grid_spec, input_output_aliases=aliases,
        compiler_params=pltpu.CompilerParams(dimension_semantics=("arbitrary",), has_side_effects=True,
                                             vmem_limit_bytes=V7X_VMEM_LIMIT_BYTES),
        name=name,
    )(*args)


def _moe_body(te1_ref, te2_ref, tval_ref, trow_ref, x_ref, w1a_ref, w1b_ref, w3a_ref, w3b_ref, w2a_ref, w2b_ref,
              wr_ref, br_ref, y_ref):
    i = pl.program_id(0)

    @pl.when(tval_ref[i] == 0)
    def _():
        y_ref[...] = jnp.zeros_like(y_ref)

    @pl.when(tval_ref[i] > 0)
    def _():
        tm = x_ref.shape[0] // V7X_SUBLANES
        x = _load_token_tiles(x_ref).astype(BF16)
        e1 = te1_ref[i]
        e2 = te2_ref[i]
        grp = e1 >> 3
        logits = _dot(x, wr_ref[...]) + br_ref[...]
        lane = lax.broadcasted_iota(jnp.int32, (tm, V7X_LANES), 1)
        pick = lambda idx: jnp.sum(jnp.where(lane == idx, logits, 0.0), axis=1, keepdims=True)
        gl = jnp.where(lane < N_GROUPS, logits, -jnp.inf)
        gm = jnp.max(gl, axis=1, keepdims=True)
        gz = jnp.sum(jnp.exp(gl - gm), axis=1, keepdims=True)
        g_w = jnp.exp(pick(ROUTER_GROUP_LANE0 + grp) - gm) / gz
        l1 = pick(ROUTER_EXPERT_LANE0 + e1)
        l2 = pick(ROUTER_EXPERT_LANE0 + e2)
        m = jnp.maximum(l1, l2)
        p1 = jnp.exp(l1 - m)
        p2 = jnp.exp(l2 - m)
        c1 = g_w * p1 / (p1 + p2)
        c2 = g_w * p2 / (p1 + p2)

        def expert(w1_ref, w3_ref, w2_ref):
            h = _dot(x, w1_ref[0])
            hid = (h * _sigmoid(h)) * _dot(x, w3_ref[0])
            return _dot(hid.astype(BF16), w2_ref[0])

        _store_token_tiles(y_ref, c1 * expert(w1a_ref, w3a_ref, w2a_ref) + c2 * expert(w1b_ref, w3b_ref, w2b_ref))


def _moe_sorted(xs, te1, te2, tval, trow, w1, w3, w2, wr_hi, br):
    d = w1.shape[1]
    n_tiles = xs.shape[0] // (MOE_TM * V7X_SUBLANES)
    tile_blk = (MOE_TM * V7X_SUBLANES, V7X_LANES)
    xmap = lambda i, te1, te2, tval, trow: (trow[i], 0)
    wa = lambda i, te1, te2, tval, trow: (te1[i], 0, 0)
    wb = lambda i, te1, te2, tval, trow: (te2[i], 0, 0)
    const = lambda i, *_: (0, 0)
    grid_spec = pltpu.PrefetchScalarGridSpec(
        num_scalar_prefetch=4, grid=(n_tiles,),
        in_specs=[pl.BlockSpec(tile_blk, xmap),
                  pl.BlockSpec((1, d, D_EXPERT), wa), pl.BlockSpec((1, d, D_EXPERT), wb),
                  pl.BlockSpec((1, d, D_EXPERT), wa), pl.BlockSpec((1, d, D_EXPERT), wb),
                  pl.BlockSpec((1, D_EXPERT, d), wa), pl.BlockSpec((1, D_EXPERT, d), wb),
                  pl.BlockSpec((d, V7X_LANES), const), pl.BlockSpec((1, V7X_LANES), const)],
        out_specs=pl.BlockSpec(tile_blk, lambda i, *_: (i, 0)))
    return pl.pallas_call(
        _moe_body, out_shape=jax.ShapeDtypeStruct(xs.shape, F32), grid_spec=grid_spec,
        compiler_params=_cparams("arbitrary"), name="moe_experts",
    )(te1, te2, tval, trow, xs, w1, w1, w3, w3, w2, w2, wr_hi, br)


def _moe(h_parts, route_parts, counts, w1, w3, w2, wr_hi, br):
    n_rows = [h.shape[0] // V7X_SUBLANES for h in h_parts]
    n_tiles = sum(n_rows) // MOE_TM + N_BUCKETS
    cnt = counts[0, :N_BUCKETS].astype(jnp.int32)
    tiles_b = (cnt + MOE_TM - 1) // MOE_TM
    tile_end = jnp.cumsum(tiles_b)
    n_valid = tile_end[-1]
    offsets = jnp.zeros((V7X_LANES,), jnp.int32).at[:N_BUCKETS].set((tile_end - tiles_b) * MOE_TM)
    tile_id = jnp.arange(n_tiles, dtype=jnp.int32)
    tval = (tile_id < n_valid).astype(jnp.int32)
    trow = jnp.minimum(tile_id, n_valid - 1)
    tbucket = jnp.sum((trow[:, None] >= tile_end[None, :]).astype(jnp.int32), axis=1)
    tbucket = jnp.minimum(tbucket, N_BUCKETS - 1)
    te1 = jnp.asarray(_BUCKET_E1)[tbucket]
    te2 = jnp.asarray(_BUCKET_E2)[tbucket]

    xs = jnp.zeros((n_tiles * MOE_TM * V7X_SUBLANES, V7X_LANES), F32)
    for h, n, (bucket, rank) in zip(h_parts, n_rows, route_parts):
        xs = _permute(offsets, bucket, rank, h, xs, n_rows=n, scatter=True, name="moe_scatter_rows")
    ys = _moe_sorted(xs, te1, te2, tval, trow, w1, w3, w2, wr_hi, br)
    return [_permute(offsets, bucket, rank, None, ys, n_rows=n, scatter=False, name="moe_gather_rows")
            for n, (bucket, rank) in zip(n_rows, route_parts)]


def _final_body(x_ref, y_ref, mod_ref, g_ref, o_ref):
    x = x_ref[...] + mod_ref[0, 5:6, :] * _load_token_tiles(y_ref)
    o_ref[...] = x * lax.rsqrt(jnp.mean(x * x, axis=-1, keepdims=True) + NORM_EPS) * g_ref[...]


def _final(x, y, mod, g, *, seq):
    t, d = x.shape
    tm = min(PROJ_TM, seq)
    tiles_per_seq = seq // tm
    row = lambda i: (i, 0)
    return pl.pallas_call(
        _final_body, out_shape=jax.ShapeDtypeStruct((t, d), F32), grid=(t // tm,),
        in_specs=[pl.BlockSpec((tm, d), row), pl.BlockSpec((tm * V7X_SUBLANES, V7X_LANES), row),
                  pl.BlockSpec((1, N_MOD, d), lambda i: (i // tiles_per_seq, 0, 0)),
                  pl.BlockSpec((1, d), lambda i: (0, 0))],
        out_specs=pl.BlockSpec((tm, d), row),
        compiler_params=_cparams("arbitrary"), name="final_norm",
    )(x, y, mod, g.reshape(1, d))


def kernel(x, c, ctx, c_ctx, w_mod, b_mod, norm_mix, norm_ffn, norm_final, even_w_in, even_w_out, even_rpb,
           even_dw, even_dw_b, even_ln_g, even_ln_b, odd_w_in, odd_w_out, odd_conv, odd_lambda, odd_subln,
           moe_wg, moe_bg, moe_we, moe_be, moe_w1, moe_w3, moe_w2):
    bsz, n_tok, d = x.shape
    ctx_len = ctx.shape[1]
    depth = w_mod.shape[0]
    assert depth == 2 and d == D_MODEL
    xs = x.reshape(bsz * n_tok, d)
    cs = ctx.reshape(bsz * ctx_len, d)

    mod_rows = -(-(bsz + 1) // 8) * 8
    cc = jnp.zeros((mod_rows, d), F32).at[:bsz].set(c).at[bsz].set(c_ctx)
    mod = _modulation(cc, w_mod, b_mod).reshape(depth, mod_rows, N_MOD, d)
    bf = lambda a: a.astype(BF16)
    ew = _expert_weights_bf16
    zero_counts = jnp.zeros((1, V7X_LANES), F32)

    l = 0
    w_in = bf(even_w_in[0])
    w_out = bf(even_w_out[0])
    px = _proj(xs, mod[l], norm_mix[l], w_in, seq=n_tok, mod_row=None, q_seg=0, name="proj_even_x")
    pc = _proj(cs, mod[l], norm_mix[l], w_in, seq=None, mod_row=bsz, q_seg=0, name="proj_even_c")
    a_x = _natten(px, pc, even_rpb[0], bsz=bsz, n_tok=n_tok, ctx_len=ctx_len)
    a_c = _ctxattn(pc, bsz=bsz, ctx_len=ctx_len)
    b_x = _conformer(px, even_dw[0], even_dw_b[0], even_ln_g[0], even_ln_b[0], bsz=bsz, seq=n_tok)
    b_c = _conformer(pc, even_dw[0], even_dw_b[0], even_ln_g[0], even_ln_b[0], bsz=bsz, seq=ctx_len)
    wr_hi, wr_lo, br = _router_weights(moe_wg[l], moe_bg[l], moe_we[l], moe_be[l])
    x1, hx, bkt_x, rnk_x, counts = _oproj(a_x, b_x, xs, mod[l], norm_ffn[l], w_out, wr_hi, wr_lo, br, zero_counts,
                                          seq=n_tok, mod_row=None, name="oproj_even_x")
    c1, hc, bkt_c, rnk_c, counts = _oproj(a_c, b_c, cs, mod[l], norm_ffn[l], w_out, wr_hi, wr_lo, br, counts,
                                          seq=None, mod_row=bsz, name="oproj_even_c")
    y_x, y_c = _moe([hx, hc], [(bkt_x.reshape(-1), rnk_x.reshape(-1)), (bkt_c.reshape(-1), rnk_c.reshape(-1))],
                    counts, ew(moe_w1, l), ew(moe_w3, l), ew(moe_w2, l), wr_hi, br)

    l = 1
    lam_init = 0.8 - 0.6 * math.exp(-0.3 * l)
    w_in = bf(odd_w_in[0])
    w_out = bf(odd_w_out[0])
    rope = _rope_tables(n_tok)
    px, x2 = _proj(x1, mod[l], norm_mix[l], w_in, seq=n_tok, mod_row=None, y=y_x, res_mod=mod[l - 1], res_idx=5,
                   rope=rope, rope_segs=(3, 4), q_seg=3, write_x=True, name="proj_odd_x")
    pc = _proj(c1, mod[l], norm_mix[l], w_in[:, 4 * SEG:], seq=None, mod_row=bsz, y=y_c, res_mod=mod[l - 1],
               res_idx=5, name="proj_odd_c")
    s_x = _shortconv(px, odd_conv[0], bsz=bsz, seq=n_tok)
    d_x = _diffattn(px, pc, odd_lambda[0], odd_subln[0], lam_init, bsz=bsz, n_tok=n_tok, ctx_len=ctx_len)
    wr_hi, wr_lo, br = _router_weights(moe_wg[l], moe_bg[l], moe_we[l], moe_be[l])
    x3, hx, bkt_x, rnk_x, counts = _oproj(s_x, d_x, x2, mod[l], norm_ffn[l], w_out, wr_hi, wr_lo, br, zero_counts,
                                          seq=n_tok, mod_row=None, name="oproj_odd_x")
    (y_x,) = _moe([hx], [(bkt_x.reshape(-1), rnk_x.reshape(-1))], counts,
                  ew(moe_w1, l), ew(moe_w3, l), ew(moe_w2, l), wr_hi, br)
    out = _final(x3, y_x, mod[l], norm_final, seq=n_tok)
    return out.reshape(bsz, n_tok, d)
```

```python
import functools
import math

import numpy as np
import jax
import jax.numpy as jnp
from jax import lax
from jax.experimental import pallas as pl
from jax.experimental.pallas import tpu as pltpu

F32 = jnp.float32
BF16 = jnp.bfloat16

D_MODEL = 1024
GRID_W = 64
N_MOD = 6
NORM_EPS = 1e-6
NEG_INF = -1e30
ROPE_BASE = 10000.0
SEG = 512
A_HEAD_DIM = 64
A_HEADS = 8
WIN_ROWS_MAX = 8
WIN_COLS = 16
B_CONV = 31
C_CONV = 3
D_QK_DIM = 64
D_V_DIM = 128
D_HEADS = 4
N_GROUPS = 4
EXPERTS_PER_GROUP = 8
N_EXPERTS = 32
D_EXPERT = 256
LOG2E = math.log2(math.e)
QUERY_SCALE = A_HEAD_DIM ** -0.5 * LOG2E
PAIRS_PER_GROUP = EXPERTS_PER_GROUP * (EXPERTS_PER_GROUP - 1) // 2
N_BUCKETS = N_GROUPS * PAIRS_PER_GROUP

V7X_LANES = 128
V7X_SUBLANES = 8
V7X_VMEM_LIMIT_BYTES = 56 * 1024 * 1024

PROJ_TM = 512
ATT_TQ = 256
NAT_TILES_PER_STEP = 4
DIFF_TQ = 1024
NAT_KROWS = 12
CONV_CHUNK = 64
CONV_PAD = 16
ROUTE_BLOCKS = 1
MOE_TM = 256
MOE_TILES_PER_STEP = 1
PERM_CHUNK = 1024
ROUTER_GROUP_LANE0 = 0
ROUTER_EXPERT_LANE0 = 8

_PAIR_LO = np.array([i for i in range(8) for j in range(i + 1, 8)], np.int32)
_PAIR_HI = np.array([j for i in range(8) for j in range(i + 1, 8)], np.int32)
_BUCKET_E1 = np.concatenate([g * 8 + _PAIR_LO for g in range(N_GROUPS)]).astype(np.int32)
_BUCKET_E2 = np.concatenate([g * 8 + _PAIR_HI for g in range(N_GROUPS)]).astype(np.int32)


def _cparams(*sem):
    return pltpu.CompilerParams(dimension_semantics=tuple(sem), vmem_limit_bytes=V7X_VMEM_LIMIT_BYTES)


def _dot(a, b):
    return jnp.dot(a, b, preferred_element_type=F32)


def _dot_t(a, b):
    return lax.dot_general(a, b, (((1,), (1,)), ((), ())), preferred_element_type=F32)


def _split_bf16(a):
    hi = a.astype(BF16)
    lo = (a - hi.astype(F32)).astype(BF16)
    return hi, lo


def _dot3(a_hi, a_lo, b_hi, b_lo):
    return _dot(a_hi, b_hi) + _dot(a_lo, b_hi) + _dot(a_hi, b_lo)


def _sigmoid(x):
    return 1.0 / (1.0 + jnp.exp(-x))


def _load_token_tiles(ref):
    tm = ref.shape[0] // V7X_SUBLANES
    return jnp.concatenate([ref[pl.ds(c, tm, stride=V7X_SUBLANES), :] for c in range(V7X_SUBLANES)], axis=1)


def _store_token_tiles(ref, v):
    tm = v.shape[0]
    for c in range(V7X_SUBLANES):
        ref[pl.ds(c, tm, stride=V7X_SUBLANES), :] = v[:, c * V7X_LANES:(c + 1) * V7X_LANES]


def _rmsnorm_mod(x, g, scale, shift):
    y = x * lax.rsqrt(jnp.mean(x * x, axis=-1, keepdims=True) + NORM_EPS)
    return (y * g) * (1.0 + scale) + shift


def _cast_body(w_ref, o_ref):
    o_ref[...] = w_ref[0].astype(BF16)


def _expert_weights_bf16(w, layer):
    _, n_exp, k, n = w.shape
    eb = 4
    return pl.pallas_call(
        _cast_body, out_shape=jax.ShapeDtypeStruct((n_exp, k, n), BF16), grid=(n_exp // eb,),
        in_specs=[pl.BlockSpec((1, eb, k, n), lambda e: (layer, e, 0, 0))],
        out_specs=pl.BlockSpec((eb, k, n), lambda e: (e, 0, 0)),
        compiler_params=_cparams("arbitrary"), name="expert_weights_bf16",
    )(w)


def _mod_body(c_ref, w_ref, b_ref, o_ref):
    c = c_ref[...]
    s = c * _sigmoid(c)
    s_hi, s_lo = _split_bf16(s)
    w_hi, w_lo = _split_bf16(w_ref[0])
    o_ref[0] = _dot3(s_hi, s_lo, w_hi, w_lo) + b_ref[0]


def _modulation(cc, w_mod, b_mod):
    depth, d, n = w_mod.shape
    rows = cc.shape[0]
    tn = 1536
    return pl.pallas_call(
        _mod_body,
        out_shape=jax.ShapeDtypeStruct((depth, rows, n), F32),
        grid=(depth, n // tn),
        in_specs=[pl.BlockSpec((rows, d), lambda l, j: (0, 0)),
                  pl.BlockSpec((1, d, tn), lambda l, j: (l, 0, j)),
                  pl.BlockSpec((1, 1, tn), lambda l, j: (l, 0, j))],
        out_specs=pl.BlockSpec((1, rows, tn), lambda l, j: (l, 0, j)),
        compiler_params=_cparams("arbitrary", "arbitrary"),
        name="modulation",
    )(cc, w_mod, b_mod.reshape(depth, 1, n))


def _rope(v, cos, sin, lane):
    up = pltpu.roll(v, V7X_LANES - 16, axis=1)
    dn = pltpu.roll(v, 16, axis=1)
    sw = jnp.where((lane & 31) < 16, up, dn)
    return v * cos + sw * sin


def _proj_body(*refs, n_seg, res_idx, shift_idx, rope_segs, q_seg, write_x):
    refs = list(refs)
    gather_refs = [refs.pop(0) for _ in range(5)] if res_idx is not None else None
    x_ref = refs.pop(0)
    ys_hbm = refs.pop(0) if res_idx is not None else None
    rmod_ref = refs.pop(0) if res_idx is not None else None
    mod_ref = refs.pop(0)
    g_ref = refs.pop(0)
    w_ref = refs.pop(0)
    cos_ref = sin_ref = None
    if rope_segs:
        cos_ref = refs.pop(0)
        sin_ref = refs.pop(0)
    out_ref = refs.pop(0)
    xo_ref = refs.pop(0) if write_x else None
    drain = lambda: None

    x = x_ref[...]
    if res_idx is not None:
        y_ref, sem = refs
        y, drain = _gathered_rows(*gather_refs, ys_hbm, y_ref, sem)
        x = x + rmod_ref[0, res_idx:res_idx + 1, :] * y
        if write_x:
            xo_ref[...] = x
    h = _rmsnorm_mod(x, g_ref[...], mod_ref[0, shift_idx + 1:shift_idx + 2, :],
                     mod_ref[0, shift_idx:shift_idx + 1, :]).astype(BF16)
    for s in range(n_seg):
        o = _dot(h, w_ref[:, s * SEG:(s + 1) * SEG])
        if s == q_seg:
            o = o * QUERY_SCALE
        if s in rope_segs:
            lane = lax.broadcasted_iota(jnp.int32, (o.shape[0], V7X_LANES), 1)
            cos = cos_ref[...]
            sin = sin_ref[...]
            o = jnp.concatenate(
                [_rope(o[:, c * V7X_LANES:(c + 1) * V7X_LANES], cos, sin, lane) for c in range(SEG // V7X_LANES)],
                axis=1)
        out_ref[:, s * SEG:(s + 1) * SEG] = o.astype(BF16)
    drain()


def _proj(x, mod, g, w, *, seq, mod_row, res=None, res_idx=None, shift_idx=0, rope=None, rope_segs=(), q_seg=None,
          write_x=False, name="proj"):
    t, d = x.shape
    n = w.shape[1]
    tm = min(PROJ_TM, t if seq is None else seq)
    n_steps = t // tm
    tiles_per_seq = None if seq is None else seq // tm
    if seq is None:
        mod_map = lambda i, *_: (mod_row, 0, 0)
    else:
        mod_map = lambda i, *_: (i // tiles_per_seq, 0, 0)
    row = lambda i, *_: (i, 0)
    const = lambda i, *_: (0, 0)
    prefetch, args, specs, scratch = [], [], [], []
    if res is not None:
        ys, offsets, bucket, rank, res_mod = res
        prefetch = [offsets]
        args += [bucket, rank, bucket, rank]
        specs += _gather_specs(tm, n_steps)
        scratch = _gather_scratch(tm)
    args.append(x)
    specs.append(pl.BlockSpec((tm, d), row))
    if res is not None:
        args += [ys, res_mod]
        specs += [pl.BlockSpec(memory_space=pl.ANY), pl.BlockSpec((1, N_MOD, d), mod_map)]
    args += [mod, g.reshape(1, d), w]
    specs += [pl.BlockSpec((1, N_MOD, d), mod_map), pl.BlockSpec((1, d), const), pl.BlockSpec((d, n), const)]
    if rope_segs:
        args += [rope[0], rope[1]]
        specs += [pl.BlockSpec((tm, V7X_LANES), lambda i, *_: (i % tiles_per_seq, 0))] * 2
    out_shape = [jax.ShapeDtypeStruct((t, n), BF16)]
    out_specs = [pl.BlockSpec((tm, n), row)]
    if write_x:
        out_shape.append(jax.ShapeDtypeStruct((t, d), F32))
        out_specs.append(pl.BlockSpec((tm, d), row))
    body = functools.partial(_proj_body, n_seg=n // SEG, res_idx=res_idx if res is not None else None,
                             shift_idx=shift_idx, rope_segs=tuple(rope_segs), q_seg=q_seg, write_x=write_x)
    grid_spec = pltpu.PrefetchScalarGridSpec(
        num_scalar_prefetch=len(prefetch), grid=(n_steps,), in_specs=specs, out_specs=out_specs,
        scratch_shapes=scratch)
    outs = pl.pallas_call(
        body, out_shape=out_shape, grid_spec=grid_spec, compiler_params=_cparams("arbitrary"), name=name,
    )(*prefetch, *args)
    return outs if write_x else outs[0]


def _rope_tables(n_tok):
    quarter = D_QK_DIM // 4
    t = np.arange(n_tok)
    row = (t // GRID_W).astype(np.float32)
    col = (t % GRID_W).astype(np.float32)
    inv = jnp.power(ROPE_BASE, -jnp.arange(quarter, dtype=F32) / quarter)
    ar = jnp.asarray(row)[:, None] * inv
    ac = jnp.asarray(col)[:, None] * inv
    cos64 = jnp.concatenate([jnp.cos(ar), jnp.cos(ar), jnp.cos(ac), jnp.cos(ac)], axis=-1)
    sin64 = jnp.concatenate([-jnp.sin(ar), jnp.sin(ar), -jnp.sin(ac), jnp.sin(ac)], axis=-1)
    return jnp.tile(cos64, (1, V7X_LANES // D_QK_DIM)), jnp.tile(sin64, (1, V7X_LANES // D_QK_DIM))


def _stack_halves(q):
    lane = lax.broadcasted_iota(jnp.int32, q.shape, 1)
    zero = jnp.zeros_like(q)
    return jnp.concatenate([jnp.where(lane < 64, q, zero), jnp.where(lane >= 64, q, zero)], axis=0)


def _attend(scores, values):
    m = functools.reduce(jnp.maximum, [jnp.max(s, axis=1, keepdims=True) for s in scores])
    acc = None
    for s, v in zip(scores, values):
        p = jnp.exp2(s - m).astype(BF16)
        lane = lax.broadcasted_iota(jnp.int32, v.shape, 1)
        ones_col = jnp.where(lane == 0, 1.0, 0.0).astype(BF16)
        o = _dot(p, jnp.concatenate([v, ones_col], axis=1))
        acc = o if acc is None else acc + o
    return acc[:, :V7X_LANES], acc[:, V7X_LANES:V7X_LANES + 1]


def _natten_body(ks_ref, pid_ref, q_ref, k_ref, v_ref, kc_ref, vc_ref, bias_ref, o_ref):
    nk = NAT_KROWS * GRID_W
    tq = ATT_TQ
    lane = lax.broadcasted_iota(jnp.int32, (tq, V7X_LANES), 1)
    for s in range(q_ref.shape[0] // tq):
        j = pl.program_id(1) * (q_ref.shape[0] // tq) + s
        k0 = pl.multiple_of(ks_ref[j] * GRID_W, GRID_W)
        p = pid_ref[j]
        rows = pl.ds(s * tq, tq)
        for hp in range(A_HEADS // 2):
            cs = slice(hp * V7X_LANES, (hp + 1) * V7X_LANES)
            q2 = _stack_halves(q_ref[rows, cs])
            s_c = _dot_t(q2, kc_ref[:, cs])
            s_l = _dot_t(q2, k_ref[pl.ds(k0, nk), cs])
            bias = jnp.concatenate([bias_ref[p, 2 * hp], bias_ref[p, 2 * hp + 1]], axis=0).astype(F32)
            o2, l = _attend([s_c, s_l + bias], [vc_ref[:, cs], v_ref[pl.ds(k0, nk), cs]])
            o2 = o2 / l
            o_ref[rows, cs] = jnp.where(lane < 64, o2[:tq], o2[tq:]).astype(BF16)


def _natten_tables(rows):
    q_rows = ATT_TQ // GRID_W
    nqb = rows // q_rows
    win_r = min(WIN_ROWS_MAX, rows)
    col = np.arange(GRID_W)
    c_start = np.clip(col - WIN_COLS // 2, 0, GRID_W - WIN_COLS)
    col_ok = (col[None, :] >= c_start[:, None]) & (col[None, :] < c_start[:, None] + WIN_COLS)
    dc = np.clip(col[None, :] - col[:, None], -(WIN_COLS - 1), WIN_COLS - 1) + WIN_COLS - 1
    ks_list, pats, pid = [], [], []
    for j in range(nqb):
        r0 = j * q_rows
        ks = int(np.clip(r0 - win_r // 2, 0, rows - NAT_KROWS))
        qr = r0 + np.arange(q_rows)
        rs = np.clip(qr - win_r // 2, 0, rows - win_r)
        kr = ks + np.arange(NAT_KROWS)
        row_ok = (kr[None, :] >= rs[:, None]) & (kr[None, :] < rs[:, None] + win_r)
        dr = np.clip(kr[None, :] - qr[:, None] + WIN_ROWS_MAX - 1, 0, 2 * WIN_ROWS_MAX - 2)
        key = (row_ok.tobytes(), (dr * row_ok).tobytes())
        for n, (k_, *_rest) in enumerate(pats):
            if k_ == key:
                pid.append(n)
                break
        else:
            pid.append(len(pats))
            pats.append((key, row_ok, dr))
        ks_list.append(ks)
    row_ok = np.stack([p_[1] for p_ in pats])
    dr = np.stack([p_[2] for p_ in pats])
    return np.array(ks_list, np.int32), np.array(pid, np.int32), row_ok, dr, col_ok, dc


def _natten_bias(rpb, row_ok, dr, col_ok, dc):
    n_pat, q_rows, k_rows = dr.shape
    n_dr, n_dc = rpb.shape[1], rpb.shape[2]
    oh_r = np.zeros((n_pat * q_rows * k_rows, n_dr), np.float32)
    oh_r[np.arange(oh_r.shape[0]), dr.reshape(-1)] = 1.0
    oh_c = np.zeros((n_dc, GRID_W * GRID_W), np.float32)
    oh_c[dc.reshape(-1), np.arange(GRID_W * GRID_W)] = 1.0
    hi = lax.Precision.HIGHEST
    t1 = jnp.einsum("mr,hrc->hmc", jnp.asarray(oh_r), rpb, precision=hi)
    t2 = jnp.einsum("hmc,cn->hmn", t1, jnp.asarray(oh_c), precision=hi)
    t2 = t2.reshape(rpb.shape[0], n_pat, q_rows, k_rows, GRID_W, GRID_W)
    valid = row_ok[None, :, :, :, None, None] & col_ok[None, None, None, None, :, :]
    bias = jnp.where(valid, t2 * LOG2E, NEG_INF).transpose(1, 0, 2, 4, 3, 5)
    return bias.reshape(n_pat, rpb.shape[0], q_rows * GRID_W, k_rows * GRID_W).astype(BF16)


def _natten(px, pc, rpb, *, bsz, n_tok, ctx_len):
    rows = n_tok // GRID_W
    assert rows >= NAT_KROWS and n_tok % ATT_TQ == 0
    ks, pid, row_ok, dr, col_ok, dc = _natten_tables(rows)
    bias = _natten_bias(rpb, row_ok, dr, col_ok, dc)
    n_pat = bias.shape[0]
    q_blk = ATT_TQ * NAT_TILES_PER_STEP
    assert n_tok % q_blk == 0
    nqb = n_tok // q_blk
    nk = NAT_KROWS * GRID_W
    grid_spec = pltpu.PrefetchScalarGridSpec(
        num_scalar_prefetch=2, grid=(bsz, nqb),
        in_specs=[pl.BlockSpec((q_blk, SEG), lambda b, j, *_: (b * nqb + j, 0)),
                  pl.BlockSpec((n_tok, SEG), lambda b, j, *_: (b, 1)),
                  pl.BlockSpec((n_tok, SEG), lambda b, j, *_: (b, 2)),
                  pl.BlockSpec((ctx_len, SEG), lambda b, j, *_: (b, 1)),
                  pl.BlockSpec((ctx_len, SEG), lambda b, j, *_: (b, 2)),
                  pl.BlockSpec((n_pat, A_HEADS, ATT_TQ, nk), lambda b, j, *_: (0, 0, 0, 0))],
        out_specs=pl.BlockSpec((q_blk, SEG), lambda b, j, *_: (b * nqb + j, 0)))
    return pl.pallas_call(
        _natten_body, out_shape=jax.ShapeDtypeStruct((bsz * n_tok, SEG), BF16), grid_spec=grid_spec,
        compiler_params=_cparams("arbitrary", "arbitrary"), name="natten",
    )(jnp.asarray(ks), jnp.asarray(pid), px, px, px, pc, pc, bias)


def _ctxattn_body(q_ref, k_ref, v_ref, o_ref):
    tq = q_ref.shape[0]
    lane = lax.broadcasted_iota(jnp.int32, (tq, V7X_LANES), 1)
    for hp in range(A_HEADS // 2):
        cs = slice(hp * V7X_LANES, (hp + 1) * V7X_LANES)
        q2 = _stack_halves(q_ref[:, cs])
        o2, l = _attend([_dot_t(q2, k_ref[:, cs])], [v_ref[:, cs]])
        o2 = o2 / l
        o_ref[:, cs] = jnp.where(lane < 64, o2[:tq], o2[tq:]).astype(BF16)


def _ctxattn(pc, *, bsz, ctx_len):
    return pl.pallas_call(
        _ctxattn_body, out_shape=jax.ShapeDtypeStruct((bsz * ctx_len, SEG), BF16), grid=(bsz,),
        in_specs=[pl.BlockSpec((ctx_len, SEG), lambda b: (b, 0)),
                  pl.BlockSpec((ctx_len, SEG), lambda b: (b, 1)),
                  pl.BlockSpec((ctx_len, SEG), lambda b: (b, 2))],
        out_specs=pl.BlockSpec((ctx_len, SEG), lambda b: (b, 0)),
        compiler_params=_cparams("arbitrary"), name="ctxattn",
    )(pc, pc, pc)


def _conv_accumulate(z_ref, w_ref, t0, taps):
    base = CONV_PAD - taps // 2
    blocks = []
    for c in range(SEG // V7X_LANES):
        cs = slice(c * V7X_LANES, (c + 1) * V7X_LANES)
        win = z_ref[pl.ds(t0, CONV_CHUNK + 2 * CONV_PAD), cs]
        acc = None
        for b in range(V7X_SUBLANES):
            ks = [k for k in range(taps) if (base + k) % V7X_SUBLANES == b]
            if not ks:
                continue
            shifted = win if b == 0 else pltpu.roll(win, win.shape[0] - b, axis=0)
            for k in ks:
                a0 = (base + k) // V7X_SUBLANES * V7X_SUBLANES
                term = shifted[a0:a0 + CONV_CHUNK, :] * w_ref[k:k + 1, cs]
                acc = term if acc is None else acc + term
        blocks.append(acc)
    return jnp.concatenate(blocks, axis=1)


def _conformer_body(u_ref, g_ref, w_ref, b_ref, lg_ref, lb_ref, o_ref, z_ref):
    n = u_ref.shape[0]
    halo = jnp.zeros((CONV_PAD, SEG), F32)
    z_ref[pl.ds(0, CONV_PAD), :] = halo
    z_ref[pl.ds(CONV_PAD + n, CONV_PAD), :] = halo
    z_ref[pl.ds(CONV_PAD, n), :] = u_ref[...].astype(F32) * _sigmoid(g_ref[...].astype(F32))

    def chunk(i, carry):
        t0 = pl.multiple_of(i * CONV_CHUNK, CONV_CHUNK)
        y = _conv_accumulate(z_ref, w_ref, t0, B_CONV) + b_ref[...]
        mu = jnp.mean(y, axis=-1, keepdims=True)
        yc = y - mu
        var = jnp.mean(yc * yc, axis=-1, keepdims=True)
        y = yc * lax.rsqrt(var + NORM_EPS) * lg_ref[...] + lb_ref[...]
        o_ref[pl.ds(t0, CONV_CHUNK), :] = (y * _sigmoid(y)).astype(BF16)
        return carry

    lax.fori_loop(0, n // CONV_CHUNK, chunk, 0)


def _conformer(p, dw, dw_b, ln_g, ln_b, *, bsz, seq):
    vec = lambda a: a.reshape(1, SEG)
    return pl.pallas_call(
        _conformer_body, out_shape=jax.ShapeDtypeStruct((bsz * seq, SEG), BF16), grid=(bsz,),
        in_specs=[pl.BlockSpec((seq, SEG), lambda b: (b, 3)),
                  pl.BlockSpec((seq, SEG), lambda b: (b, 4)),
                  pl.BlockSpec((B_CONV, SEG), lambda b: (0, 0))] + [pl.BlockSpec((1, SEG), lambda b: (0, 0))] * 3,
        out_specs=pl.BlockSpec((seq, SEG), lambda b: (b, 0)),
        scratch_shapes=[pltpu.VMEM((seq + 2 * CONV_PAD, SEG), F32)],
        compiler_params=_cparams("arbitrary"), name="conformer_conv",
    )(p, p, dw, vec(dw_b), vec(ln_g), vec(ln_b))


def _shortconv_body(bg_ref, cg_ref, u_ref, w_ref, o_ref, z_ref):
    n = u_ref.shape[0]
    halo = jnp.zeros((CONV_PAD, SEG), F32)
    z_ref[pl.ds(0, CONV_PAD), :] = halo
    z_ref[pl.ds(CONV_PAD + n, CONV_PAD), :] = halo
    z_ref[pl.ds(CONV_PAD, n), :] = cg_ref[...].astype(F32) * u_ref[...].astype(F32)

    def chunk(i, carry):
        t0 = pl.multiple_of(i * CONV_CHUNK, CONV_CHUNK)
        y = _conv_accumulate(z_ref, w_ref, t0, C_CONV)
        o_ref[pl.ds(t0, CONV_CHUNK), :] = (bg_ref[pl.ds(t0, CONV_CHUNK), :].astype(F32) * y).astype(BF16)
        return carry

    lax.fori_loop(0, n // CONV_CHUNK, chunk, 0)


def _shortconv(p, conv_w, *, bsz, seq):
    return pl.pallas_call(
        _shortconv_body, out_shape=jax.ShapeDtypeStruct((bsz * seq, SEG), BF16), grid=(bsz,),
        in_specs=[pl.BlockSpec((seq, SEG), lambda b: (b, 0)),
                  pl.BlockSpec((seq, SEG), lambda b: (b, 1)),
                  pl.BlockSpec((seq, SEG), lambda b: (b, 2)),
                  pl.BlockSpec((C_CONV, SEG), lambda b: (0, 0))],
        out_specs=pl.BlockSpec((seq, SEG), lambda b: (b, 0)),
        scratch_shapes=[pltpu.VMEM((seq + 2 * CONV_PAD, SEG), F32)],
        compiler_params=_cparams("arbitrary"), name="short_conv",
    )(p, p, p, conv_w)


def _diffattn_body(q_ref, k_ref, v_ref, kc_ref, vc_ref, lam_ref, g_ref, o_ref, *, lam_init):
    tq = ATT_TQ
    lp = lam_ref[...]
    lam = (jnp.exp(jnp.sum(lp[0:1, :] * lp[1:2, :], axis=1, keepdims=True))
           - jnp.exp(jnp.sum(lp[2:3, :] * lp[3:4, :], axis=1, keepdims=True)) + lam_init)
    for s in range(q_ref.shape[0] // tq):
        rows = pl.ds(s * tq, tq)
        for h in range(D_HEADS):
            cs = slice(h * V7X_LANES, (h + 1) * V7X_LANES)
            q2 = _stack_halves(q_ref[rows, cs])
            o2, l = _attend([_dot_t(q2, kc_ref[:, cs]), _dot_t(q2, k_ref[:, cs])], [vc_ref[:, cs], v_ref[:, cs]])
            o2 = o2 / l
            o = o2[:tq] - lam * o2[tq:]
            o = o * lax.rsqrt(jnp.mean(o * o, axis=-1, keepdims=True) + NORM_EPS) * g_ref[...]
            o_ref[rows, cs] = (o * (1.0 - lam_init)).astype(BF16)


def _diffattn(px, pc, lam_p, subln, lam_init, *, bsz, n_tok, ctx_len):
    nq = n_tok // DIFF_TQ
    body = functools.partial(_diffattn_body, lam_init=lam_init)
    return pl.pallas_call(
        body, out_shape=jax.ShapeDtypeStruct((bsz * n_tok, SEG), BF16), grid=(bsz, nq),
        in_specs=[pl.BlockSpec((DIFF_TQ, SEG), lambda b, j: (b * nq + j, 3)),
                  pl.BlockSpec((n_tok, SEG), lambda b, j: (b, 4)),
                  pl.BlockSpec((n_tok, SEG), lambda b, j: (b, 5)),
                  pl.BlockSpec((ctx_len, SEG), lambda b, j: (b, 0)),
                  pl.BlockSpec((ctx_len, SEG), lambda b, j: (b, 1)),
                  pl.BlockSpec((4, D_QK_DIM), lambda b, j: (0, 0)),
                  pl.BlockSpec((1, D_V_DIM), lambda b, j: (0, 0))],
        out_specs=pl.BlockSpec((DIFF_TQ, SEG), lambda b, j: (b * nq + j, 0)),
        compiler_params=_cparams("arbitrary", "arbitrary"), name="diff_attn",
    )(px, px, px, pc, pc, lam_p, subln.reshape(1, D_V_DIM))


def _oproj_body(a_ref, b_ref, x_ref, mod_ref, g_ref, wo_ref, wrh_ref, wrl_ref, br_ref, cin_ref,
                x1_ref, h_ref, bucket_ref, rank_ref, cout_ref, carry_ref, before_ref):
    i = pl.program_id(0)
    tm = x_ref.shape[0]

    @pl.when(i == 0)
    def _():
        carry_ref[...] = cin_ref[...]
        r_i = lax.broadcasted_iota(jnp.int32, (tm, tm), 0)
        c_i = lax.broadcasted_iota(jnp.int32, (tm, tm), 1)
        before_ref[...] = jnp.where(r_i > c_i, 1.0, 0.0).astype(BF16)

    def first_max(v, ln):
        m = jnp.max(v, axis=1, keepdims=True)
        return jnp.min(jnp.where(v == m, ln, jnp.int32(V7X_LANES)), axis=1, keepdims=True)

    def route(lg, ln):
        g_sel = first_max(jnp.where(ln < N_GROUPS, lg, -jnp.inf), ln)
        e_lane0 = ROUTER_EXPERT_LANE0 + EXPERTS_PER_GROUP * g_sel
        in_group = (ln >= e_lane0) & (ln < e_lane0 + EXPERTS_PER_GROUP)
        el = jnp.where(in_group, lg, -jnp.inf)
        i1 = first_max(el, ln)
        i2 = first_max(jnp.where(ln == i1, -jnp.inf, el), ln)
        lo = jnp.minimum(i1, i2) - e_lane0
        hi = jnp.maximum(i1, i2) - e_lane0
        return g_sel * PAIRS_PER_GROUP + ((lo * (2 * EXPERTS_PER_GROUP - 1 - lo)) >> 1) + (hi - lo - 1)

    rb = tm // ROUTE_BLOCKS
    lane_b = lax.broadcasted_iota(jnp.int32, (rb, V7X_LANES), 1)
    oh_blocks = []
    for r in range(ROUTE_BLOCKS):
        rows = pl.ds(r * rb, rb)
        y = _dot(a_ref[rows, :], wo_ref[0:SEG, :]) + _dot(b_ref[rows, :], wo_ref[SEG:2 * SEG, :])
        x1 = x_ref[rows, :] + mod_ref[0, 2:3, :] * y
        x1_ref[rows, :] = x1
        h = _rmsnorm_mod(x1, g_ref[...], mod_ref[0, 4:5, :], mod_ref[0, 3:4, :])
        _store_token_tiles(h_ref.at[pl.ds(r * rb * V7X_SUBLANES, rb * V7X_SUBLANES)], h)
        h_hi, h_lo = _split_bf16(h)
        logits = _dot3(h_hi, h_lo, wrh_ref[...], wrl_ref[...]) + br_ref[...]
        bucket = route(logits, lane_b)
        bucket_ref[rows, :] = bucket
        oh_blocks.append(jnp.where(lane_b == bucket, 1.0, 0.0))
    oh = jnp.concatenate(oh_blocks, axis=0)
    cum = _dot(before_ref[...], oh.astype(BF16)) + carry_ref[...]
    rank = jnp.sum(oh * cum, axis=1, keepdims=True)
    rank_ref[...] = rank.astype(jnp.int32)
    carry_ref[...] = carry_ref[...] + jnp.sum(oh, axis=0, keepdims=True)

    @pl.when(i == pl.num_programs(0) - 1)
    def _():
        cout_ref[...] = carry_ref[...]


def _oproj(a, b, x, mod, g, wo, wr_hi, wr_lo, br, counts_in, *, seq, mod_row, name="oproj"):
    t, d = x.shape
    tm = min(PROJ_TM, t if seq is None else seq)
    if seq is None:
        mod_map = lambda i: (mod_row, 0, 0)
    else:
        tiles_per_seq = seq // tm
        mod_map = lambda i: (i // tiles_per_seq, 0, 0)
    row = lambda i: (i, 0)
    const = lambda i: (0, 0)
    return pl.pallas_call(
        _oproj_body,
        out_shape=[jax.ShapeDtypeStruct((t, d), F32), jax.ShapeDtypeStruct((t * V7X_SUBLANES, V7X_LANES), F32),
                   jax.ShapeDtypeStruct((t, 1), jnp.int32), jax.ShapeDtypeStruct((t, 1), jnp.int32),
                   jax.ShapeDtypeStruct((1, V7X_LANES), F32)],
        grid=(t // tm,),
        in_specs=[pl.BlockSpec((tm, SEG), row), pl.BlockSpec((tm, SEG), row), pl.BlockSpec((tm, d), row),
                  pl.BlockSpec((1, N_MOD, d), mod_map), pl.BlockSpec((1, d), const),
                  pl.BlockSpec((2 * SEG, d), const), pl.BlockSpec((d, V7X_LANES), const),
                  pl.BlockSpec((d, V7X_LANES), const), pl.BlockSpec((1, V7X_LANES), const),
                  pl.BlockSpec((1, V7X_LANES), const)],
        out_specs=[pl.BlockSpec((tm, d), row), pl.BlockSpec((tm * V7X_SUBLANES, V7X_LANES), row),
                   pl.BlockSpec((tm, 1), row),
                   pl.BlockSpec((tm, 1), row), pl.BlockSpec((1, V7X_LANES), const)],
        scratch_shapes=[pltpu.VMEM((1, V7X_LANES), F32), pltpu.VMEM((tm, tm), BF16)],
        compiler_params=_cparams("arbitrary"), name=name,
    )(a, b, x, mod, g.reshape(1, d), wo, wr_hi, wr_lo, br, counts_in)


def _router_weights(wg, bg, we, be):
    d = wg.shape[0]
    w = jnp.zeros((d, V7X_LANES), F32)
    w = w.at[:, ROUTER_GROUP_LANE0:ROUTER_GROUP_LANE0 + N_GROUPS].set(wg)
    w = w.at[:, ROUTER_EXPERT_LANE0:ROUTER_EXPERT_LANE0 + N_EXPERTS].set(we)
    b = jnp.zeros((1, V7X_LANES), F32)
    b = b.at[0, ROUTER_GROUP_LANE0:ROUTER_GROUP_LANE0 + N_GROUPS].set(bg)
    b = b.at[0, ROUTER_EXPERT_LANE0:ROUTER_EXPERT_LANE0 + N_EXPERTS].set(be)
    w_hi = w.astype(BF16)
    w_lo = (w - w_hi.astype(F32)).astype(BF16)
    return w_hi, w_lo, b


def _token_tile(ref, t):
    return ref.at[pl.ds(pl.multiple_of(t * V7X_SUBLANES, V7X_SUBLANES), V7X_SUBLANES)]


def _scatter_body(off_ref, bucket_ref, rank_ref, rows_ref, sorted_hbm, out_hbm, sem, *, chunk):
    del sorted_hbm

    def start(t, carry):
        p = off_ref[bucket_ref[t]] + rank_ref[t]
        pltpu.make_async_copy(_token_tile(rows_ref, t), _token_tile(out_hbm, p), sem).start()
        return carry

    lax.fori_loop(0, chunk, start, 0, unroll=8)
    pltpu.make_async_copy(rows_ref, out_hbm.at[pl.ds(0, chunk * V7X_SUBLANES)], sem).wait()


def _gather_body(off_ref, bucket_ref, rank_ref, sorted_hbm, out_ref, sem, *, chunk):
    def start(t, carry):
        p = off_ref[bucket_ref[t]] + rank_ref[t]
        pltpu.make_async_copy(_token_tile(sorted_hbm, p), _token_tile(out_ref, t), sem).start()
        return carry

    lax.fori_loop(0, chunk, start, 0, unroll=8)
    pltpu.make_async_copy(sorted_hbm.at[pl.ds(0, chunk * V7X_SUBLANES)], out_ref, sem).wait()


def _gathered_rows(off_ref, b0_ref, r0_ref, bn_ref, rn_ref, sorted_hbm, y_ref, sem):
    i = pl.program_id(0)
    tm = y_ref.shape[0] // V7X_SUBLANES

    def issue(b_ref, r_ref):
        for t in range(tm):
            p = off_ref[b_ref[t]] + r_ref[t]
            pltpu.make_async_copy(_token_tile(sorted_hbm, p), _token_tile(y_ref, t), sem).start()

    def wait():
        pltpu.make_async_copy(sorted_hbm.at[pl.ds(0, tm * V7X_SUBLANES)], y_ref, sem).wait()

    @pl.when(i == 0)
    def _():
        issue(b0_ref, r0_ref)

    wait()
    y = _load_token_tiles(y_ref)
    issue(bn_ref, rn_ref)

    def drain():
        @pl.when(i == pl.num_programs(0) - 1)
        def _():
            wait()

    return y, drain


def _gather_specs(tm, n_steps):
    smem = lambda f: pl.BlockSpec((tm,), f, memory_space=pltpu.SMEM)
    cur = lambda i, *_: (i,)
    nxt = lambda i, *_: (jnp.minimum(i + 1, n_steps - 1),)
    return [smem(cur), smem(cur), smem(nxt), smem(nxt)]


def _gather_scratch(tm):
    return [pltpu.VMEM((tm * V7X_SUBLANES, V7X_LANES), F32), pltpu.SemaphoreType.DMA]


def _permute(offsets, bucket, rank, rows, sorted_rows, *, n_rows, scatter, name):
    chunk = min(PERM_CHUNK, n_rows)
    smem_blk = pl.BlockSpec((chunk,), lambda i, *_: (i,), memory_space=pltpu.SMEM)
    any_spec = pl.BlockSpec(memory_space=pl.ANY)
    vmem_blk = pl.BlockSpec((chunk * V7X_SUBLANES, V7X_LANES), lambda i, *_: (i, 0))
    if scatter:
        body = functools.partial(_scatter_body, chunk=chunk)
        args = (offsets, bucket, rank, rows, sorted_rows)
        in_specs = [smem_blk, smem_blk, vmem_blk, any_spec]
        out_shape = jax.ShapeDtypeStruct(sorted_rows.shape, sorted_rows.dtype)
        out_specs = any_spec
        aliases = {4: 0}
    else:
        body = functools.partial(_gather_body, chunk=chunk)
        args = (offsets, bucket, rank, sorted_rows)
        in_specs = [smem_blk, smem_blk, any_spec]
        out_shape = jax.ShapeDtypeStruct((n_rows * V7X_SUBLANES, V7X_LANES), sorted_rows.dtype)
        out_specs = vmem_blk
        aliases = {}
    grid_spec = pltpu.PrefetchScalarGridSpec(
        num_scalar_prefetch=1, grid=(n_rows // chunk,), in_specs=in_specs, out_specs=out_specs,
        scratch_shapes=[pltpu.SemaphoreType.DMA])
    return pl.pallas_call(
        body, out_shape=out_shape, grid_spec=grid_spec, input_output_aliases=aliases,
        compiler_params=pltpu.CompilerParams(dimension_semantics=("arbitrary",), has_side_effects=True,
                                             vmem_limit_bytes=V7X_VMEM_LIMIT_BYTES),
        name=name,
    )(*args)


def _moe_body(te1_ref, te2_ref, sval_ref, srow_ref, x_ref, *rest):
    w_refs, (wr_ref, br_ref, y_ref) = rest[:-3], rest[-3:]
    i = pl.program_id(0)
    tm = MOE_TM

    @pl.when(sval_ref[i] == 0)
    def _():
        y_ref[...] = jnp.zeros_like(y_ref)

    def tile(s):
        t = i * MOE_TILES_PER_STEP + s
        rows = pl.ds(s * tm * V7X_SUBLANES, tm * V7X_SUBLANES)
        w1a_ref, w1b_ref, w3a_ref, w3b_ref, w2a_ref, w2b_ref = w_refs[6 * s:6 * s + 6]
        x = _load_token_tiles(x_ref.at[rows]).astype(BF16)
        e1 = te1_ref[t]
        e2 = te2_ref[t]
        grp = e1 >> 3
        logits = _dot(x, wr_ref[...]) + br_ref[...]
        lane = lax.broadcasted_iota(jnp.int32, (tm, V7X_LANES), 1)
        pick = lambda idx: jnp.sum(jnp.where(lane == idx, logits, 0.0), axis=1, keepdims=True)
        gl = jnp.where(lane < N_GROUPS, logits, -jnp.inf)
        gm = jnp.max(gl, axis=1, keepdims=True)
        gz = jnp.sum(jnp.exp(gl - gm), axis=1, keepdims=True)
        g_w = jnp.exp(pick(ROUTER_GROUP_LANE0 + grp) - gm) / gz
        l1 = pick(ROUTER_EXPERT_LANE0 + e1)
        l2 = pick(ROUTER_EXPERT_LANE0 + e2)
        m = jnp.maximum(l1, l2)
        p1 = jnp.exp(l1 - m)
        p2 = jnp.exp(l2 - m)
        c1 = g_w * p1 / (p1 + p2)
        c2 = g_w * p2 / (p1 + p2)

        def expert(w1_ref, w3_ref, w2_ref):
            h = _dot(x, w1_ref[0])
            hid = (h * _sigmoid(h)) * _dot(x, w3_ref[0])
            return _dot(hid.astype(BF16), w2_ref[0])

        _store_token_tiles(y_ref.at[rows],
                           c1 * expert(w1a_ref, w3a_ref, w2a_ref) + c2 * expert(w1b_ref, w3b_ref, w2b_ref))

    @pl.when(sval_ref[i] > 0)
    def _():
        for s in range(MOE_TILES_PER_STEP):
            tile(s)


def _moe_sorted(xs, te1, te2, sval, srow, w1, w3, w2, wr_hi, br):
    d = w1.shape[1]
    n_steps = xs.shape[0] // (MOE_TILES_PER_STEP * MOE_TM * V7X_SUBLANES)
    step_blk = (MOE_TILES_PER_STEP * MOE_TM * V7X_SUBLANES, V7X_LANES)
    const = lambda i, *_: (0, 0)
    w_specs, w_args = [], []
    for s in range(MOE_TILES_PER_STEP):
        wa = lambda i, te1, te2, sval, srow, s=s: (te1[i * MOE_TILES_PER_STEP + s], 0, 0)
        wb = lambda i, te1, te2, sval, srow, s=s: (te2[i * MOE_TILES_PER_STEP + s], 0, 0)
        w_specs += [pl.BlockSpec((1, d, D_EXPERT), wa), pl.BlockSpec((1, d, D_EXPERT), wb),
                    pl.BlockSpec((1, d, D_EXPERT), wa), pl.BlockSpec((1, d, D_EXPERT), wb),
                    pl.BlockSpec((1, D_EXPERT, d), wa), pl.BlockSpec((1, D_EXPERT, d), wb)]
        w_args += [w1, w1, w3, w3, w2, w2]
    grid_spec = pltpu.PrefetchScalarGridSpec(
        num_scalar_prefetch=4, grid=(n_steps,),
        in_specs=[pl.BlockSpec(step_blk, lambda i, te1, te2, sval, srow: (srow[i], 0))] + w_specs
        + [pl.BlockSpec((d, V7X_LANES), const), pl.BlockSpec((1, V7X_LANES), const)],
        out_specs=pl.BlockSpec(step_blk, lambda i, *_: (i, 0)))
    return pl.pallas_call(
        _moe_body, out_shape=jax.ShapeDtypeStruct(xs.shape, F32), grid_spec=grid_spec,
        compiler_params=_cparams("arbitrary"), name="moe_experts",
    )(te1, te2, sval, srow, xs, *w_args, wr_hi, br)


def _moe(h_parts, route_parts, counts, w1, w3, w2, wr_hi, br, gather=True):
    n_rows = [h.shape[0] // V7X_SUBLANES for h in h_parts]
    n_steps = -(-(sum(n_rows) // MOE_TM + N_BUCKETS) // MOE_TILES_PER_STEP)
    n_tiles = n_steps * MOE_TILES_PER_STEP
    cnt = counts[0, :N_BUCKETS].astype(jnp.int32)
    tiles_b = (cnt + MOE_TM - 1) // MOE_TM
    tile_end = jnp.cumsum(tiles_b)
    n_valid = tile_end[-1]
    offsets = jnp.zeros((V7X_LANES,), jnp.int32).at[:N_BUCKETS].set((tile_end - tiles_b) * MOE_TM)
    tile_id = jnp.arange(n_tiles, dtype=jnp.int32)
    trow = jnp.minimum(tile_id, n_valid - 1)
    step_id = jnp.arange(n_steps, dtype=jnp.int32)
    sval = (step_id * MOE_TILES_PER_STEP < n_valid).astype(jnp.int32)
    srow = jnp.minimum(step_id, (n_valid - 1) // MOE_TILES_PER_STEP)
    tbucket = jnp.sum((trow[:, None] >= tile_end[None, :]).astype(jnp.int32), axis=1)
    tbucket = jnp.minimum(tbucket, N_BUCKETS - 1)
    te1 = jnp.asarray(_BUCKET_E1)[tbucket]
    te2 = jnp.asarray(_BUCKET_E2)[tbucket]

    xs = jnp.zeros((n_tiles * MOE_TM * V7X_SUBLANES, V7X_LANES), F32)
    for h, n, (bucket, rank) in zip(h_parts, n_rows, route_parts):
        xs = _permute(offsets, bucket, rank, h, xs, n_rows=n, scatter=True, name="moe_scatter_rows")
    ys = _moe_sorted(xs, te1, te2, sval, srow, w1, w3, w2, wr_hi, br)
    if not gather:
        return ys, offsets
    return [_permute(offsets, bucket, rank, None, ys, n_rows=n, scatter=False, name="moe_gather_rows")
            for n, (bucket, rank) in zip(n_rows, route_parts)]


def _final_body(x_ref, y_ref, mod_ref, g_ref, o_ref):
    x = x_ref[...] + mod_ref[0, 5:6, :] * _load_token_tiles(y_ref)
    o_ref[...] = x * lax.rsqrt(jnp.mean(x * x, axis=-1, keepdims=True) + NORM_EPS) * g_ref[...]


def _final_gather_body(off_ref, b0_ref, r0_ref, bn_ref, rn_ref, x_ref, ys_hbm, mod_ref, g_ref, o_ref, y_ref, sem):
    y, drain = _gathered_rows(off_ref, b0_ref, r0_ref, bn_ref, rn_ref, ys_hbm, y_ref, sem)
    x = x_ref[...] + mod_ref[0, 5:6, :] * y
    o_ref[...] = x * lax.rsqrt(jnp.mean(x * x, axis=-1, keepdims=True) + NORM_EPS) * g_ref[...]
    drain()


def _final_gather(x, ys, offsets, bucket, rank, mod, g, *, seq):
    t, d = x.shape
    tm = min(PROJ_TM, seq)
    tiles_per_seq = seq // tm
    row = lambda i, *_: (i, 0)
    grid_spec = pltpu.PrefetchScalarGridSpec(
        num_scalar_prefetch=1, grid=(t // tm,),
        in_specs=_gather_specs(tm, t // tm) + [
            pl.BlockSpec((tm, d), row), pl.BlockSpec(memory_space=pl.ANY),
            pl.BlockSpec((1, N_MOD, d), lambda i, *_: (i // tiles_per_seq, 0, 0)),
            pl.BlockSpec((1, d), lambda i, *_: (0, 0))],
        out_specs=pl.BlockSpec((tm, d), row), scratch_shapes=_gather_scratch(tm))
    return pl.pallas_call(
        _final_gather_body, out_shape=jax.ShapeDtypeStruct((t, d), F32), grid_spec=grid_spec,
        compiler_params=_cparams("arbitrary"), name="final_norm_gather",
    )(offsets, bucket, rank, bucket, rank, x, ys, mod, g.reshape(1, d))


def _final(x, y, mod, g, *, seq):
    t, d = x.shape
    tm = min(PROJ_TM, seq)
    tiles_per_seq = seq // tm
    row = lambda i: (i, 0)
    return pl.pallas_call(
        _final_body, out_shape=jax.ShapeDtypeStruct((t, d), F32), grid=(t // tm,),
        in_specs=[pl.BlockSpec((tm, d), row), pl.BlockSpec((tm * V7X_SUBLANES, V7X_LANES), row),
                  pl.BlockSpec((1, N_MOD, d), lambda i: (i // tiles_per_seq, 0, 0)),
                  pl.BlockSpec((1, d), lambda i: (0, 0))],
        out_specs=pl.BlockSpec((tm, d), row),
        compiler_params=_cparams("arbitrary"), name="final_norm",
    )(x, y, mod, g.reshape(1, d))


def kernel(x, c, ctx, c_ctx, w_mod, b_mod, norm_mix, norm_ffn, norm_final, even_w_in, even_w_out, even_rpb,
           even_dw, even_dw_b, even_ln_g, even_ln_b, odd_w_in, odd_w_out, odd_conv, odd_lambda, odd_subln,
           moe_wg, moe_bg, moe_we, moe_be, moe_w1, moe_w3, moe_w2):
    bsz, n_tok, d = x.shape
    ctx_len = ctx.shape[1]
    depth = w_mod.shape[0]
    assert depth == 2 and d == D_MODEL
    xs = x.reshape(bsz * n_tok, d)
    cs = ctx.reshape(bsz * ctx_len, d)

    mod_rows = -(-(bsz + 1) // 8) * 8
    cc = jnp.zeros((mod_rows, d), F32).at[:bsz].set(c).at[bsz].set(c_ctx)
    mod = _modulation(cc, w_mod, b_mod).reshape(depth, mod_rows, N_MOD, d)
    bf = lambda a: a.astype(BF16)
    ew = _expert_weights_bf16
    zero_counts = jnp.zeros((1, V7X_LANES), F32)

    l = 0
    w_in = bf(even_w_in[0])
    w_out = bf(even_w_out[0])
    px = _proj(xs, mod[l], norm_mix[l], w_in, seq=n_tok, mod_row=None, q_seg=0, name="proj_even_x")
    pc = _proj(cs, mod[l], norm_mix[l], w_in, seq=None, mod_row=bsz, q_seg=0, name="proj_even_c")
    a_x = _natten(px, pc, even_rpb[0], bsz=bsz, n_tok=n_tok, ctx_len=ctx_len)
    a_c = _ctxattn(pc, bsz=bsz, ctx_len=ctx_len)
    b_x = _conformer(px, even_dw[0], even_dw_b[0], even_ln_g[0], even_ln_b[0], bsz=bsz, seq=n_tok)
    b_c = _conformer(pc, even_dw[0], even_dw_b[0], even_ln_g[0], even_ln_b[0], bsz=bsz, seq=ctx_len)
    wr_hi, wr_lo, br = _router_weights(moe_wg[l], moe_bg[l], moe_we[l], moe_be[l])
    x1, hx, bkt_x, rnk_x, counts = _oproj(a_x, b_x, xs, mod[l], norm_ffn[l], w_out, wr_hi, wr_lo, br, zero_counts,
                                          seq=n_tok, mod_row=None, name="oproj_even_x")
    c1, hc, bkt_c, rnk_c, counts = _oproj(a_c, b_c, cs, mod[l], norm_ffn[l], w_out, wr_hi, wr_lo, br, counts,
                                          seq=None, mod_row=bsz, name="oproj_even_c")
    route_x = (bkt_x.reshape(-1), rnk_x.reshape(-1))
    route_c = (bkt_c.reshape(-1), rnk_c.reshape(-1))
    ys, offsets = _moe([hx, hc], [route_x, route_c], counts, ew(moe_w1, l), ew(moe_w3, l), ew(moe_w2, l), wr_hi, br,
                       gather=False)

    l = 1
    lam_init = 0.8 - 0.6 * math.exp(-0.3 * l)
    w_in = bf(odd_w_in[0])
    w_out = bf(odd_w_out[0])
    rope = _rope_tables(n_tok)
    px, x2 = _proj(x1, mod[l], norm_mix[l], w_in, seq=n_tok, mod_row=None, res=(ys, offsets, *route_x, mod[l - 1]),
                   res_idx=5, rope=rope, rope_segs=(3, 4), q_seg=3, write_x=True, name="proj_odd_x")
    pc = _proj(c1, mod[l], norm_mix[l], w_in[:, 4 * SEG:], seq=None, mod_row=bsz,
               res=(ys, offsets, *route_c, mod[l - 1]), res_idx=5, name="proj_odd_c")
    s_x = _shortconv(px, odd_conv[0], bsz=bsz, seq=n_tok)
    d_x = _diffattn(px, pc, odd_lambda[0], odd_subln[0], lam_init, bsz=bsz, n_tok=n_tok, ctx_len=ctx_len)
    wr_hi, wr_lo, br = _router_weights(moe_wg[l], moe_bg[l], moe_we[l], moe_be[l])
    x3, hx, bkt_x, rnk_x, counts = _oproj(s_x, d_x, x2, mod[l], norm_ffn[l], w_out, wr_hi, wr_lo, br, zero_counts,
                                          seq=n_tok, mod_row=None, name="oproj_odd_x")
    bkt_x, rnk_x = bkt_x.reshape(-1), rnk_x.reshape(-1)
    ys, offsets = _moe([hx], [(bkt_x, rnk_x)], counts, ew(moe_w1, l), ew(moe_w3, l), ew(moe_w2, l), wr_hi, br,
                       gather=False)
    out = _final_gather(x3, ys, offsets, bkt_x, rnk_x, mod[l], norm_final, seq=n_tok)
    return out.reshape(bsz, n_tok, d)
```

```python
import functools
import math

import numpy as np
import jax
import jax.numpy as jnp
from jax import lax
from jax.experimental import pallas as pl
from jax.experimental.pallas import tpu as pltpu

F32 = jnp.float32
BF16 = jnp.bfloat16

D_MODEL = 1024
GRID_W = 64
N_MOD = 6
NORM_EPS = 1e-6
NEG_INF = -1e30
ROPE_BASE = 10000.0
SEG = 512
A_HEAD_DIM = 64
A_HEADS = 8
WIN_ROWS_MAX = 8
WIN_COLS = 16
B_CONV = 31
C_CONV = 3
D_QK_DIM = 64
D_V_DIM = 128
D_HEADS = 4
N_GROUPS = 4
EXPERTS_PER_GROUP = 8
N_EXPERTS = 32
D_EXPERT = 256
LOG2E = math.log2(math.e)
QUERY_SCALE = A_HEAD_DIM ** -0.5 * LOG2E
PAIRS_PER_GROUP = EXPERTS_PER_GROUP * (EXPERTS_PER_GROUP - 1) // 2
N_BUCKETS = N_GROUPS * PAIRS_PER_GROUP

V7X_LANES = 128
V7X_SUBLANES = 8
V7X_VMEM_LIMIT_BYTES = 56 * 1024 * 1024

PROJ_TM = 512
ATT_TQ = 256
NAT_TILES_PER_STEP = 4
DIFF_TQ = 1024
NAT_KROWS = 12
CONV_CHUNK = 64
CONV_PAD = 16
MOE_TM = 256
PERM_CHUNK = 1024
ROUTER_GROUP_LANE0 = 0
ROUTER_EXPERT_LANE0 = 8

_PAIR_LO = np.array([i for i in range(8) for j in range(i + 1, 8)], np.int32)
_PAIR_HI = np.array([j for i in range(8) for j in range(i + 1, 8)], np.int32)
_BUCKET_E1 = np.concatenate([g * 8 + _PAIR_LO for g in range(N_GROUPS)]).astype(np.int32)
_BUCKET_E2 = np.concatenate([g * 8 + _PAIR_HI for g in range(N_GROUPS)]).astype(np.int32)


def _cparams(*sem):
    return pltpu.CompilerParams(dimension_semantics=tuple(sem), vmem_limit_bytes=V7X_VMEM_LIMIT_BYTES)


def _dot(a, b):
    return jnp.dot(a, b, preferred_element_type=F32)


def _dot_t(a, b):
    return lax.dot_general(a, b, (((1,), (1,)), ((), ())), preferred_element_type=F32)


def _split_bf16(a):
    hi = a.astype(BF16)
    lo = (a - hi.astype(F32)).astype(BF16)
    return hi, lo


def _dot3(a_hi, a_lo, b_hi, b_lo):
    return _dot(a_hi, b_hi) + _dot(a_lo, b_hi) + _dot(a_hi, b_lo)


def _sigmoid(x):
    return 1.0 / (1.0 + jnp.exp(-x))


def _load_token_tiles(ref):
    tm = ref.shape[0] // V7X_SUBLANES
    return jnp.concatenate([ref[pl.ds(c, tm, stride=V7X_SUBLANES), :] for c in range(V7X_SUBLANES)], axis=1)


def _store_token_tiles(ref, v):
    tm = v.shape[0]
    for c in range(V7X_SUBLANES):
        ref[pl.ds(c, tm, stride=V7X_SUBLANES), :] = v[:, c * V7X_LANES:(c + 1) * V7X_LANES]


def _rmsnorm_mod(x, g, scale, shift):
    y = x * lax.rsqrt(jnp.mean(x * x, axis=-1, keepdims=True) + NORM_EPS)
    return (y * g) * (1.0 + scale) + shift


def _cast_body(*refs):
    *w_refs, o_ref = refs
    n = w_refs[0].shape[-1]
    for j, w_ref in enumerate(w_refs):
        o_ref[:, :, j * n:(j + 1) * n] = w_ref[0].astype(BF16)


def _expert_weights_bf16(ws, layer):
    _, n_exp, k, n = ws[0].shape
    eb = 4
    return pl.pallas_call(
        _cast_body, out_shape=jax.ShapeDtypeStruct((n_exp, k, len(ws) * n), BF16), grid=(n_exp // eb,),
        in_specs=[pl.BlockSpec((1, eb, k, n), lambda e: (layer, e, 0, 0))] * len(ws),
        out_specs=pl.BlockSpec((eb, k, len(ws) * n), lambda e: (e, 0, 0)),
        compiler_params=_cparams("arbitrary"), name="expert_weights_bf16",
    )(*ws)


def _mod_body(c_ref, w_ref, b_ref, o_ref):
    c = c_ref[...]
    s = c * _sigmoid(c)
    s_hi, s_lo = _split_bf16(s)
    w_hi, w_lo = _split_bf16(w_ref[0])
    o_ref[0] = _dot3(s_hi, s_lo, w_hi, w_lo) + b_ref[0]


def _modulation(cc, w_mod, b_mod):
    depth, d, n = w_mod.shape
    rows = cc.shape[0]
    tn = 1536
    return pl.pallas_call(
        _mod_body,
        out_shape=jax.ShapeDtypeStruct((depth, rows, n), F32),
        grid=(depth, n // tn),
        in_specs=[pl.BlockSpec((rows, d), lambda l, j: (0, 0)),
                  pl.BlockSpec((1, d, tn), lambda l, j: (l, 0, j)),
                  pl.BlockSpec((1, 1, tn), lambda l, j: (l, 0, j))],
        out_specs=pl.BlockSpec((1, rows, tn), lambda l, j: (l, 0, j)),
        compiler_params=_cparams("arbitrary", "arbitrary"),
        name="modulation",
    )(cc, w_mod, b_mod.reshape(depth, 1, n))


def _rope(v, cos, sin, lane):
    up = pltpu.roll(v, V7X_LANES - 16, axis=1)
    dn = pltpu.roll(v, 16, axis=1)
    sw = jnp.where((lane & 31) < 16, up, dn)
    return v * cos + sw * sin


def _proj_body(*refs, n_seg, res_idx, shift_idx, rope_segs, q_seg, write_x):
    refs = list(refs)
    x_ref = refs.pop(0)
    y_ref = refs.pop(0) if res_idx is not None else None
    rmod_ref = refs.pop(0) if res_idx is not None else None
    mod_ref = refs.pop(0)
    g_ref = refs.pop(0)
    w_ref = refs.pop(0)
    cos_ref = sin_ref = None
    if rope_segs:
        cos_ref = refs.pop(0)
        sin_ref = refs.pop(0)
    out_ref = refs.pop(0)
    xo_ref = refs.pop(0) if write_x else None

    x = x_ref[...]
    if y_ref is not None:
        x = x + rmod_ref[0, res_idx:res_idx + 1, :] * _load_token_tiles(y_ref)
        if write_x:
            xo_ref[...] = x
    h = _rmsnorm_mod(x, g_ref[...], mod_ref[0, shift_idx + 1:shift_idx + 2, :],
                     mod_ref[0, shift_idx:shift_idx + 1, :]).astype(BF16)
    for s in range(n_seg):
        o = _dot(h, w_ref[:, s * SEG:(s + 1) * SEG])
        if s == q_seg:
            o = o * QUERY_SCALE
        if s in rope_segs:
            lane = lax.broadcasted_iota(jnp.int32, (o.shape[0], V7X_LANES), 1)
            cos = cos_ref[...]
            sin = sin_ref[...]
            o = jnp.concatenate(
                [_rope(o[:, c * V7X_LANES:(c + 1) * V7X_LANES], cos, sin, lane) for c in range(SEG // V7X_LANES)],
                axis=1)
        out_ref[:, s * SEG:(s + 1) * SEG] = o.astype(BF16)


def _proj(x, mod, g, w, *, seq, mod_row, y=None, res_mod=None, res_idx=None, shift_idx=0, rope=None, rope_segs=(),
          q_seg=None, write_x=False, name="proj"):
    t, d = x.shape
    n = w.shape[1]
    tm = min(PROJ_TM, t if seq is None else seq)
    tiles_per_seq = None if seq is None else seq // tm
    if seq is None:
        mod_map = lambda i: (mod_row, 0, 0)
    else:
        mod_map = lambda i: (i // tiles_per_seq, 0, 0)
    args = [x]
    specs = [pl.BlockSpec((tm, d), lambda i: (i, 0))]
    if y is not None:
        args += [y, res_mod]
        specs += [pl.BlockSpec((tm * V7X_SUBLANES, V7X_LANES), lambda i: (i, 0)),
                  pl.BlockSpec((1, N_MOD, d), mod_map)]
    args += [mod, g.reshape(1, d), w]
    specs += [pl.BlockSpec((1, N_MOD, d), mod_map), pl.BlockSpec((1, d), lambda i: (0, 0)),
              pl.BlockSpec((d, n), lambda i: (0, 0))]
    if rope_segs:
        args += [rope[0], rope[1]]
        specs += [pl.BlockSpec((tm, V7X_LANES), lambda i: (i % tiles_per_seq, 0))] * 2
    out_shape = [jax.ShapeDtypeStruct((t, n), BF16)]
    out_specs = [pl.BlockSpec((tm, n), lambda i: (i, 0))]
    if write_x:
        out_shape.append(jax.ShapeDtypeStruct((t, d), F32))
        out_specs.append(pl.BlockSpec((tm, d), lambda i: (i, 0)))
    body = functools.partial(_proj_body, n_seg=n // SEG, res_idx=res_idx if y is not None else None,
                             shift_idx=shift_idx, rope_segs=tuple(rope_segs), q_seg=q_seg, write_x=write_x)
    outs = pl.pallas_call(
        body, out_shape=out_shape, grid=(t // tm,), in_specs=specs, out_specs=out_specs,
        compiler_params=_cparams("arbitrary"), name=name,
    )(*args)
    return outs if write_x else outs[0]


def _rope_tables(n_tok):
    quarter = D_QK_DIM // 4
    t = np.arange(n_tok)
    row = (t // GRID_W).astype(np.float32)
    col = (t % GRID_W).astype(np.float32)
    inv = jnp.power(ROPE_BASE, -jnp.arange(quarter, dtype=F32) / quarter)
    ar = jnp.asarray(row)[:, None] * inv
    ac = jnp.asarray(col)[:, None] * inv
    cos64 = jnp.concatenate([jnp.cos(ar), jnp.cos(ar), jnp.cos(ac), jnp.cos(ac)], axis=-1)
    sin64 = jnp.concatenate([-jnp.sin(ar), jnp.sin(ar), -jnp.sin(ac), jnp.sin(ac)], axis=-1)
    return jnp.tile(cos64, (1, V7X_LANES // D_QK_DIM)), jnp.tile(sin64, (1, V7X_LANES // D_QK_DIM))


def _stack_halves(q):
    lane = lax.broadcasted_iota(jnp.int32, q.shape, 1)
    zero = jnp.zeros_like(q)
    return jnp.concatenate([jnp.where(lane < 64, q, zero), jnp.where(lane >= 64, q, zero)], axis=0)


def _attend(scores, values):
    m = functools.reduce(jnp.maximum, [jnp.max(s, axis=1, keepdims=True) for s in scores])
    acc = None
    for s, v in zip(scores, values):
        p = jnp.exp2(s - m).astype(BF16)
        lane = lax.broadcasted_iota(jnp.int32, v.shape, 1)
        ones_col = jnp.where(lane == 0, 1.0, 0.0).astype(BF16)
        o = _dot(p, jnp.concatenate([v, ones_col], axis=1))
        acc = o if acc is None else acc + o
    return acc[:, :V7X_LANES], acc[:, V7X_LANES:V7X_LANES + 1]


def _natten_body(ks_ref, pid_ref, q_ref, k_ref, v_ref, kc_ref, vc_ref, bias_ref, o_ref):
    nk = NAT_KROWS * GRID_W
    tq = ATT_TQ
    lane = lax.broadcasted_iota(jnp.int32, (tq, V7X_LANES), 1)
    for s in range(q_ref.shape[0] // tq):
        j = pl.program_id(1) * (q_ref.shape[0] // tq) + s
        k0 = pl.multiple_of(ks_ref[j] * GRID_W, GRID_W)
        p = pid_ref[j]
        rows = pl.ds(s * tq, tq)
        for hp in range(A_HEADS // 2):
            cs = slice(hp * V7X_LANES, (hp + 1) * V7X_LANES)
            q2 = _stack_halves(q_ref[rows, cs])
            s_c = _dot_t(q2, kc_ref[:, cs])
            s_l = _dot_t(q2, k_ref[pl.ds(k0, nk), cs])
            bias = jnp.concatenate([bias_ref[p, 2 * hp], bias_ref[p, 2 * hp + 1]], axis=0).astype(F32)
            o2, l = _attend([s_c, s_l + bias], [vc_ref[:, cs], v_ref[pl.ds(k0, nk), cs]])
            o2 = o2 / l
            o_ref[rows, cs] = jnp.where(lane < 64, o2[:tq], o2[tq:]).astype(BF16)


def _natten_tables(rows):
    q_rows = ATT_TQ // GRID_W
    nqb = rows // q_rows
    win_r = min(WIN_ROWS_MAX, rows)
    col = np.arange(GRID_W)
    c_start = np.clip(col - WIN_COLS // 2, 0, GRID_W - WIN_COLS)
    col_ok = (col[None, :] >= c_start[:, None]) & (col[None, :] < c_start[:, None] + WIN_COLS)
    dc = np.clip(col[None, :] - col[:, None], -(WIN_COLS - 1), WIN_COLS - 1) + WIN_COLS - 1
    ks_list, pats, pid = [], [], []
    for j in range(nqb):
        r0 = j * q_rows
        ks = int(np.clip(r0 - win_r // 2, 0, rows - NAT_KROWS))
        qr = r0 + np.arange(q_rows)
        rs = np.clip(qr - win_r // 2, 0, rows - win_r)
        kr = ks + np.arange(NAT_KROWS)
        row_ok = (kr[None, :] >= rs[:, None]) & (kr[None, :] < rs[:, None] + win_r)
        dr = np.clip(kr[None, :] - qr[:, None] + WIN_ROWS_MAX - 1, 0, 2 * WIN_ROWS_MAX - 2)
        key = (row_ok.tobytes(), (dr * row_ok).tobytes())
        for n, (k_, *_rest) in enumerate(pats):
            if k_ == key:
                pid.append(n)
                break
        else:
            pid.append(len(pats))
            pats.append((key, row_ok, dr))
        ks_list.append(ks)
    row_ok = np.stack([p_[1] for p_ in pats])
    dr = np.stack([p_[2] for p_ in pats])
    return np.array(ks_list, np.int32), np.array(pid, np.int32), row_ok, dr, col_ok, dc


def _natten_bias(rpb, row_ok, dr, col_ok, dc):
    n_pat, q_rows, k_rows = dr.shape
    n_dr, n_dc = rpb.shape[1], rpb.shape[2]
    oh_r = np.zeros((n_pat * q_rows * k_rows, n_dr), np.float32)
    oh_r[np.arange(oh_r.shape[0]), dr.reshape(-1)] = 1.0
    oh_c = np.zeros((n_dc, GRID_W * GRID_W), np.float32)
    oh_c[dc.reshape(-1), np.arange(GRID_W * GRID_W)] = 1.0
    hi = lax.Precision.HIGHEST
    t1 = jnp.einsum("mr,hrc->hmc", jnp.asarray(oh_r), rpb, precision=hi)
    t2 = jnp.einsum("hmc,cn->hmn", t1, jnp.asarray(oh_c), precision=hi)
    t2 = t2.reshape(rpb.shape[0], n_pat, q_rows, k_rows, GRID_W, GRID_W)
    valid = row_ok[None, :, :, :, None, None] & col_ok[None, None, None, None, :, :]
    bias = jnp.where(valid, t2 * LOG2E, NEG_INF).transpose(1, 0, 2, 4, 3, 5)
    return bias.reshape(n_pat, rpb.shape[0], q_rows * GRID_W, k_rows * GRID_W).astype(BF16)


def _natten(px, pc, rpb, *, bsz, n_tok, ctx_len):
    rows = n_tok // GRID_W
    assert rows >= NAT_KROWS and n_tok % ATT_TQ == 0
    ks, pid, row_ok, dr, col_ok, dc = _natten_tables(rows)
    bias = _natten_bias(rpb, row_ok, dr, col_ok, dc)
    n_pat = bias.shape[0]
    q_blk = ATT_TQ * NAT_TILES_PER_STEP
    assert n_tok % q_blk == 0
    nqb = n_tok // q_blk
    nk = NAT_KROWS * GRID_W
    grid_spec = pltpu.PrefetchScalarGridSpec(
        num_scalar_prefetch=2, grid=(bsz, nqb),
        in_specs=[pl.BlockSpec((q_blk, SEG), lambda b, j, *_: (b * nqb + j, 0)),
                  pl.BlockSpec((n_tok, SEG), lambda b, j, *_: (b, 1)),
                  pl.BlockSpec((n_tok, SEG), lambda b, j, *_: (b, 2)),
                  pl.BlockSpec((ctx_len, SEG), lambda b, j, *_: (b, 1)),
                  pl.BlockSpec((ctx_len, SEG), lambda b, j, *_: (b, 2)),
                  pl.BlockSpec((n_pat, A_HEADS, ATT_TQ, nk), lambda b, j, *_: (0, 0, 0, 0))],
        out_specs=pl.BlockSpec((q_blk, SEG), lambda b, j, *_: (b * nqb + j, 0)))
    return pl.pallas_call(
        _natten_body, out_shape=jax.ShapeDtypeStruct((bsz * n_tok, SEG), BF16), grid_spec=grid_spec,
        compiler_params=_cparams("arbitrary", "arbitrary"), name="natten",
    )(jnp.asarray(ks), jnp.asarray(pid), px, px, px, pc, pc, bias)


def _ctxattn_body(q_ref, k_ref, v_ref, o_ref):
    tq = q_ref.shape[0]
    lane = lax.broadcasted_iota(jnp.int32, (tq, V7X_LANES), 1)
    for hp in range(A_HEADS // 2):
        cs = slice(hp * V7X_LANES, (hp + 1) * V7X_LANES)
        q2 = _stack_halves(q_ref[:, cs])
        o2, l = _attend([_dot_t(q2, k_ref[:, cs])], [v_ref[:, cs]])
        o2 = o2 / l
        o_ref[:, cs] = jnp.where(lane < 64, o2[:tq], o2[tq:]).astype(BF16)


def _ctxattn(pc, *, bsz, ctx_len):
    return pl.pallas_call(
        _ctxattn_body, out_shape=jax.ShapeDtypeStruct((bsz * ctx_len, SEG), BF16), grid=(bsz,),
        in_specs=[pl.BlockSpec((ctx_len, SEG), lambda b: (b, 0)),
                  pl.BlockSpec((ctx_len, SEG), lambda b: (b, 1)),
                  pl.BlockSpec((ctx_len, SEG), lambda b: (b, 2))],
        out_specs=pl.BlockSpec((ctx_len, SEG), lambda b: (b, 0)),
        compiler_params=_cparams("arbitrary"), name="ctxattn",
    )(pc, pc, pc)


def _conv_accumulate(z_ref, w_ref, t0, taps):
    base = CONV_PAD - taps // 2
    blocks = []
    for c in range(SEG // V7X_LANES):
        cs = slice(c * V7X_LANES, (c + 1) * V7X_LANES)
        win = z_ref[pl.ds(t0, CONV_CHUNK + 2 * CONV_PAD), cs]
        acc = None
        for b in range(V7X_SUBLANES):
            ks = [k for k in range(taps) if (base + k) % V7X_SUBLANES == b]
            if not ks:
                continue
            shifted = win if b == 0 else pltpu.roll(win, win.shape[0] - b, axis=0)
            for k in ks:
                a0 = (base + k) // V7X_SUBLANES * V7X_SUBLANES
                term = shifted[a0:a0 + CONV_CHUNK, :] * w_ref[k:k + 1, cs]
                acc = term if acc is None else acc + term
        blocks.append(acc)
    return jnp.concatenate(blocks, axis=1)


def _conformer_body(u_ref, g_ref, w_ref, b_ref, lg_ref, lb_ref, o_ref, z_ref):
    n = u_ref.shape[0]
    halo = jnp.zeros((CONV_PAD, SEG), F32)
    z_ref[pl.ds(0, CONV_PAD), :] = halo
    z_ref[pl.ds(CONV_PAD + n, CONV_PAD), :] = halo
    z_ref[pl.ds(CONV_PAD, n), :] = u_ref[...].astype(F32) * _sigmoid(g_ref[...].astype(F32))

    def chunk(i, carry):
        t0 = pl.multiple_of(i * CONV_CHUNK, CONV_CHUNK)
        y = _conv_accumulate(z_ref, w_ref, t0, B_CONV) + b_ref[...]
        mu = jnp.mean(y, axis=-1, keepdims=True)
        yc = y - mu
        var = jnp.mean(yc * yc, axis=-1, keepdims=True)
        y = yc * lax.rsqrt(var + NORM_EPS) * lg_ref[...] + lb_ref[...]
        o_ref[pl.ds(t0, CONV_CHUNK), :] = (y * _sigmoid(y)).astype(BF16)
        return carry

    lax.fori_loop(0, n // CONV_CHUNK, chunk, 0)


def _conformer(p, dw, dw_b, ln_g, ln_b, *, bsz, seq):
    vec = lambda a: a.reshape(1, SEG)
    return pl.pallas_call(
        _conformer_body, out_shape=jax.ShapeDtypeStruct((bsz * seq, SEG), BF16), grid=(bsz,),
        in_specs=[pl.BlockSpec((seq, SEG), lambda b: (b, 3)),
                  pl.BlockSpec((seq, SEG), lambda b: (b, 4)),
                  pl.BlockSpec((B_CONV, SEG), lambda b: (0, 0))] + [pl.BlockSpec((1, SEG), lambda b: (0, 0))] * 3,
        out_specs=pl.BlockSpec((seq, SEG), lambda b: (b, 0)),
        scratch_shapes=[pltpu.VMEM((seq + 2 * CONV_PAD, SEG), F32)],
        compiler_params=_cparams("arbitrary"), name="conformer_conv",
    )(p, p, dw, vec(dw_b), vec(ln_g), vec(ln_b))


def _shortconv_body(bg_ref, cg_ref, u_ref, w_ref, o_ref, z_ref):
    n = u_ref.shape[0]
    halo = jnp.zeros((CONV_PAD, SEG), F32)
    z_ref[pl.ds(0, CONV_PAD), :] = halo
    z_ref[pl.ds(CONV_PAD + n, CONV_PAD), :] = halo
    z_ref[pl.ds(CONV_PAD, n), :] = cg_ref[...].astype(F32) * u_ref[...].astype(F32)

    def chunk(i, carry):
        t0 = pl.multiple_of(i * CONV_CHUNK, CONV_CHUNK)
        y = _conv_accumulate(z_ref, w_ref, t0, C_CONV)
        o_ref[pl.ds(t0, CONV_CHUNK), :] = (bg_ref[pl.ds(t0, CONV_CHUNK), :].astype(F32) * y).astype(BF16)
        return carry

    lax.fori_loop(0, n // CONV_CHUNK, chunk, 0)


def _shortconv(p, conv_w, *, bsz, seq):
    return pl.pallas_call(
        _shortconv_body, out_shape=jax.ShapeDtypeStruct((bsz * seq, SEG), BF16), grid=(bsz,),
        in_specs=[pl.BlockSpec((seq, SEG), lambda b: (b, 0)),
                  pl.BlockSpec((seq, SEG), lambda b: (b, 1)),
                  pl.BlockSpec((seq, SEG), lambda b: (b, 2)),
                  pl.BlockSpec((C_CONV, SEG), lambda b: (0, 0))],
        out_specs=pl.BlockSpec((seq, SEG), lambda b: (b, 0)),
        scratch_shapes=[pltpu.VMEM((seq + 2 * CONV_PAD, SEG), F32)],
        compiler_params=_cparams("arbitrary"), name="short_conv",
    )(p, p, p, conv_w)


def _diffattn_body(q_ref, k_ref, v_ref, kc_ref, vc_ref, lam_ref, g_ref, o_ref, *, lam_init):
    tq = ATT_TQ
    lp = lam_ref[...]
    lam = (jnp.exp(jnp.sum(lp[0:1, :] * lp[1:2, :], axis=1, keepdims=True))
           - jnp.exp(jnp.sum(lp[2:3, :] * lp[3:4, :], axis=1, keepdims=True)) + lam_init)
    for s in range(q_ref.shape[0] // tq):
        rows = pl.ds(s * tq, tq)
        for h in range(D_HEADS):
            cs = slice(h * V7X_LANES, (h + 1) * V7X_LANES)
            q2 = _stack_halves(q_ref[rows, cs])
            o2, l = _attend([_dot_t(q2, kc_ref[:, cs]), _dot_t(q2, k_ref[:, cs])], [vc_ref[:, cs], v_ref[:, cs]])
            o2 = o2 / l
            o = o2[:tq] - lam * o2[tq:]
            o = o * lax.rsqrt(jnp.mean(o * o, axis=-1, keepdims=True) + NORM_EPS) * g_ref[...]
            o_ref[rows, cs] = (o * (1.0 - lam_init)).astype(BF16)


def _diffattn(px, pc, lam_p, subln, lam_init, *, bsz, n_tok, ctx_len):
    nq = n_tok // DIFF_TQ
    body = functools.partial(_diffattn_body, lam_init=lam_init)
    return pl.pallas_call(
        body, out_shape=jax.ShapeDtypeStruct((bsz * n_tok, SEG), BF16), grid=(bsz, nq),
        in_specs=[pl.BlockSpec((DIFF_TQ, SEG), lambda b, j: (b * nq + j, 3)),
                  pl.BlockSpec((n_tok, SEG), lambda b, j: (b, 4)),
                  pl.BlockSpec((n_tok, SEG), lambda b, j: (b, 5)),
                  pl.BlockSpec((ctx_len, SEG), lambda b, j: (b, 0)),
                  pl.BlockSpec((ctx_len, SEG), lambda b, j: (b, 1)),
                  pl.BlockSpec((4, D_QK_DIM), lambda b, j: (0, 0)),
                  pl.BlockSpec((1, D_V_DIM), lambda b, j: (0, 0))],
        out_specs=pl.BlockSpec((DIFF_TQ, SEG), lambda b, j: (b * nq + j, 0)),
        compiler_params=_cparams("arbitrary", "arbitrary"), name="diff_attn",
    )(px, px, px, pc, pc, lam_p, subln.reshape(1, D_V_DIM))


def _oproj_body(a_ref, b_ref, x_ref, mod_ref, g_ref, wo_ref, wr_ref, br_ref, cin_ref,
                x1_ref, h_ref, bucket_ref, rank_ref, cout_ref, carry_ref, before_ref):
    i = pl.program_id(0)
    tm = x_ref.shape[0]

    @pl.when(i == 0)
    def _():
        carry_ref[...] = cin_ref[...]
        r_i = lax.broadcasted_iota(jnp.int32, (tm, tm), 0)
        c_i = lax.broadcasted_iota(jnp.int32, (tm, tm), 1)
        before_ref[...] = jnp.where(r_i > c_i, 1.0, 0.0).astype(BF16)

    y = _dot(a_ref[...], wo_ref[0:SEG, :]) + _dot(b_ref[...], wo_ref[SEG:2 * SEG, :])
    x1 = x_ref[...] + mod_ref[0, 2:3, :] * y
    x1_ref[...] = x1
    h = _rmsnorm_mod(x1, g_ref[...], mod_ref[0, 4:5, :], mod_ref[0, 3:4, :])
    _store_token_tiles(h_ref, h)
    logits = _dot(h.astype(BF16), wr_ref[...]) + br_ref[...]
    lane = lax.broadcasted_iota(jnp.int32, (tm, V7X_LANES), 1)

    def first_max(v):
        m = jnp.max(v, axis=1, keepdims=True)
        return jnp.min(jnp.where(v == m, lane, jnp.int32(V7X_LANES)), axis=1, keepdims=True)

    g_sel = first_max(jnp.where(lane < N_GROUPS, logits, -jnp.inf))
    e_lane0 = ROUTER_EXPERT_LANE0 + EXPERTS_PER_GROUP * g_sel
    in_group = (lane >= e_lane0) & (lane < e_lane0 + EXPERTS_PER_GROUP)
    el = jnp.where(in_group, logits, -jnp.inf)
    i1 = first_max(el)
    i2 = first_max(jnp.where(lane == i1, -jnp.inf, el))
    lo = jnp.minimum(i1, i2) - e_lane0
    hi = jnp.maximum(i1, i2) - e_lane0
    bucket = g_sel * PAIRS_PER_GROUP + ((lo * (2 * EXPERTS_PER_GROUP - 1 - lo)) >> 1) + (hi - lo - 1)
    bucket_ref[...] = bucket
    oh = jnp.where(lane == bucket, 1.0, 0.0)
    cum = _dot(before_ref[...], oh.astype(BF16)) + carry_ref[...]
    rank = jnp.sum(oh * cum, axis=1, keepdims=True)
    rank_ref[...] = rank.astype(jnp.int32)
    carry_ref[...] = carry_ref[...] + jnp.sum(oh, axis=0, keepdims=True)

    @pl.when(i == pl.num_programs(0) - 1)
    def _():
        cout_ref[...] = carry_ref[...]


def _oproj(a, b, x, mod, g, wo, wr, br, counts_in, *, seq, mod_row, name="oproj"):
    t, d = x.shape
    tm = min(PROJ_TM, t if seq is None else seq)
    if seq is None:
        mod_map = lambda i: (mod_row, 0, 0)
    else:
        tiles_per_seq = seq // tm
        mod_map = lambda i: (i // tiles_per_seq, 0, 0)
    row = lambda i: (i, 0)
    const = lambda i: (0, 0)
    return pl.pallas_call(
        _oproj_body,
        out_shape=[jax.ShapeDtypeStruct((t, d), F32), jax.ShapeDtypeStruct((t * V7X_SUBLANES, V7X_LANES), F32),
                   jax.ShapeDtypeStruct((t, 1), jnp.int32), jax.ShapeDtypeStruct((t, 1), jnp.int32),
                   jax.ShapeDtypeStruct((1, V7X_LANES), F32)],
        grid=(t // tm,),
        in_specs=[pl.BlockSpec((tm, SEG), row), pl.BlockSpec((tm, SEG), row), pl.BlockSpec((tm, d), row),
                  pl.BlockSpec((1, N_MOD, d), mod_map), pl.BlockSpec((1, d), const),
                  pl.BlockSpec((2 * SEG, d), const), pl.BlockSpec((d, V7X_LANES), const),
                  pl.BlockSpec((1, V7X_LANES), const), pl.BlockSpec((1, V7X_LANES), const)],
        out_specs=[pl.BlockSpec((tm, d), row), pl.BlockSpec((tm * V7X_SUBLANES, V7X_LANES), row),
                   pl.BlockSpec((tm, 1), row),
                   pl.BlockSpec((tm, 1), row), pl.BlockSpec((1, V7X_LANES), const)],
        scratch_shapes=[pltpu.VMEM((1, V7X_LANES), F32), pltpu.VMEM((tm, tm), BF16)],
        compiler_params=_cparams("arbitrary"), name=name,
    )(a, b, x, mod, g.reshape(1, d), wo, wr, br, counts_in)


def _router_weights(wg, bg, we, be):
    d = wg.shape[0]
    w = jnp.zeros((d, V7X_LANES), F32)
    w = w.at[:, ROUTER_GROUP_LANE0:ROUTER_GROUP_LANE0 + N_GROUPS].set(wg)
    w = w.at[:, ROUTER_EXPERT_LANE0:ROUTER_EXPERT_LANE0 + N_EXPERTS].set(we)
    b = jnp.zeros((1, V7X_LANES), F32)
    b = b.at[0, ROUTER_GROUP_LANE0:ROUTER_GROUP_LANE0 + N_GROUPS].set(bg)
    b = b.at[0, ROUTER_EXPERT_LANE0:ROUTER_EXPERT_LANE0 + N_EXPERTS].set(be)
    return w.astype(BF16), b


def _token_tile(ref, t):
    return ref.at[pl.ds(pl.multiple_of(t * V7X_SUBLANES, V7X_SUBLANES), V7X_SUBLANES)]


def _scatter_body(off_ref, bucket_ref, rank_ref, rows_ref, sorted_hbm, out_hbm, sem, *, chunk):
    del sorted_hbm

    def start(t, carry):
        p = off_ref[bucket_ref[t]] + rank_ref[t]
        pltpu.make_async_copy(_token_tile(rows_ref, t), _token_tile(out_hbm, p), sem).start()
        return carry

    lax.fori_loop(0, chunk, start, 0, unroll=8)
    pltpu.make_async_copy(rows_ref, out_hbm.at[pl.ds(0, chunk * V7X_SUBLANES)], sem).wait()


def _gather_body(off_ref, bucket_ref, rank_ref, sorted_hbm, out_ref, sem, *, chunk):
    def start(t, carry):
        p = off_ref[bucket_ref[t]] + rank_ref[t]
        pltpu.make_async_copy(_token_tile(sorted_hbm, p), _token_tile(out_ref, t), sem).start()
        return carry

    lax.fori_loop(0, chunk, start, 0, unroll=8)
    pltpu.make_async_copy(sorted_hbm.at[pl.ds(0, chunk * V7X_SUBLANES)], out_ref, sem).wait()


def _permute(offsets, bucket, rank, rows, sorted_rows, *, n_rows, scatter, name):
    chunk = min(PERM_CHUNK, n_rows)
    smem_blk = pl.BlockSpec((chunk,), lambda i, *_: (i,), memory_space=pltpu.SMEM)
    any_spec = pl.BlockSpec(memory_space=pl.ANY)
    vmem_blk = pl.BlockSpec((chunk * V7X_SUBLANES, V7X_LANES), lambda i, *_: (i, 0))
    if scatter:
        body = functools.partial(_scatter_body, chunk=chunk)
        args = (offsets, bucket, rank, rows, sorted_rows)
        in_specs = [smem_blk, smem_blk, vmem_blk, any_spec]
        out_shape = jax.ShapeDtypeStruct(sorted_rows.shape, sorted_rows.dtype)
        out_specs = any_spec
        aliases = {4: 0}
    else:
        body = functools.partial(_gather_body, chunk=chunk)
        args = (offsets, bucket, rank, sorted_rows)
        in_specs = [smem_blk, smem_blk, any_spec]
        out_shape = jax.ShapeDtypeStruct((n_rows * V7X_SUBLANES, V7X_LANES), sorted_rows.dtype)
        out_specs = vmem_blk
        aliases = {}
    grid_spec = pltpu.PrefetchScalarGridSpec(
        num_scalar_prefetch=1, grid=(n_rows // chunk,), in_specs=in_specs, out_specs=out_specs,
        scratch_shapes=[pltpu.SemaphoreType.DMA])
    return pl.pallas_call(
        body, out_shape=out_shape, grid_spec=grid_spec, input_output_aliases=aliases,
        compiler_params=pltpu.CompilerParams(dimension_semantics=("arbitrary",), has_side_effects=True,
                                             vmem_limit_bytes=V7X_VMEM_LIMIT_BYTES),
        name=name,
    )(*args)


def _moe_body(te1_ref, te2_ref, tval_ref, trow_ref, x_ref, w13a_ref, w13b_ref, w2a_ref, w2b_ref,
              wr_ref, br_ref, y_ref):
    i = pl.program_id(0)

    @pl.when(tval_ref[i] == 0)
    def _():
        y_ref[...] = jnp.zeros_like(y_ref)

    @pl.when(tval_ref[i] > 0)
    def _():
        tm = x_ref.shape[0] // V7X_SUBLANES
        x = _load_token_tiles(x_ref).astype(BF16)
        e1 = te1_ref[i]
        e2 = te2_ref[i]
        grp = e1 >> 3
        logits = _dot(x, wr_ref[...]) + br_ref[...]
        lane = lax.broadcasted_iota(jnp.int32, (tm, V7X_LANES), 1)
        pick = lambda idx: jnp.sum(jnp.where(lane == idx, logits, 0.0), axis=1, keepdims=True)
        gl = jnp.where(lane < N_GROUPS, logits, -jnp.inf)
        gm = jnp.max(gl, axis=1, keepdims=True)
        gz = jnp.sum(jnp.exp(gl - gm), axis=1, keepdims=True)
        g_w = jnp.exp(pick(ROUTER_GROUP_LANE0 + grp) - gm) / gz
        l1 = pick(ROUTER_EXPERT_LANE0 + e1)
        l2 = pick(ROUTER_EXPERT_LANE0 + e2)
        m = jnp.maximum(l1, l2)
        p1 = jnp.exp(l1 - m)
        p2 = jnp.exp(l2 - m)
        c1 = g_w * p1 / (p1 + p2)
        c2 = g_w * p2 / (p1 + p2)

        def expert(w13_ref, w2_ref):
            hg = _dot(x, w13_ref[0])
            h = hg[:, :D_EXPERT]
            hid = (h * _sigmoid(h)) * hg[:, D_EXPERT:]
            return _dot(hid.astype(BF16), w2_ref[0])

        _store_token_tiles(y_ref, c1 * expert(w13a_ref, w2a_ref) + c2 * expert(w13b_ref, w2b_ref))


def _moe_sorted(xs, te1, te2, tval, trow, w13, w2, wr_hi, br):
    d = w13.shape[1]
    n_tiles = xs.shape[0] // (MOE_TM * V7X_SUBLANES)
    tile_blk = (MOE_TM * V7X_SUBLANES, V7X_LANES)
    xmap = lambda i, te1, te2, tval, trow: (trow[i], 0)
    wa = lambda i, te1, te2, tval, trow: (te1[i], 0, 0)
    wb = lambda i, te1, te2, tval, trow: (te2[i], 0, 0)
    const = lambda i, *_: (0, 0)
    grid_spec = pltpu.PrefetchScalarGridSpec(
        num_scalar_prefetch=4, grid=(n_tiles,),
        in_specs=[pl.BlockSpec(tile_blk, xmap),
                  pl.BlockSpec((1, d, 2 * D_EXPERT), wa), pl.BlockSpec((1, d, 2 * D_EXPERT), wb),
                  pl.BlockSpec((1, D_EXPERT, d), wa), pl.BlockSpec((1, D_EXPERT, d), wb),
                  pl.BlockSpec((d, V7X_LANES), const), pl.BlockSpec((1, V7X_LANES), const)],
        out_specs=pl.BlockSpec(tile_blk, lambda i, *_: (i, 0)))
    return pl.pallas_call(
        _moe_body, out_shape=jax.ShapeDtypeStruct(xs.shape, F32), grid_spec=grid_spec,
        compiler_params=_cparams("arbitrary"), name="moe_experts",
    )(te1, te2, tval, trow, xs, w13, w13, w2, w2, wr_hi, br)


def _moe(h_parts, route_parts, counts, w13, w2, wr_hi, br):
    n_rows = [h.shape[0] // V7X_SUBLANES for h in h_parts]
    n_tiles = sum(n_rows) // MOE_TM + N_BUCKETS
    cnt = counts[0, :N_BUCKETS].astype(jnp.int32)
    tiles_b = (cnt + MOE_TM - 1) // MOE_TM
    tile_end = jnp.cumsum(tiles_b)
    n_valid = tile_end[-1]
    offsets = jnp.zeros((V7X_LANES,), jnp.int32).at[:N_BUCKETS].set((tile_end - tiles_b) * MOE_TM)
    tile_id = jnp.arange(n_tiles, dtype=jnp.int32)
    tval = (tile_id < n_valid).astype(jnp.int32)
    trow = jnp.minimum(tile_id, n_valid - 1)
    tbucket = jnp.sum((trow[:, None] >= tile_end[None, :]).astype(jnp.int32), axis=1)
    tbucket = jnp.minimum(tbucket, N_BUCKETS - 1)
    te1 = jnp.asarray(_BUCKET_E1)[tbucket]
    te2 = jnp.asarray(_BUCKET_E2)[tbucket]

    xs = jnp.zeros((n_tiles * MOE_TM * V7X_SUBLANES, V7X_LANES), F32)
    for h, n, (bucket, rank) in zip(h_parts, n_rows, route_parts):
        xs = _permute(offsets, bucket, rank, h, xs, n_rows=n, scatter=True, name="moe_scatter_rows")
    ys = _moe_sorted(xs, te1, te2, tval, trow, w13, w2, wr_hi, br)
    return [_permute(offsets, bucket, rank, None, ys, n_rows=n, scatter=False, name="moe_gather_rows")
            for n, (bucket, rank) in zip(n_rows, route_parts)]


def _final_body(x_ref, y_ref, mod_ref, g_ref, o_ref):
    x = x_ref[...] + mod_ref[0, 5:6, :] * _load_token_tiles(y_ref)
    o_ref[...] = x * lax.rsqrt(jnp.mean(x * x, axis=-1, keepdims=True) + NORM_EPS) * g_ref[...]


def _final(x, y, mod, g, *, seq):
    t, d = x.shape
    tm = min(PROJ_TM, seq)
    tiles_per_seq = seq // tm
    row = lambda i: (i, 0)
    return pl.pallas_call(
        _final_body, out_shape=jax.ShapeDtypeStruct((t, d), F32), grid=(t // tm,),
        in_specs=[pl.BlockSpec((tm, d), row), pl.BlockSpec((tm * V7X_SUBLANES, V7X_LANES), row),
                  pl.BlockSpec((1, N_MOD, d), lambda i: (i // tiles_per_seq, 0, 0)),
                  pl.BlockSpec((1, d), lambda i: (0, 0))],
        out_specs=pl.BlockSpec((tm, d), row),
        compiler_params=_cparams("arbitrary"), name="final_norm",
    )(x, y, mod, g.reshape(1, d))


def kernel(x, c, ctx, c_ctx, w_mod, b_mod, norm_mix, norm_ffn, norm_final, even_w_in, even_w_out, even_rpb,
           even_dw, even_dw_b, even_ln_g, even_ln_b, odd_w_in, odd_w_out, odd_conv, odd_lambda, odd_subln,
           moe_wg, moe_bg, moe_we, moe_be, moe_w1, moe_w3, moe_w2):
    bsz, n_tok, d = x.shape
    ctx_len = ctx.shape[1]
    depth = w_mod.shape[0]
    assert depth == 2 and d == D_MODEL
    xs = x.reshape(bsz * n_tok, d)
    cs = ctx.reshape(bsz * ctx_len, d)

    mod_rows = -(-(bsz + 1) // 8) * 8
    cc = jnp.zeros((mod_rows, d), F32).at[:bsz].set(c).at[bsz].set(c_ctx)
    mod = _modulation(cc, w_mod, b_mod).reshape(depth, mod_rows, N_MOD, d)
    bf = lambda a: a.astype(BF16)
    ew = _expert_weights_bf16
    zero_counts = jnp.zeros((1, V7X_LANES), F32)

    l = 0
    w_in = bf(even_w_in[0])
    w_out = bf(even_w_out[0])
    px = _proj(xs, mod[l], norm_mix[l], w_in, seq=n_tok, mod_row=None, q_seg=0, name="proj_even_x")
    pc = _proj(cs, mod[l], norm_mix[l], w_in, seq=None, mod_row=bsz, q_seg=0, name="proj_even_c")
    a_x = _natten(px, pc, even_rpb[0], bsz=bsz, n_tok=n_tok, ctx_len=ctx_len)
    a_c = _ctxattn(pc, bsz=bsz, ctx_len=ctx_len)
    b_x = _conformer(px, even_dw[0], even_dw_b[0], even_ln_g[0], even_ln_b[0], bsz=bsz, seq=n_tok)
    b_c = _conformer(pc, even_dw[0], even_dw_b[0], even_ln_g[0], even_ln_b[0], bsz=bsz, seq=ctx_len)
    wr, br = _router_weights(moe_wg[l], moe_bg[l], moe_we[l], moe_be[l])
    x1, hx, bkt_x, rnk_x, counts = _oproj(a_x, b_x, xs, mod[l], norm_ffn[l], w_out, wr, br, zero_counts,
                                          seq=n_tok, mod_row=None, name="oproj_even_x")
    c1, hc, bkt_c, rnk_c, counts = _oproj(a_c, b_c, cs, mod[l], norm_ffn[l], w_out, wr, br, counts,
                                          seq=None, mod_row=bsz, name="oproj_even_c")
    y_x, y_c = _moe([hx, hc], [(bkt_x.reshape(-1), rnk_x.reshape(-1)), (bkt_c.reshape(-1), rnk_c.reshape(-1))],
                    counts, ew((moe_w1, moe_w3), l), ew((moe_w2,), l), wr, br)

    l = 1
    lam_init = 0.8 - 0.6 * math.exp(-0.3 * l)
    w_in = bf(odd_w_in[0])
    w_out = bf(odd_w_out[0])
    rope = _rope_tables(n_tok)
    px, x2 = _proj(x1, mod[l], norm_mix[l], w_in, seq=n_tok, mod_row=None, y=y_x, res_mod=mod[l - 1], res_idx=5,
                   rope=rope, rope_segs=(3, 4), q_seg=3, write_x=True, name="proj_odd_x")
    pc = _proj(c1, mod[l], norm_mix[l], w_in[:, 4 * SEG:], seq=None, mod_row=bsz, y=y_c, res_mod=mod[l - 1],
               res_idx=5, name="proj_odd_c")
    s_x = _shortconv(px, odd_conv[0], bsz=bsz, seq=n_tok)
    d_x = _diffattn(px, pc, odd_lambda[0], odd_subln[0], lam_init, bsz=bsz, n_tok=n_tok, ctx_len=ctx_len)
    wr, br = _router_weights(moe_wg[l], moe_bg[l], moe_we[l], moe_be[l])
    x3, hx, bkt_x, rnk_x, counts = _oproj(s_x, d_x, x2, mod[l], norm_ffn[l], w_out, wr, br, zero_counts,
                                          seq=n_tok, mod_row=None, name="oproj_odd_x")
    (y_x,) = _moe([hx], [(bkt_x.reshape(-1), rnk_x.reshape(-1))], counts,
                  ew((moe_w1, moe_w3), l), ew((moe_w2,), l), wr, br)
    out = _final(x3, y_x, mod[l], norm_final, seq=n_tok)
    return out.reshape(bsz, n_tok, d)
```

```python
import functools
import math

import numpy as np
import jax
import jax.numpy as jnp
from jax import lax
from jax.experimental import pallas as pl
from jax.experimental.pallas import tpu as pltpu

F32 = jnp.float32
BF16 = jnp.bfloat16

D_MODEL = 1024
GRID_W = 64
N_MOD = 6
NORM_EPS = 1e-6
NEG_INF = -1e30
ROPE_BASE = 10000.0
SEG = 512
A_HEAD_DIM = 64
A_HEADS = 8
WIN_ROWS_MAX = 8
WIN_COLS = 16
B_CONV = 31
C_CONV = 3
D_QK_DIM = 64
D_V_DIM = 128
D_HEADS = 4
N_GROUPS = 4
EXPERTS_PER_GROUP = 8
N_EXPERTS = 32
D_EXPERT = 256
LOG2E = math.log2(math.e)
QUERY_SCALE = A_HEAD_DIM ** -0.5 * LOG2E
PAIRS_PER_GROUP = EXPERTS_PER_GROUP * (EXPERTS_PER_GROUP - 1) // 2
N_BUCKETS = N_GROUPS * PAIRS_PER_GROUP

V7X_LANES = 128
V7X_SUBLANES = 8
V7X_VMEM_LIMIT_BYTES = 56 * 1024 * 1024

PROJ_TM = 512
ATT_TQ = 256
NAT_TILES_PER_STEP = 4
DIFF_TQ = 1024
NAT_KROWS = 12
CONV_CHUNK = 64
CONV_PAD = 16
MOE_TM = 256
PERM_CHUNK = 1024
ROUTER_GROUP_LANE0 = 0
ROUTER_EXPERT_LANE0 = 8

_PAIR_LO = np.array([i for i in range(8) for j in range(i + 1, 8)], np.int32)
_PAIR_HI = np.array([j for i in range(8) for j in range(i + 1, 8)], np.int32)
_BUCKET_E1 = np.concatenate([g * 8 + _PAIR_LO for g in range(N_GROUPS)]).astype(np.int32)
_BUCKET_E2 = np.concatenate([g * 8 + _PAIR_HI for g in range(N_GROUPS)]).astype(np.int32)


def _cparams(*sem):
    return pltpu.CompilerParams(dimension_semantics=tuple(sem), vmem_limit_bytes=V7X_VMEM_LIMIT_BYTES)


def _dot(a, b):
    return jnp.dot(a, b, preferred_element_type=F32)


def _dot_t(a, b):
    return lax.dot_general(a, b, (((1,), (1,)), ((), ())), preferred_element_type=F32)


def _split_bf16(a):
    hi = a.astype(BF16)
    lo = (a - hi.astype(F32)).astype(BF16)
    return hi, lo


def _dot3(a_hi, a_lo, b_hi, b_lo):
    return _dot(a_hi, b_hi) + _dot(a_lo, b_hi) + _dot(a_hi, b_lo)


def _sigmoid(x):
    return 1.0 / (1.0 + jnp.exp(-x))


def _load_token_tiles(ref):
    tm = ref.shape[0] // V7X_SUBLANES
    return jnp.concatenate([ref[pl.ds(c, tm, stride=V7X_SUBLANES), :] for c in range(V7X_SUBLANES)], axis=1)


def _store_token_tiles(ref, v):
    tm = v.shape[0]
    for c in range(V7X_SUBLANES):
        ref[pl.ds(c, tm, stride=V7X_SUBLANES), :] = v[:, c * V7X_LANES:(c + 1) * V7X_LANES]


def _rmsnorm_mod(x, g, scale, shift):
    y = x * lax.rsqrt(jnp.mean(x * x, axis=-1, keepdims=True) + NORM_EPS)
    return (y * g) * (1.0 + scale) + shift


def _cast_body(*refs):
    *w_refs, o_ref = refs
    n = w_refs[0].shape[-1]
    for j, w_ref in enumerate(w_refs):
        o_ref[:, :, j * n:(j + 1) * n] = w_ref[0].astype(BF16)


def _expert_weights_bf16(ws, layer):
    _, n_exp, k, n = ws[0].shape
    eb = 4
    return pl.pallas_call(
        _cast_body, out_shape=jax.ShapeDtypeStruct((n_exp, k, len(ws) * n), BF16), grid=(n_exp // eb,),
        in_specs=[pl.BlockSpec((1, eb, k, n), lambda e: (layer, e, 0, 0))] * len(ws),
        out_specs=pl.BlockSpec((eb, k, len(ws) * n), lambda e: (e, 0, 0)),
        compiler_params=_cparams("arbitrary"), name="expert_weights_bf16",
    )(*ws)


def _mod_body(c_ref, w_ref, b_ref, o_ref):
    c = c_ref[...]
    s = c * _sigmoid(c)
    s_hi, s_lo = _split_bf16(s)
    w_hi, w_lo = _split_bf16(w_ref[0])
    o_ref[0] = _dot3(s_hi, s_lo, w_hi, w_lo) + b_ref[0]


def _modulation(cc, w_mod, b_mod):
    depth, d, n = w_mod.shape
    rows = cc.shape[0]
    tn = 1536
    return pl.pallas_call(
        _mod_body,
        out_shape=jax.ShapeDtypeStruct((depth, rows, n), F32),
        grid=(depth, n // tn),
        in_specs=[pl.BlockSpec((rows, d), lambda l, j: (0, 0)),
                  pl.BlockSpec((1, d, tn), lambda l, j: (l, 0, j)),
                  pl.BlockSpec((1, 1, tn), lambda l, j: (l, 0, j))],
        out_specs=pl.BlockSpec((1, rows, tn), lambda l, j: (l, 0, j)),
        compiler_params=_cparams("arbitrary", "arbitrary"),
        name="modulation",
    )(cc, w_mod, b_mod.reshape(depth, 1, n))


def _rope(v, cos, sin, lane):
    up = pltpu.roll(v, V7X_LANES - 16, axis=1)
    dn = pltpu.roll(v, 16, axis=1)
    sw = jnp.where((lane & 31) < 16, up, dn)
    return v * cos + sw * sin


def _proj_body(*refs, n_seg, res_idx, shift_idx, rope_segs, q_seg, write_x):
    refs = list(refs)
    x_ref = refs.pop(0)
    y_ref = refs.pop(0) if res_idx is not None else None
    rmod_ref = refs.pop(0) if res_idx is not None else None
    mod_ref = refs.pop(0)
    g_ref = refs.pop(0)
    w_ref = refs.pop(0)
    cos_ref = sin_ref = None
    if rope_segs:
        cos_ref = refs.pop(0)
        sin_ref = refs.pop(0)
    out_ref = refs.pop(0)
    xo_ref = refs.pop(0) if write_x else None

    x = x_ref[...]
    if y_ref is not None:
        x = x + rmod_ref[0, res_idx:res_idx + 1, :] * _load_token_tiles(y_ref)
        if write_x:
            xo_ref[...] = x
    h = _rmsnorm_mod(x, g_ref[...], mod_ref[0, shift_idx + 1:shift_idx + 2, :],
                     mod_ref[0, shift_idx:shift_idx + 1, :]).astype(BF16)
    for s in range(n_seg):
        o = _dot(h, w_ref[:, s * SEG:(s + 1) * SEG])
        if s == q_seg:
            o = o * QUERY_SCALE
        if s in rope_segs:
            lane = lax.broadcasted_iota(jnp.int32, (o.shape[0], V7X_LANES), 1)
            cos = cos_ref[...]
            sin = sin_ref[...]
            o = jnp.concatenate(
                [_rope(o[:, c * V7X_LANES:(c + 1) * V7X_LANES], cos, sin, lane) for c in range(SEG // V7X_LANES)],
                axis=1)
        out_ref[:, s * SEG:(s + 1) * SEG] = o.astype(BF16)


def _proj(x, mod, g, w, *, seq, mod_row, y=None, res_mod=None, res_idx=None, shift_idx=0, rope=None, rope_segs=(),
          q_seg=None, write_x=False, name="proj"):
    t, d = x.shape
    n = w.shape[1]
    tm = min(PROJ_TM, t if seq is None else seq)
    tiles_per_seq = None if seq is None else seq // tm
    if seq is None:
        mod_map = lambda i: (mod_row, 0, 0)
    else:
        mod_map = lambda i: (i // tiles_per_seq, 0, 0)
    args = [x]
    specs = [pl.BlockSpec((tm, d), lambda i: (i, 0))]
    if y is not None:
        args += [y, res_mod]
        specs += [pl.BlockSpec((tm * V7X_SUBLANES, V7X_LANES), lambda i: (i, 0)),
                  pl.BlockSpec((1, N_MOD, d), mod_map)]
    args += [mod, g.reshape(1, d), w]
    specs += [pl.BlockSpec((1, N_MOD, d), mod_map), pl.BlockSpec((1, d), lambda i: (0, 0)),
              pl.BlockSpec((d, n), lambda i: (0, 0))]
    if rope_segs:
        args += [rope[0], rope[1]]
        specs += [pl.BlockSpec((tm, V7X_LANES), lambda i: (i % tiles_per_seq, 0))] * 2
    out_shape = [jax.ShapeDtypeStruct((t, n), BF16)]
    out_specs = [pl.BlockSpec((tm, n), lambda i: (i, 0))]
    if write_x:
        out_shape.append(jax.ShapeDtypeStruct((t, d), F32))
        out_specs.append(pl.BlockSpec((tm, d), lambda i: (i, 0)))
    body = functools.partial(_proj_body, n_seg=n // SEG, res_idx=res_idx if y is not None else None,
                             shift_idx=shift_idx, rope_segs=tuple(rope_segs), q_seg=q_seg, write_x=write_x)
    outs = pl.pallas_call(
        body, out_shape=out_shape, grid=(t // tm,), in_specs=specs, out_specs=out_specs,
        compiler_params=_cparams("arbitrary"), name=name,
    )(*args)
    return outs if write_x else outs[0]


def _rope_tables(n_tok):
    quarter = D_QK_DIM // 4
    t = np.arange(n_tok)
    row = (t // GRID_W).astype(np.float32)
    col = (t % GRID_W).astype(np.float32)
    inv = jnp.power(ROPE_BASE, -jnp.arange(quarter, dtype=F32) / quarter)
    ar = jnp.asarray(row)[:, None] * inv
    ac = jnp.asarray(col)[:, None] * inv
    cos64 = jnp.concatenate([jnp.cos(ar), jnp.cos(ar), jnp.cos(ac), jnp.cos(ac)], axis=-1)
    sin64 = jnp.concatenate([-jnp.sin(ar), jnp.sin(ar), -jnp.sin(ac), jnp.sin(ac)], axis=-1)
    return jnp.tile(cos64, (1, V7X_LANES // D_QK_DIM)), jnp.tile(sin64, (1, V7X_LANES // D_QK_DIM))


def _stack_halves(q):
    lane = lax.broadcasted_iota(jnp.int32, q.shape, 1)
    zero = jnp.zeros_like(q)
    return jnp.concatenate([jnp.where(lane < 64, q, zero), jnp.where(lane >= 64, q, zero)], axis=0)


def _attend(scores, values):
    m = functools.reduce(jnp.maximum, [jnp.max(s, axis=1, keepdims=True) for s in scores])
    acc = None
    for s, v in zip(scores, values):
        p = jnp.exp2(s - m).astype(BF16)
        lane = lax.broadcasted_iota(jnp.int32, v.shape, 1)
        ones_col = jnp.where(lane == 0, 1.0, 0.0).astype(BF16)
        o = _dot(p, jnp.concatenate([v, ones_col], axis=1))
        acc = o if acc is None else acc + o
    return acc[:, :V7X_LANES], acc[:, V7X_LANES:V7X_LANES + 1]


def _natten_body(ks_ref, pid_ref, q_ref, k_ref, v_ref, kc_ref, vc_ref, bias_ref, o_ref):
    nk = NAT_KROWS * GRID_W
    tq = ATT_TQ
    lane = lax.broadcasted_iota(jnp.int32, (tq, V7X_LANES), 1)
    for s in range(q_ref.shape[0] // tq):
        j = pl.program_id(1) * (q_ref.shape[0] // tq) + s
        k0 = pl.multiple_of(ks_ref[j] * GRID_W, GRID_W)
        p = pid_ref[j]
        rows = pl.ds(s * tq, tq)
        for hp in range(A_HEADS // 2):
            cs = slice(hp * V7X_LANES, (hp + 1) * V7X_LANES)
            q2 = _stack_halves(q_ref[rows, cs])
            s_c = _dot_t(q2, kc_ref[:, cs])
            s_l = _dot_t(q2, k_ref[pl.ds(k0, nk), cs])
            bias = jnp.concatenate([bias_ref[p, 2 * hp], bias_ref[p, 2 * hp + 1]], axis=0).astype(F32)
            o2, l = _attend([s_c, s_l + bias], [vc_ref[:, cs], v_ref[pl.ds(k0, nk), cs]])
            o2 = o2 / l
            o_ref[rows, cs] = jnp.where(lane < 64, o2[:tq], o2[tq:]).astype(BF16)


def _natten_tables(rows):
    q_rows = ATT_TQ // GRID_W
    nqb = rows // q_rows
    win_r = min(WIN_ROWS_MAX, rows)
    col = np.arange(GRID_W)
    c_start = np.clip(col - WIN_COLS // 2, 0, GRID_W - WIN_COLS)
    col_ok = (col[None, :] >= c_start[:, None]) & (col[None, :] < c_start[:, None] + WIN_COLS)
    dc = np.clip(col[None, :] - col[:, None], -(WIN_COLS - 1), WIN_COLS - 1) + WIN_COLS - 1
    ks_list, pats, pid = [], [], []
    for j in range(nqb):
        r0 = j * q_rows
        ks = int(np.clip(r0 - win_r // 2, 0, rows - NAT_KROWS))
        qr = r0 + np.arange(q_rows)
        rs = np.clip(qr - win_r // 2, 0, rows - win_r)
        kr = ks + np.arange(NAT_KROWS)
        row_ok = (kr[None, :] >= rs[:, None]) & (kr[None, :] < rs[:, None] + win_r)
        dr = np.clip(kr[None, :] - qr[:, None] + WIN_ROWS_MAX - 1, 0, 2 * WIN_ROWS_MAX - 2)
        key = (row_ok.tobytes(), (dr * row_ok).tobytes())
        for n, (k_, *_rest) in enumerate(pats):
            if k_ == key:
                pid.append(n)
                break
        else:
            pid.append(len(pats))
            pats.append((key, row_ok, dr))
        ks_list.append(ks)
    row_ok = np.stack([p_[1] for p_ in pats])
    dr = np.stack([p_[2] for p_ in pats])
    return np.array(ks_list, np.int32), np.array(pid, np.int32), row_ok, dr, col_ok, dc


def _natten_bias(rpb, row_ok, dr, col_ok, dc):
    n_pat, q_rows, k_rows = dr.shape
    n_dr, n_dc = rpb.shape[1], rpb.shape[2]
    oh_r = np.zeros((n_pat * q_rows * k_rows, n_dr), np.float32)
    oh_r[np.arange(oh_r.shape[0]), dr.reshape(-1)] = 1.0
    oh_c = np.zeros((n_dc, GRID_W * GRID_W), np.float32)
    oh_c[dc.reshape(-1), np.arange(GRID_W * GRID_W)] = 1.0
    hi = lax.Precision.HIGHEST
    t1 = jnp.einsum("mr,hrc->hmc", jnp.asarray(oh_r), rpb, precision=hi)
    t2 = jnp.einsum("hmc,cn->hmn", t1, jnp.asarray(oh_c), precision=hi)
    t2 = t2.reshape(rpb.shape[0], n_pat, q_rows, k_rows, GRID_W, GRID_W)
    valid = row_ok[None, :, :, :, None, None] & col_ok[None, None, None, None, :, :]
    bias = jnp.where(valid, t2 * LOG2E, NEG_INF).transpose(1, 0, 2, 4, 3, 5)
    return bias.reshape(n_pat, rpb.shape[0], q_rows * GRID_W, k_rows * GRID_W).astype(BF16)


def _natten(px, pc, rpb, *, bsz, n_tok, ctx_len):
    rows = n_tok // GRID_W
    assert rows >= NAT_KROWS and n_tok % ATT_TQ == 0
    ks, pid, row_ok, dr, col_ok, dc = _natten_tables(rows)
    bias = _natten_bias(rpb, row_ok, dr, col_ok, dc)
    n_pat = bias.shape[0]
    q_blk = ATT_TQ * NAT_TILES_PER_STEP
    assert n_tok % q_blk == 0
    nqb = n_tok // q_blk
    nk = NAT_KROWS * GRID_W
    grid_spec = pltpu.PrefetchScalarGridSpec(
        num_scalar_prefetch=2, grid=(bsz, nqb),
        in_specs=[pl.BlockSpec((q_blk, SEG), lambda b, j, *_: (b * nqb + j, 0)),
                  pl.BlockSpec((n_tok, SEG), lambda b, j, *_: (b, 1)),
                  pl.BlockSpec((n_tok, SEG), lambda b, j, *_: (b, 2)),
                  pl.BlockSpec((ctx_len, SEG), lambda b, j, *_: (b, 1)),
                  pl.BlockSpec((ctx_len, SEG), lambda b, j, *_: (b, 2)),
                  pl.BlockSpec((n_pat, A_HEADS, ATT_TQ, nk), lambda b, j, *_: (0, 0, 0, 0))],
        out_specs=pl.BlockSpec((q_blk, SEG), lambda b, j, *_: (b * nqb + j, 0)))
    return pl.pallas_call(
        _natten_body, out_shape=jax.ShapeDtypeStruct((bsz * n_tok, SEG), BF16), grid_spec=grid_spec,
        compiler_params=_cparams("arbitrary", "arbitrary"), name="natten",
    )(jnp.asarray(ks), jnp.asarray(pid), px, px, px, pc, pc, bias)


def _ctxattn_body(q_ref, k_ref, v_ref, o_ref):
    tq = q_ref.shape[0]
    lane = lax.broadcasted_iota(jnp.int32, (tq, V7X_LANES), 1)
    for hp in range(A_HEADS // 2):
        cs = slice(hp * V7X_LANES, (hp + 1) * V7X_LANES)
        q2 = _stack_halves(q_ref[:, cs])
        o2, l = _attend([_dot_t(q2, k_ref[:, cs])], [v_ref[:, cs]])
        o2 = o2 / l
        o_ref[:, cs] = jnp.where(lane < 64, o2[:tq], o2[tq:]).astype(BF16)


def _ctxattn(pc, *, bsz, ctx_len):
    return pl.pallas_call(
        _ctxattn_body, out_shape=jax.ShapeDtypeStruct((bsz * ctx_len, SEG), BF16), grid=(bsz,),
        in_specs=[pl.BlockSpec((ctx_len, SEG), lambda b: (b, 0)),
                  pl.BlockSpec((ctx_len, SEG), lambda b: (b, 1)),
                  pl.BlockSpec((ctx_len, SEG), lambda b: (b, 2))],
        out_specs=pl.BlockSpec((ctx_len, SEG), lambda b: (b, 0)),
        compiler_params=_cparams("arbitrary"), name="ctxattn",
    )(pc, pc, pc)


def _conv_accumulate(z_ref, w_ref, t0, taps):
    base = CONV_PAD - taps // 2
    blocks = []
    for c in range(SEG // V7X_LANES):
        cs = slice(c * V7X_LANES, (c + 1) * V7X_LANES)
        win = z_ref[pl.ds(t0, CONV_CHUNK + 2 * CONV_PAD), cs]
        acc = None
        for b in range(V7X_SUBLANES):
            ks = [k for k in range(taps) if (base + k) % V7X_SUBLANES == b]
            if not ks:
                continue
            shifted = win if b == 0 else pltpu.roll(win, win.shape[0] - b, axis=0)
            for k in ks:
                a0 = (base + k) // V7X_SUBLANES * V7X_SUBLANES
                term = shifted[a0:a0 + CONV_CHUNK, :] * w_ref[k:k + 1, cs]
                acc = term if acc is None else acc + term
        blocks.append(acc)
    return jnp.concatenate(blocks, axis=1)


def _conformer_body(u_ref, g_ref, w_ref, b_ref, lg_ref, lb_ref, o_ref, z_ref):
    n = u_ref.shape[0]
    halo = jnp.zeros((CONV_PAD, SEG), F32)
    z_ref[pl.ds(0, CONV_PAD), :] = halo
    z_ref[pl.ds(CONV_PAD + n, CONV_PAD), :] = halo
    z_ref[pl.ds(CONV_PAD, n), :] = u_ref[...].astype(F32) * _sigmoid(g_ref[...].astype(F32))

    def chunk(i, carry):
        t0 = pl.multiple_of(i * CONV_CHUNK, CONV_CHUNK)
        y = _conv_accumulate(z_ref, w_ref, t0, B_CONV) + b_ref[...]
        mu = jnp.mean(y, axis=-1, keepdims=True)
        yc = y - mu
        var = jnp.mean(yc * yc, axis=-1, keepdims=True)
        y = yc * lax.rsqrt(var + NORM_EPS) * lg_ref[...] + lb_ref[...]
        o_ref[pl.ds(t0, CONV_CHUNK), :] = (y * _sigmoid(y)).astype(BF16)
        return carry

    lax.fori_loop(0, n // CONV_CHUNK, chunk, 0)


def _conformer(p, dw, dw_b, ln_g, ln_b, *, bsz, seq):
    vec = lambda a: a.reshape(1, SEG)
    return pl.pallas_call(
        _conformer_body, out_shape=jax.ShapeDtypeStruct((bsz * seq, SEG), BF16), grid=(bsz,),
        in_specs=[pl.BlockSpec((seq, SEG), lambda b: (b, 3)),
                  pl.BlockSpec((seq, SEG), lambda b: (b, 4)),
                  pl.BlockSpec((B_CONV, SEG), lambda b: (0, 0))] + [pl.BlockSpec((1, SEG), lambda b: (0, 0))] * 3,
        out_specs=pl.BlockSpec((seq, SEG), lambda b: (b, 0)),
        scratch_shapes=[pltpu.VMEM((seq + 2 * CONV_PAD, SEG), F32)],
        compiler_params=_cparams("arbitrary"), name="conformer_conv",
    )(p, p, dw, vec(dw_b), vec(ln_g), vec(ln_b))


def _shortconv_body(bg_ref, cg_ref, u_ref, w_ref, o_ref, z_ref):
    n = u_ref.shape[0]
    halo = jnp.zeros((CONV_PAD, SEG), F32)
    z_ref[pl.ds(0, CONV_PAD), :] = halo
    z_ref[pl.ds(CONV_PAD + n, CONV_PAD), :] = halo
    z_ref[pl.ds(CONV_PAD, n), :] = cg_ref[...].astype(F32) * u_ref[...].astype(F32)

    def chunk(i, carry):
        t0 = pl.multiple_of(i * CONV_CHUNK, CONV_CHUNK)
        y = _conv_accumulate(z_ref, w_ref, t0, C_CONV)
        o_ref[pl.ds(t0, CONV_CHUNK), :] = (bg_ref[pl.ds(t0, CONV_CHUNK), :].astype(F32) * y).astype(BF16)
        return carry

    lax.fori_loop(0, n // CONV_CHUNK, chunk, 0)


def _shortconv(p, conv_w, *, bsz, seq):
    return pl.pallas_call(
        _shortconv_body, out_shape=jax.ShapeDtypeStruct((bsz * seq, SEG), BF16), grid=(bsz,),
        in_specs=[pl.BlockSpec((seq, SEG), lambda b: (b, 0)),
                  pl.BlockSpec((seq, SEG), lambda b: (b, 1)),
                  pl.BlockSpec((seq, SEG), lambda b: (b, 2)),
                  pl.BlockSpec((C_CONV, SEG), lambda b: (0, 0))],
        out_specs=pl.BlockSpec((seq, SEG), lambda b: (b, 0)),
        scratch_shapes=[pltpu.VMEM((seq + 2 * CONV_PAD, SEG), F32)],
        compiler_params=_cparams("arbitrary"), name="short_conv",
    )(p, p, p, conv_w)


def _diffattn_body(q_ref, k_ref, v_ref, kc_ref, vc_ref, lam_ref, g_ref, o_ref, *, lam_init):
    tq = ATT_TQ
    lp = lam_ref[...]
    lam = (jnp.exp(jnp.sum(lp[0:1, :] * lp[1:2, :], axis=1, keepdims=True))
           - jnp.exp(jnp.sum(lp[2:3, :] * lp[3:4, :], axis=1, keepdims=True)) + lam_init)
    for s in range(q_ref.shape[0] // tq):
        rows = pl.ds(s * tq, tq)
        for h in range(D_HEADS):
            cs = slice(h * V7X_LANES, (h + 1) * V7X_LANES)
            q2 = _stack_halves(q_ref[rows, cs])
            o2, l = _attend([_dot_t(q2, kc_ref[:, cs]), _dot_t(q2, k_ref[:, cs])], [vc_ref[:, cs], v_ref[:, cs]])
            o2 = o2 / l
            o = o2[:tq] - lam * o2[tq:]
            o = o * lax.rsqrt(jnp.mean(o * o, axis=-1, keepdims=True) + NORM_EPS) * g_ref[...]
            o_ref[rows, cs] = (o * (1.0 - lam_init)).astype(BF16)


def _diffattn(px, pc, lam_p, subln, lam_init, *, bsz, n_tok, ctx_len):
    nq = n_tok // DIFF_TQ
    body = functools.partial(_diffattn_body, lam_init=lam_init)
    return pl.pallas_call(
        body, out_shape=jax.ShapeDtypeStruct((bsz * n_tok, SEG), BF16), grid=(bsz, nq),
        in_specs=[pl.BlockSpec((DIFF_TQ, SEG), lambda b, j: (b * nq + j, 3)),
                  pl.BlockSpec((n_tok, SEG), lambda b, j: (b, 4)),
                  pl.BlockSpec((n_tok, SEG), lambda b, j: (b, 5)),
                  pl.BlockSpec((ctx_len, SEG), lambda b, j: (b, 0)),
                  pl.BlockSpec((ctx_len, SEG), lambda b, j: (b, 1)),
                  pl.BlockSpec((4, D_QK_DIM), lambda b, j: (0, 0)),
                  pl.BlockSpec((1, D_V_DIM), lambda b, j: (0, 0))],
        out_specs=pl.BlockSpec((DIFF_TQ, SEG), lambda b, j: (b * nq + j, 0)),
        compiler_params=_cparams("arbitrary", "arbitrary"), name="diff_attn",
    )(px, px, px, pc, pc, lam_p, subln.reshape(1, D_V_DIM))


def _oproj_body(a_ref, b_ref, x_ref, mod_ref, g_ref, wo_ref, wr_ref, br_ref, cin_ref,
                x1_ref, h_ref, bucket_ref, rank_ref, cout_ref, carry_ref, before_ref):
    i = pl.program_id(0)
    tm = x_ref.shape[0]

    @pl.when(i == 0)
    def _():
        carry_ref[...] = cin_ref[...]
        r_i = lax.broadcasted_iota(jnp.int32, (tm, tm), 0)
        c_i = lax.broadcasted_iota(jnp.int32, (tm, tm), 1)
        before_ref[...] = jnp.where(r_i > c_i, 1.0, 0.0).astype(BF16)

    y = _dot(a_ref[...], wo_ref[0:SEG, :]) + _dot(b_ref[...], wo_ref[SEG:2 * SEG, :])
    x1 = x_ref[...] + mod_ref[0, 2:3, :] * y
    x1_ref[...] = x1
    h = _rmsnorm_mod(x1, g_ref[...], mod_ref[0, 4:5, :], mod_ref[0, 3:4, :])
    _store_token_tiles(h_ref, h)
    logits = _dot(h.astype(BF16), wr_ref[...]) + br_ref[...]
    lane = lax.broadcasted_iota(jnp.int32, (tm, V7X_LANES), 1)

    def first_max(v):
        m = jnp.max(v, axis=1, keepdims=True)
        return jnp.min(jnp.where(v == m, lane, jnp.int32(V7X_LANES)), axis=1, keepdims=True)

    g_sel = first_max(jnp.where(lane < N_GROUPS, logits, -jnp.inf))
    e_lane0 = ROUTER_EXPERT_LANE0 + EXPERTS_PER_GROUP * g_sel
    in_group = (lane >= e_lane0) & (lane < e_lane0 + EXPERTS_PER_GROUP)
    el = jnp.where(in_group, logits, -jnp.inf)
    i1 = first_max(el)
    i2 = first_max(jnp.where(lane == i1, -jnp.inf, el))
    lo = jnp.minimum(i1, i2) - e_lane0
    hi = jnp.maximum(i1, i2) - e_lane0
    bucket = g_sel * PAIRS_PER_GROUP + ((lo * (2 * EXPERTS_PER_GROUP - 1 - lo)) >> 1) + (hi - lo - 1)
    oh = jnp.where(lane == bucket, 1.0, 0.0)
    oh16 = oh.astype(BF16)
    cum = _dot(before_ref[...], oh16) + carry_ref[...]
    carry_ref[...] = carry_ref[...] + jnp.sum(oh, axis=0, keepdims=True)
    rk = oh * cum
    rk_hi = jnp.floor(rk * (1.0 / 256.0))
    rk_lo = rk - 256.0 * rk_hi
    sub = lax.broadcasted_iota(jnp.int32, (V7X_SUBLANES, V7X_LANES), 0)
    lane_id = lax.broadcasted_iota(jnp.int32, (V7X_SUBLANES, V7X_LANES), 1)
    pick_lane = jnp.where(sub == 0, lane_id, 0).astype(F32).astype(BF16)
    ones_row = jnp.where(sub == 0, 1.0, 0.0).astype(BF16)
    bucket_ref[...] = _dot_t(pick_lane, oh16)[0:1, :].astype(jnp.int32)
    rank_row = 256.0 * _dot_t(ones_row, rk_hi.astype(BF16)) + _dot_t(ones_row, rk_lo.astype(BF16))
    rank_ref[...] = rank_row[0:1, :].astype(jnp.int32)

    @pl.when(i == pl.num_programs(0) - 1)
    def _():
        cout_ref[...] = carry_ref[...]


def _oproj(a, b, x, mod, g, wo, wr, br, counts_in, *, seq, mod_row, name="oproj"):
    t, d = x.shape
    tm = min(PROJ_TM, t if seq is None else seq)
    if seq is None:
        mod_map = lambda i: (mod_row, 0, 0)
    else:
        tiles_per_seq = seq // tm
        mod_map = lambda i: (i // tiles_per_seq, 0, 0)
    row = lambda i: (i, 0)
    const = lambda i: (0, 0)
    return pl.pallas_call(
        _oproj_body,
        out_shape=[jax.ShapeDtypeStruct((t, d), F32), jax.ShapeDtypeStruct((t * V7X_SUBLANES, V7X_LANES), F32),
                   jax.ShapeDtypeStruct((1, t), jnp.int32), jax.ShapeDtypeStruct((1, t), jnp.int32),
                   jax.ShapeDtypeStruct((1, V7X_LANES), F32)],
        grid=(t // tm,),
        in_specs=[pl.BlockSpec((tm, SEG), row), pl.BlockSpec((tm, SEG), row), pl.BlockSpec((tm, d), row),
                  pl.BlockSpec((1, N_MOD, d), mod_map), pl.BlockSpec((1, d), const),
                  pl.BlockSpec((2 * SEG, d), const), pl.BlockSpec((d, V7X_LANES), const),
                  pl.BlockSpec((1, V7X_LANES), const), pl.BlockSpec((1, V7X_LANES), const)],
        out_specs=[pl.BlockSpec((tm, d), row), pl.BlockSpec((tm * V7X_SUBLANES, V7X_LANES), row),
                   pl.BlockSpec((1, tm), lambda i: (0, i)), pl.BlockSpec((1, tm), lambda i: (0, i)),
                   pl.BlockSpec((1, V7X_LANES), const)],
        scratch_shapes=[pltpu.VMEM((1, V7X_LANES), F32), pltpu.VMEM((tm, tm), BF16)],
        compiler_params=_cparams("arbitrary"), name=name,
    )(a, b, x, mod, g.reshape(1, d), wo, wr, br, counts_in)


def _router_weights(wg, bg, we, be):
    d = wg.shape[0]
    w = jnp.zeros((d, V7X_LANES), F32)
    w = w.at[:, ROUTER_GROUP_LANE0:ROUTER_GROUP_LANE0 + N_GROUPS].set(wg)
    w = w.at[:, ROUTER_EXPERT_LANE0:ROUTER_EXPERT_LANE0 + N_EXPERTS].set(we)
    b = jnp.zeros((1, V7X_LANES), F32)
    b = b.at[0, ROUTER_GROUP_LANE0:ROUTER_GROUP_LANE0 + N_GROUPS].set(bg)
    b = b.at[0, ROUTER_EXPERT_LANE0:ROUTER_EXPERT_LANE0 + N_EXPERTS].set(be)
    return w.astype(BF16), b


def _token_tile(ref, t):
    return ref.at[pl.ds(pl.multiple_of(t * V7X_SUBLANES, V7X_SUBLANES), V7X_SUBLANES)]


def _scatter_body(off_ref, bucket_ref, rank_ref, rows_ref, sorted_hbm, out_hbm, sem, *, chunk):
    del sorted_hbm

    def start(t, carry):
        p = off_ref[bucket_ref[t]] + rank_ref[t]
        pltpu.make_async_copy(_token_tile(rows_ref, t), _token_tile(out_hbm, p), sem).start()
        return carry

    lax.fori_loop(0, chunk, start, 0, unroll=8)
    pltpu.make_async_copy(rows_ref, out_hbm.at[pl.ds(0, chunk * V7X_SUBLANES)], sem).wait()


def _gather_body(off_ref, bucket_ref, rank_ref, sorted_hbm, out_ref, sem, *, chunk):
    def start(t, carry):
        p = off_ref[bucket_ref[t]] + rank_ref[t]
        pltpu.make_async_copy(_token_tile(sorted_hbm, p), _token_tile(out_ref, t), sem).start()
        return carry

    lax.fori_loop(0, chunk, start, 0, unroll=8)
    pltpu.make_async_copy(sorted_hbm.at[pl.ds(0, chunk * V7X_SUBLANES)], out_ref, sem).wait()


def _permute(offsets, bucket, rank, rows, sorted_rows, *, n_rows, scatter, name):
    chunk = min(PERM_CHUNK, n_rows)
    smem_blk = pl.BlockSpec((chunk,), lambda i, *_: (i,), memory_space=pltpu.SMEM)
    any_spec = pl.BlockSpec(memory_space=pl.ANY)
    vmem_blk = pl.BlockSpec((chunk * V7X_SUBLANES, V7X_LANES), lambda i, *_: (i, 0))
    if scatter:
        body = functools.partial(_scatter_body, chunk=chunk)
        args = (offsets, bucket, rank, rows, sorted_rows)
        in_specs = [smem_blk, smem_blk, vmem_blk, any_spec]
        out_shape = jax.ShapeDtypeStruct(sorted_rows.shape, sorted_rows.dtype)
        out_specs = any_spec
        aliases = {4: 0}
    else:
        body = functools.partial(_gather_body, chunk=chunk)
        args = (offsets, bucket, rank, sorted_rows)
        in_specs = [smem_blk, smem_blk, any_spec]
        out_shape = jax.ShapeDtypeStruct((n_rows * V7X_SUBLANES, V7X_LANES), sorted_rows.dtype)
        out_specs = vmem_blk
        aliases = {}
    grid_spec = pltpu.PrefetchScalarGridSpec(
        num_scalar_prefetch=1, grid=(n_rows // chunk,), in_specs=in_specs, out_specs=out_specs,
        scratch_shapes=[pltpu.SemaphoreType.DMA])
    return pl.pallas_call(
        body, out_shape=out_shape, grid_spec=grid_spec, input_output_aliases=aliases,
        compiler_params=pltpu.CompilerParams(dimension_semantics=("arbitrary",), has_side_effects=True,
                                             vmem_limit_bytes=V7X_VMEM_LIMIT_BYTES),
        name=name,
    )(*args)


def _moe_body(te1_ref, te2_ref, tval_ref, trow_ref, x_ref, w13a_ref, w13b_ref, w2a_ref, w2b_ref,
              wr_ref, br_ref, y_ref):
    i = pl.program_id(0)

    @pl.when(tval_ref[i] == 0)
    def _():
        y_ref[...] = jnp.zeros_like(y_ref)

    @pl.when(tval_ref[i] > 0)
    def _():
        tm = x_ref.shape[0] // V7X_SUBLANES
        x = _load_token_tiles(x_ref).astype(BF16)
        e1 = te1_ref[i]
        e2 = te2_ref[i]
        grp = e1 >> 3
        logits = _dot(x, wr_ref[...]) + br_ref[...]
        lane = lax.broadcasted_iota(jnp.int32, (tm, V7X_LANES), 1)
        pick = lambda idx: jnp.sum(jnp.where(lane == idx, logits, 0.0), axis=1, keepdims=True)
        gl = jnp.where(lane < N_GROUPS, logits, -jnp.inf)
        gm = jnp.max(gl, axis=1, keepdims=True)
        gz = jnp.sum(jnp.exp(gl - gm), axis=1, keepdims=True)
        g_w = jnp.exp(pick(ROUTER_GROUP_LANE0 + grp) - gm) / gz
        l1 = pick(ROUTER_EXPERT_LANE0 + e1)
        l2 = pick(ROUTER_EXPERT_LANE0 + e2)
        m = jnp.maximum(l1, l2)
        p1 = jnp.exp(l1 - m)
        p2 = jnp.exp(l2 - m)
        c1 = g_w * p1 / (p1 + p2)
        c2 = g_w * p2 / (p1 + p2)

        def expert(w13_ref, w2_ref):
            hg = _dot(x, w13_ref[0])
            h = hg[:, :D_EXPERT]
            hid = (h * _sigmoid(h)) * hg[:, D_EXPERT:]
            return _dot(hid.astype(BF16), w2_ref[0])

        _store_token_tiles(y_ref, c1 * expert(w13a_ref, w2a_ref) + c2 * expert(w13b_ref, w2b_ref))


def _moe_sorted(xs, te1, te2, tval, trow, w13, w2, wr_hi, br):
    d = w13.shape[1]
    n_tiles = xs.shape[0] // (MOE_TM * V7X_SUBLANES)
    tile_blk = (MOE_TM * V7X_SUBLANES, V7X_LANES)
    xmap = lambda i, te1, te2, tval, trow: (trow[i], 0)
    wa = lambda i, te1, te2, tval, trow: (te1[i], 0, 0)
    wb = lambda i, te1, te2, tval, trow: (te2[i], 0, 0)
    const = lambda i, *_: (0, 0)
    grid_spec = pltpu.PrefetchScalarGridSpec(
        num_scalar_prefetch=4, grid=(n_tiles,),
        in_specs=[pl.BlockSpec(tile_blk, xmap),
                  pl.BlockSpec((1, d, 2 * D_EXPERT), wa), pl.BlockSpec((1, d, 2 * D_EXPERT), wb),
                  pl.BlockSpec((1, D_EXPERT, d), wa), pl.BlockSpec((1, D_EXPERT, d), wb),
                  pl.BlockSpec((d, V7X_LANES), const), pl.BlockSpec((1, V7X_LANES), const)],
        out_specs=pl.BlockSpec(tile_blk, lambda i, *_: (i, 0)))
    return pl.pallas_call(
        _moe_body, out_shape=jax.ShapeDtypeStruct(xs.shape, F32), grid_spec=grid_spec,
        compiler_params=_cparams("arbitrary"), name="moe_experts",
    )(te1, te2, tval, trow, xs, w13, w13, w2, w2, wr_hi, br)


def _moe(h_parts, route_parts, counts, w13, w2, wr_hi, br):
    n_rows = [h.shape[0] // V7X_SUBLANES for h in h_parts]
    n_tiles = sum(n_rows) // MOE_TM + N_BUCKETS
    cnt = counts[0, :N_BUCKETS].astype(jnp.int32)
    tiles_b = (cnt + MOE_TM - 1) // MOE_TM
    tile_end = jnp.cumsum(tiles_b)
    n_valid = tile_end[-1]
    offsets = jnp.zeros((V7X_LANES,), jnp.int32).at[:N_BUCKETS].set((tile_end - tiles_b) * MOE_TM)
    tile_id = jnp.arange(n_tiles, dtype=jnp.int32)
    tval = (tile_id < n_valid).astype(jnp.int32)
    trow = jnp.minimum(tile_id, n_valid - 1)
    tbucket = jnp.sum((trow[:, None] >= tile_end[None, :]).astype(jnp.int32), axis=1)
    tbucket = jnp.minimum(tbucket, N_BUCKETS - 1)
    te1 = jnp.asarray(_BUCKET_E1)[tbucket]
    te2 = jnp.asarray(_BUCKET_E2)[tbucket]

    xs = jnp.zeros((n_tiles * MOE_TM * V7X_SUBLANES, V7X_LANES), F32)
    for h, n, (bucket, rank) in zip(h_parts, n_rows, route_parts):
        xs = _permute(offsets, bucket, rank, h, xs, n_rows=n, scatter=True, name="moe_scatter_rows")
    ys = _moe_sorted(xs, te1, te2, tval, trow, w13, w2, wr_hi, br)
    return [_permute(offsets, bucket, rank, None, ys, n_rows=n, scatter=False, name="moe_gather_rows")
            for n, (bucket, rank) in zip(n_rows, route_parts)]


def _final_body(x_ref, y_ref, mod_ref, g_ref, o_ref):
    x = x_ref[...] + mod_ref[0, 5:6, :] * _load_token_tiles(y_ref)
    o_ref[...] = x * lax.rsqrt(jnp.mean(x * x, axis=-1, keepdims=True) + NORM_EPS) * g_ref[...]


def _final(x, y, mod, g, *, seq):
    t, d = x.shape
    tm = min(PROJ_TM, seq)
    tiles_per_seq = seq // tm
    row = lambda i: (i, 0)
    return pl.pallas_call(
        _final_body, out_shape=jax.ShapeDtypeStruct((t, d), F32), grid=(t // tm,),
        in_specs=[pl.BlockSpec((tm, d), row), pl.BlockSpec((tm * V7X_SUBLANES, V7X_LANES), row),
                  pl.BlockSpec((1, N_MOD, d), lambda i: (i // tiles_per_seq, 0, 0)),
                  pl.BlockSpec((1, d), lambda i: (0, 0))],
        out_specs=pl.BlockSpec((tm, d), row),
        compiler_params=_cparams("arbitrary"), name="final_norm",
    )(x, y, mod, g.reshape(1, d))


def kernel(x, c, ctx, c_ctx, w_mod, b_mod, norm_mix, norm_ffn, norm_final, even_w_in, even_w_out, even_rpb,
           even_dw, even_dw_b, even_ln_g, even_ln_b, odd_w_in, odd_w_out, odd_conv, odd_lambda, odd_subln,
           moe_wg, moe_bg, moe_we, moe_be, moe_w1, moe_w3, moe_w2):
    bsz, n_tok, d = x.shape
    ctx_len = ctx.shape[1]
    depth = w_mod.shape[0]
    assert depth == 2 and d == D_MODEL
    xs = x.reshape(bsz * n_tok, d)
    cs = ctx.reshape(bsz * ctx_len, d)

    mod_rows = -(-(bsz + 1) // 8) * 8
    cc = jnp.zeros((mod_rows, d), F32).at[:bsz].set(c).at[bsz].set(c_ctx)
    mod = _modulation(cc, w_mod, b_mod).reshape(depth, mod_rows, N_MOD, d)
    bf = lambda a: a.astype(BF16)
    ew = _expert_weights_bf16
    zero_counts = jnp.zeros((1, V7X_LANES), F32)

    l = 0
    w_in = bf(even_w_in[0])
    w_out = bf(even_w_out[0])
    px = _proj(xs, mod[l], norm_mix[l], w_in, seq=n_tok, mod_row=None, q_seg=0, name="proj_even_x")
    pc = _proj(cs, mod[l], norm_mix[l], w_in, seq=None, mod_row=bsz, q_seg=0, name="proj_even_c")
    a_x = _natten(px, pc, even_rpb[0], bsz=bsz, n_tok=n_tok, ctx_len=ctx_len)
    a_c = _ctxattn(pc, bsz=bsz, ctx_len=ctx_len)
    b_x = _conformer(px, even_dw[0], even_dw_b[0], even_ln_g[0], even_ln_b[0], bsz=bsz, seq=n_tok)
    b_c = _conformer(pc, even_dw[0], even_dw_b[0], even_ln_g[0], even_ln_b[0], bsz=bsz, seq=ctx_len)
    wr, br = _router_weights(moe_wg[l], moe_bg[l], moe_we[l], moe_be[l])
    x1, hx, bkt_x, rnk_x, counts = _oproj(a_x, b_x, xs, mod[l], norm_ffn[l], w_out, wr, br, zero_counts,
                                          seq=n_tok, mod_row=None, name="oproj_even_x")
    c1, hc, bkt_c, rnk_c, counts = _oproj(a_c, b_c, cs, mod[l], norm_ffn[l], w_out, wr, br, counts,
                                          seq=None, mod_row=bsz, name="oproj_even_c")
    y_x, y_c = _moe([hx, hc], [(bkt_x.reshape(-1), rnk_x.reshape(-1)), (bkt_c.reshape(-1), rnk_c.reshape(-1))],
                    counts, ew((moe_w1, moe_w3), l), ew((moe_w2,), l), wr, br)

    l = 1
    lam_init = 0.8 - 0.6 * math.exp(-0.3 * l)
    w_in = bf(odd_w_in[0])
    w_out = bf(odd_w_out[0])
    rope = _rope_tables(n_tok)
    px, x2 = _proj(x1, mod[l], norm_mix[l], w_in, seq=n_tok, mod_row=None, y=y_x, res_mod=mod[l - 1], res_idx=5,
                   rope=rope, rope_segs=(3, 4), q_seg=3, write_x=True, name="proj_odd_x")
    pc = _proj(c1, mod[l], norm_mix[l], w_in[:, 4 * SEG:], seq=None, mod_row=bsz, y=y_c, res_mod=mod[l - 1],
               res_idx=5, name="proj_odd_c")
    s_x = _shortconv(px, odd_conv[0], bsz=bsz, seq=n_tok)
    d_x = _diffattn(px, pc, odd_lambda[0], odd_subln[0], lam_init, bsz=bsz, n_tok=n_tok, ctx_len=ctx_len)
    wr, br = _router_weights(moe_wg[l], moe_bg[l], moe_we[l], moe_be[l])
    x3, hx, bkt_x, rnk_x, counts = _oproj(s_x, d_x, x2, mod[l], norm_ffn[l], w_out, wr, br, zero_counts,
                                          seq=n_tok, mod_row=None, name="oproj_odd_x")
    (y_x,) = _moe([hx], [(bkt_x.reshape(-1), rnk_x.reshape(-1))], counts,
                  ew((moe_w1, moe_w3), l), ew((moe_w2,), l), wr, br)
    out = _final(x3, y_x, mod[l], norm_final, seq=n_tok)
    return out.reshape(bsz, n_tok, d)
```

```python
import functools
import math

import numpy as np
import jax
import jax.numpy as jnp
from jax import lax
from jax.experimental import pallas as pl
from jax.experimental.pallas import tpu as pltpu

F32 = jnp.float32
BF16 = jnp.bfloat16

D_MODEL = 1024
GRID_W = 64
N_MOD = 6
NORM_EPS = 1e-6
NEG_INF = -1e30
ROPE_BASE = 10000.0
SEG = 512
A_HEAD_DIM = 64
A_HEADS = 8
WIN_ROWS_MAX = 8
WIN_COLS = 16
B_CONV = 31
C_CONV = 3
D_QK_DIM = 64
D_V_DIM = 128
D_HEADS = 4
N_GROUPS = 4
EXPERTS_PER_GROUP = 8
N_EXPERTS = 32
D_EXPERT = 256
LOG2E = math.log2(math.e)
QUERY_SCALE = A_HEAD_DIM ** -0.5 * LOG2E
PAIRS_PER_GROUP = EXPERTS_PER_GROUP * (EXPERTS_PER_GROUP - 1) // 2
N_BUCKETS = N_GROUPS * PAIRS_PER_GROUP

V7X_LANES = 128
V7X_SUBLANES = 8
V7X_VMEM_LIMIT_BYTES = 56 * 1024 * 1024

PROJ_TM = 512
ATT_TQ = 256
NAT_TILES_PER_STEP = 4
DIFF_TQ = 1024
NAT_KROWS = 12
CONV_CHUNK = 128
CONV_PAD = 16
MOE_TM = 256
PERM_CHUNK = 1024
ROUTER_GROUP_LANE0 = 0
ROUTER_EXPERT_LANE0 = 8

_PAIR_LO = np.array([i for i in range(8) for j in range(i + 1, 8)], np.int32)
_PAIR_HI = np.array([j for i in range(8) for j in range(i + 1, 8)], np.int32)
_BUCKET_E1 = np.concatenate([g * 8 + _PAIR_LO for g in range(N_GROUPS)]).astype(np.int32)
_BUCKET_E2 = np.concatenate([g * 8 + _PAIR_HI for g in range(N_GROUPS)]).astype(np.int32)


def _cparams(*sem):
    return pltpu.CompilerParams(dimension_semantics=tuple(sem), vmem_limit_bytes=V7X_VMEM_LIMIT_BYTES)


def _dot(a, b):
    return jnp.dot(a, b, preferred_element_type=F32)


def _dot_t(a, b):
    return lax.dot_general(a, b, (((1,), (1,)), ((), ())), preferred_element_type=F32)


def _split_bf16(a):
    hi = a.astype(BF16)
    lo = (a - hi.astype(F32)).astype(BF16)
    return hi, lo


def _dot3(a_hi, a_lo, b_hi, b_lo):
    return _dot(a_hi, b_hi) + _dot(a_lo, b_hi) + _dot(a_hi, b_lo)


def _sigmoid(x):
    return 1.0 / (1.0 + jnp.exp(-x))


def _load_token_tiles(ref):
    tm = ref.shape[0] // V7X_SUBLANES
    return jnp.concatenate([ref[pl.ds(c, tm, stride=V7X_SUBLANES), :] for c in range(V7X_SUBLANES)], axis=1)


def _store_token_tiles(ref, v):
    tm = v.shape[0]
    for c in range(V7X_SUBLANES):
        ref[pl.ds(c, tm, stride=V7X_SUBLANES), :] = v[:, c * V7X_LANES:(c + 1) * V7X_LANES]


def _rmsnorm_mod(x, g, scale, shift):
    y = x * lax.rsqrt(jnp.mean(x * x, axis=-1, keepdims=True) + NORM_EPS)
    return (y * g) * (1.0 + scale) + shift


def _cast_body(*refs):
    *w_refs, o_ref = refs
    n = w_refs[0].shape[-1]
    for j, w_ref in enumerate(w_refs):
        o_ref[:, :, j * n:(j + 1) * n] = w_ref[0].astype(BF16)


def _expert_weights_bf16(ws, layer):
    _, n_exp, k, n = ws[0].shape
    eb = 4
    return pl.pallas_call(
        _cast_body, out_shape=jax.ShapeDtypeStruct((n_exp, k, len(ws) * n), BF16), grid=(n_exp // eb,),
        in_specs=[pl.BlockSpec((1, eb, k, n), lambda e: (layer, e, 0, 0))] * len(ws),
        out_specs=pl.BlockSpec((eb, k, len(ws) * n), lambda e: (e, 0, 0)),
        compiler_params=_cparams("arbitrary"), name="expert_weights_bf16",
    )(*ws)


def _mod_body(c_ref, w_ref, b_ref, o_ref):
    c = c_ref[...]
    s = c * _sigmoid(c)
    s_hi, s_lo = _split_bf16(s)
    w_hi, w_lo = _split_bf16(w_ref[0])
    o_ref[0] = _dot3(s_hi, s_lo, w_hi, w_lo) + b_ref[0]


def _modulation(cc, w_mod, b_mod):
    depth, d, n = w_mod.shape
    rows = cc.shape[0]
    tn = 1536
    return pl.pallas_call(
        _mod_body,
        out_shape=jax.ShapeDtypeStruct((depth, rows, n), F32),
        grid=(depth, n // tn),
        in_specs=[pl.BlockSpec((rows, d), lambda l, j: (0, 0)),
                  pl.BlockSpec((1, d, tn), lambda l, j: (l, 0, j)),
                  pl.BlockSpec((1, 1, tn), lambda l, j: (l, 0, j))],
        out_specs=pl.BlockSpec((1, rows, tn), lambda l, j: (l, 0, j)),
        compiler_params=_cparams("arbitrary", "arbitrary"),
        name="modulation",
    )(cc, w_mod, b_mod.reshape(depth, 1, n))


def _rope(v, cos, sin, lane):
    up = pltpu.roll(v, V7X_LANES - 16, axis=1)
    dn = pltpu.roll(v, 16, axis=1)
    sw = jnp.where((lane & 31) < 16, up, dn)
    return v * cos + sw * sin


def _proj_body(*refs, n_seg, res_idx, shift_idx, rope_segs, q_seg, write_x):
    refs = list(refs)
    x_ref = refs.pop(0)
    y_ref = refs.pop(0) if res_idx is not None else None
    rmod_ref = refs.pop(0) if res_idx is not None else None
    mod_ref = refs.pop(0)
    g_ref = refs.pop(0)
    w_ref = refs.pop(0)
    cos_ref = sin_ref = None
    if rope_segs:
        cos_ref = refs.pop(0)
        sin_ref = refs.pop(0)
    out_ref = refs.pop(0)
    xo_ref = refs.pop(0) if write_x else None

    x = x_ref[...]
    if y_ref is not None:
        x = x + rmod_ref[0, res_idx:res_idx + 1, :] * _load_token_tiles(y_ref)
        if write_x:
            xo_ref[...] = x
    h = _rmsnorm_mod(x, g_ref[...], mod_ref[0, shift_idx + 1:shift_idx + 2, :],
                     mod_ref[0, shift_idx:shift_idx + 1, :]).astype(BF16)
    for s in range(n_seg):
        o = _dot(h, w_ref[:, s * SEG:(s + 1) * SEG])
        if s == q_seg:
            o = o * QUERY_SCALE
        if s in rope_segs:
            lane = lax.broadcasted_iota(jnp.int32, (o.shape[0], V7X_LANES), 1)
            cos = cos_ref[...]
            sin = sin_ref[...]
            o = jnp.concatenate(
                [_rope(o[:, c * V7X_LANES:(c + 1) * V7X_LANES], cos, sin, lane) for c in range(SEG // V7X_LANES)],
                axis=1)
        out_ref[:, s * SEG:(s + 1) * SEG] = o.astype(BF16)


def _proj(x, mod, g, w, *, seq, mod_row, y=None, res_mod=None, res_idx=None, shift_idx=0, rope=None, rope_segs=(),
          q_seg=None, write_x=False, name="proj"):
    t, d = x.shape
    n = w.shape[1]
    tm = min(PROJ_TM, t if seq is None else seq)
    tiles_per_seq = None if seq is None else seq // tm
    if seq is None:
        mod_map = lambda i: (mod_row, 0, 0)
    else:
        mod_map = lambda i: (i // tiles_per_seq, 0, 0)
    args = [x]
    specs = [pl.BlockSpec((tm, d), lambda i: (i, 0))]
    if y is not None:
        args += [y, res_mod]
        specs += [pl.BlockSpec((tm * V7X_SUBLANES, V7X_LANES), lambda i: (i, 0)),
                  pl.BlockSpec((1, N_MOD, d), mod_map)]
    args += [mod, g.reshape(1, d), w]
    specs += [pl.BlockSpec((1, N_MOD, d), mod_map), pl.BlockSpec((1, d), lambda i: (0, 0)),
              pl.BlockSpec((d, n), lambda i: (0, 0))]
    if rope_segs:
        args += [rope[0], rope[1]]
        specs += [pl.BlockSpec((tm, V7X_LANES), lambda i: (i % tiles_per_seq, 0))] * 2
    out_shape = [jax.ShapeDtypeStruct((t, n), BF16)]
    out_specs = [pl.BlockSpec((tm, n), lambda i: (i, 0))]
    if write_x:
        out_shape.append(jax.ShapeDtypeStruct((t, d), F32))
        out_specs.append(pl.BlockSpec((tm, d), lambda i: (i, 0)))
    body = functools.partial(_proj_body, n_seg=n // SEG, res_idx=res_idx if y is not None else None,
                             shift_idx=shift_idx, rope_segs=tuple(rope_segs), q_seg=q_seg, write_x=write_x)
    outs = pl.pallas_call(
        body, out_shape=out_shape, grid=(t // tm,), in_specs=specs, out_specs=out_specs,
        compiler_params=_cparams("arbitrary"), name=name,
    )(*args)
    return outs if write_x else outs[0]


def _rope_tables(n_tok):
    quarter = D_QK_DIM // 4
    t = np.arange(n_tok)
    row = (t // GRID_W).astype(np.float32)
    col = (t % GRID_W).astype(np.float32)
    inv = jnp.power(ROPE_BASE, -jnp.arange(quarter, dtype=F32) / quarter)
    ar = jnp.asarray(row)[:, None] * inv
    ac = jnp.asarray(col)[:, None] * inv
    cos64 = jnp.concatenate([jnp.cos(ar), jnp.cos(ar), jnp.cos(ac), jnp.cos(ac)], axis=-1)
    sin64 = jnp.concatenate([-jnp.sin(ar), jnp.sin(ar), -jnp.sin(ac), jnp.sin(ac)], axis=-1)
    return jnp.tile(cos64, (1, V7X_LANES // D_QK_DIM)), jnp.tile(sin64, (1, V7X_LANES // D_QK_DIM))


def _stack_halves(q):
    lane = lax.broadcasted_iota(jnp.int32, q.shape, 1)
    zero = jnp.zeros_like(q)
    return jnp.concatenate([jnp.where(lane < 64, q, zero), jnp.where(lane >= 64, q, zero)], axis=0)


def _attend(scores, values):
    m = functools.reduce(jnp.maximum, [jnp.max(s, axis=1, keepdims=True) for s in scores])
    acc = None
    for s, v in zip(scores, values):
        p = jnp.exp2(s - m).astype(BF16)
        lane = lax.broadcasted_iota(jnp.int32, v.shape, 1)
        ones_col = jnp.where(lane == 0, 1.0, 0.0).astype(BF16)
        o = _dot(p, jnp.concatenate([v, ones_col], axis=1))
        acc = o if acc is None else acc + o
    return acc[:, :V7X_LANES], acc[:, V7X_LANES:V7X_LANES + 1]


def _natten_body(ks_ref, pid_ref, q_ref, k_ref, v_ref, kc_ref, vc_ref, bias_ref, o_ref):
    nk = NAT_KROWS * GRID_W
    tq = ATT_TQ
    lane = lax.broadcasted_iota(jnp.int32, (tq, V7X_LANES), 1)
    for s in range(q_ref.shape[0] // tq):
        j = pl.program_id(1) * (q_ref.shape[0] // tq) + s
        k0 = pl.multiple_of(ks_ref[j] * GRID_W, GRID_W)
        p = pid_ref[j]
        rows = pl.ds(s * tq, tq)
        for hp in range(A_HEADS // 2):
            cs = slice(hp * V7X_LANES, (hp + 1) * V7X_LANES)
            q2 = _stack_halves(q_ref[rows, cs])
            s_c = _dot_t(q2, kc_ref[:, cs])
            s_l = _dot_t(q2, k_ref[pl.ds(k0, nk), cs])
            bias = jnp.concatenate([bias_ref[p, 2 * hp], bias_ref[p, 2 * hp + 1]], axis=0).astype(F32)
            o2, l = _attend([s_c, s_l + bias], [vc_ref[:, cs], v_ref[pl.ds(k0, nk), cs]])
            o2 = o2 / l
            o_ref[rows, cs] = jnp.where(lane < 64, o2[:tq], o2[tq:]).astype(BF16)


def _natten_tables(rows):
    q_rows = ATT_TQ // GRID_W
    nqb = rows // q_rows
    win_r = min(WIN_ROWS_MAX, rows)
    col = np.arange(GRID_W)
    c_start = np.clip(col - WIN_COLS // 2, 0, GRID_W - WIN_COLS)
    col_ok = (col[None, :] >= c_start[:, None]) & (col[None, :] < c_start[:, None] + WIN_COLS)
    dc = np.clip(col[None, :] - col[:, None], -(WIN_COLS - 1), WIN_COLS - 1) + WIN_COLS - 1
    ks_list, pats, pid = [], [], []
    for j in range(nqb):
        r0 = j * q_rows
        ks = int(np.clip(r0 - win_r // 2, 0, rows - NAT_KROWS))
        qr = r0 + np.arange(q_rows)
        rs = np.clip(qr - win_r // 2, 0, rows - win_r)
        kr = ks + np.arange(NAT_KROWS)
        row_ok = (kr[None, :] >= rs[:, None]) & (kr[None, :] < rs[:, None] + win_r)
        dr = np.clip(kr[None, :] - qr[:, None] + WIN_ROWS_MAX - 1, 0, 2 * WIN_ROWS_MAX - 2)
        key = (row_ok.tobytes(), (dr * row_ok).tobytes())
        for n, (k_, *_rest) in enumerate(pats):
            if k_ == key:
                pid.append(n)
                break
        else:
            pid.append(len(pats))
            pats.append((key, row_ok, dr))
        ks_list.append(ks)
    row_ok = np.stack([p_[1] for p_ in pats])
    dr = np.stack([p_[2] for p_ in pats])
    return np.array(ks_list, np.int32), np.array(pid, np.int32), row_ok, dr, col_ok, dc


def _natten_bias(rpb, row_ok, dr, col_ok, dc):
    n_pat, q_rows, k_rows = dr.shape
    n_dr, n_dc = rpb.shape[1], rpb.shape[2]
    oh_r = np.zeros((n_pat * q_rows * k_rows, n_dr), np.float32)
    oh_r[np.arange(oh_r.shape[0]), dr.reshape(-1)] = 1.0
    oh_c = np.zeros((n_dc, GRID_W * GRID_W), np.float32)
    oh_c[dc.reshape(-1), np.arange(GRID_W * GRID_W)] = 1.0
    hi = lax.Precision.HIGHEST
    t1 = jnp.einsum("mr,hrc->hmc", jnp.asarray(oh_r), rpb, precision=hi)
    t2 = jnp.einsum("hmc,cn->hmn", t1, jnp.asarray(oh_c), precision=hi)
    t2 = t2.reshape(rpb.shape[0], n_pat, q_rows, k_rows, GRID_W, GRID_W)
    valid = row_ok[None, :, :, :, None, None] & col_ok[None, None, None, None, :, :]
    bias = jnp.where(valid, t2 * LOG2E, NEG_INF).transpose(1, 0, 2, 4, 3, 5)
    return bias.reshape(n_pat, rpb.shape[0], q_rows * GRID_W, k_rows * GRID_W).astype(BF16)


def _natten(px, pc, rpb, *, bsz, n_tok, ctx_len):
    rows = n_tok // GRID_W
    assert rows >= NAT_KROWS and n_tok % ATT_TQ == 0
    ks, pid, row_ok, dr, col_ok, dc = _natten_tables(rows)
    bias = _natten_bias(rpb, row_ok, dr, col_ok, dc)
    n_pat = bias.shape[0]
    q_blk = ATT_TQ * NAT_TILES_PER_STEP
    assert n_tok % q_blk == 0
    nqb = n_tok // q_blk
    nk = NAT_KROWS * GRID_W
    grid_spec = pltpu.PrefetchScalarGridSpec(
        num_scalar_prefetch=2, grid=(bsz, nqb),
        in_specs=[pl.BlockSpec((q_blk, SEG), lambda b, j, *_: (b * nqb + j, 0)),
                  pl.BlockSpec((n_tok, SEG), lambda b, j, *_: (b, 1)),
                  pl.BlockSpec((n_tok, SEG), lambda b, j, *_: (b, 2)),
                  pl.BlockSpec((ctx_len, SEG), lambda b, j, *_: (b, 1)),
                  pl.BlockSpec((ctx_len, SEG), lambda b, j, *_: (b, 2)),
                  pl.BlockSpec((n_pat, A_HEADS, ATT_TQ, nk), lambda b, j, *_: (0, 0, 0, 0))],
        out_specs=pl.BlockSpec((q_blk, SEG), lambda b, j, *_: (b * nqb + j, 0)))
    return pl.pallas_call(
        _natten_body, out_shape=jax.ShapeDtypeStruct((bsz * n_tok, SEG), BF16), grid_spec=grid_spec,
        compiler_params=_cparams("arbitrary", "arbitrary"), name="natten",
    )(jnp.asarray(ks), jnp.asarray(pid), px, px, px, pc, pc, bias)


def _ctxattn_body(q_ref, k_ref, v_ref, o_ref):
    tq = q_ref.shape[0]
    lane = lax.broadcasted_iota(jnp.int32, (tq, V7X_LANES), 1)
    for hp in range(A_HEADS // 2):
        cs = slice(hp * V7X_LANES, (hp + 1) * V7X_LANES)
        q2 = _stack_halves(q_ref[:, cs])
        o2, l = _attend([_dot_t(q2, k_ref[:, cs])], [v_ref[:, cs]])
        o2 = o2 / l
        o_ref[:, cs] = jnp.where(lane < 64, o2[:tq], o2[tq:]).astype(BF16)


def _ctxattn(pc, *, bsz, ctx_len):
    return pl.pallas_call(
        _ctxattn_body, out_shape=jax.ShapeDtypeStruct((bsz * ctx_len, SEG), BF16), grid=(bsz,),
        in_specs=[pl.BlockSpec((ctx_len, SEG), lambda b: (b, 0)),
                  pl.BlockSpec((ctx_len, SEG), lambda b: (b, 1)),
                  pl.BlockSpec((ctx_len, SEG), lambda b: (b, 2))],
        out_specs=pl.BlockSpec((ctx_len, SEG), lambda b: (b, 0)),
        compiler_params=_cparams("arbitrary"), name="ctxattn",
    )(pc, pc, pc)


def _conv_accumulate(z_ref, w_ref, t0, taps):
    base = CONV_PAD - taps // 2
    blocks = []
    for c in range(SEG // V7X_LANES):
        cs = slice(c * V7X_LANES, (c + 1) * V7X_LANES)
        win = z_ref[pl.ds(t0, CONV_CHUNK + 2 * CONV_PAD), cs]
        acc = None
        for b in range(V7X_SUBLANES):
            ks = [k for k in range(taps) if (base + k) % V7X_SUBLANES == b]
            if not ks:
                continue
            shifted = win if b == 0 else pltpu.roll(win, win.shape[0] - b, axis=0)
            for k in ks:
                a0 = (base + k) // V7X_SUBLANES * V7X_SUBLANES
                term = shifted[a0:a0 + CONV_CHUNK, :] * w_ref[k:k + 1, cs]
                acc = term if acc is None else acc + term
        blocks.append(acc)
    return jnp.concatenate(blocks, axis=1)


def _conformer_body(u_ref, g_ref, w_ref, b_ref, lg_ref, lb_ref, o_ref, z_ref):
    n = u_ref.shape[0]
    halo = jnp.zeros((CONV_PAD, SEG), F32)
    z_ref[pl.ds(0, CONV_PAD), :] = halo
    z_ref[pl.ds(CONV_PAD + n, CONV_PAD), :] = halo
    z_ref[pl.ds(CONV_PAD, n), :] = u_ref[...].astype(F32) * _sigmoid(g_ref[...].astype(F32))

    def chunk(i, carry):
        t0 = pl.multiple_of(i * CONV_CHUNK, CONV_CHUNK)
        y = _conv_accumulate(z_ref, w_ref, t0, B_CONV) + b_ref[...]
        mu = jnp.mean(y, axis=-1, keepdims=True)
        yc = y - mu
        var = jnp.mean(yc * yc, axis=-1, keepdims=True)
        y = yc * lax.rsqrt(var + NORM_EPS) * lg_ref[...] + lb_ref[...]
        o_ref[pl.ds(t0, CONV_CHUNK), :] = (y * _sigmoid(y)).astype(BF16)
        return carry

    lax.fori_loop(0, n // CONV_CHUNK, chunk, 0)


def _conformer(p, dw, dw_b, ln_g, ln_b, *, bsz, seq):
    vec = lambda a: a.reshape(1, SEG)
    return pl.pallas_call(
        _conformer_body, out_shape=jax.ShapeDtypeStruct((bsz * seq, SEG), BF16), grid=(bsz,),
        in_specs=[pl.BlockSpec((seq, SEG), lambda b: (b, 3)),
                  pl.BlockSpec((seq, SEG), lambda b: (b, 4)),
                  pl.BlockSpec((B_CONV, SEG), lambda b: (0, 0))] + [pl.BlockSpec((1, SEG), lambda b: (0, 0))] * 3,
        out_specs=pl.BlockSpec((seq, SEG), lambda b: (b, 0)),
        scratch_shapes=[pltpu.VMEM((seq + 2 * CONV_PAD, SEG), F32)],
        compiler_params=_cparams("arbitrary"), name="conformer_conv",
    )(p, p, dw, vec(dw_b), vec(ln_g), vec(ln_b))


def _shortconv_body(bg_ref, cg_ref, u_ref, w_ref, o_ref, z_ref):
    n = u_ref.shape[0]
    halo = jnp.zeros((CONV_PAD, SEG), F32)
    z_ref[pl.ds(0, CONV_PAD), :] = halo
    z_ref[pl.ds(CONV_PAD + n, CONV_PAD), :] = halo
    z_ref[pl.ds(CONV_PAD, n), :] = cg_ref[...].astype(F32) * u_ref[...].astype(F32)

    def chunk(i, carry):
        t0 = pl.multiple_of(i * CONV_CHUNK, CONV_CHUNK)
        y = _conv_accumulate(z_ref, w_ref, t0, C_CONV)
        o_ref[pl.ds(t0, CONV_CHUNK), :] = (bg_ref[pl.ds(t0, CONV_CHUNK), :].astype(F32) * y).astype(BF16)
        return carry

    lax.fori_loop(0, n // CONV_CHUNK, chunk, 0)


def _shortconv(p, conv_w, *, bsz, seq):
    return pl.pallas_call(
        _shortconv_body, out_shape=jax.ShapeDtypeStruct((bsz * seq, SEG), BF16), grid=(bsz,),
        in_specs=[pl.BlockSpec((seq, SEG), lambda b: (b, 0)),
                  pl.BlockSpec((seq, SEG), lambda b: (b, 1)),
                  pl.BlockSpec((seq, SEG), lambda b: (b, 2)),
                  pl.BlockSpec((C_CONV, SEG), lambda b: (0, 0))],
        out_specs=pl.BlockSpec((seq, SEG), lambda b: (b, 0)),
        scratch_shapes=[pltpu.VMEM((seq + 2 * CONV_PAD, SEG), F32)],
        compiler_params=_cparams("arbitrary"), name="short_conv",
    )(p, p, p, conv_w)


def _diffattn_body(q_ref, k_ref, v_ref, kc_ref, vc_ref, lam_ref, g_ref, o_ref, *, lam_init):
    tq = ATT_TQ
    lp = lam_ref[...]
    lam = (jnp.exp(jnp.sum(lp[0:1, :] * lp[1:2, :], axis=1, keepdims=True))
           - jnp.exp(jnp.sum(lp[2:3, :] * lp[3:4, :], axis=1, keepdims=True)) + lam_init)
    for s in range(q_ref.shape[0] // tq):
        rows = pl.ds(s * tq, tq)
        for h in range(D_HEADS):
            cs = slice(h * V7X_LANES, (h + 1) * V7X_LANES)
            q2 = _stack_halves(q_ref[rows, cs])
            o2, l = _attend([_dot_t(q2, kc_ref[:, cs]), _dot_t(q2, k_ref[:, cs])], [vc_ref[:, cs], v_ref[:, cs]])
            o2 = o2 / l
            o = o2[:tq] - lam * o2[tq:]
            o = o * lax.rsqrt(jnp.mean(o * o, axis=-1, keepdims=True) + NORM_EPS) * g_ref[...]
            o_ref[rows, cs] = (o * (1.0 - lam_init)).astype(BF16)


def _diffattn(px, pc, lam_p, subln, lam_init, *, bsz, n_tok, ctx_len):
    nq = n_tok // DIFF_TQ
    body = functools.partial(_diffattn_body, lam_init=lam_init)
    return pl.pallas_call(
        body, out_shape=jax.ShapeDtypeStruct((bsz * n_tok, SEG), BF16), grid=(bsz, nq),
        in_specs=[pl.BlockSpec((DIFF_TQ, SEG), lambda b, j: (b * nq + j, 3)),
                  pl.BlockSpec((n_tok, SEG), lambda b, j: (b, 4)),
                  pl.BlockSpec((n_tok, SEG), lambda b, j: (b, 5)),
                  pl.BlockSpec((ctx_len, SEG), lambda b, j: (b, 0)),
                  pl.BlockSpec((ctx_len, SEG), lambda b, j: (b, 1)),
                  pl.BlockSpec((4, D_QK_DIM), lambda b, j: (0, 0)),
                  pl.BlockSpec((1, D_V_DIM), lambda b, j: (0, 0))],
        out_specs=pl.BlockSpec((DIFF_TQ, SEG), lambda b, j: (b * nq + j, 0)),
        compiler_params=_cparams("arbitrary", "arbitrary"), name="diff_attn",
    )(px, px, px, pc, pc, lam_p, subln.reshape(1, D_V_DIM))


def _oproj_body(a_ref, b_ref, x_ref, mod_ref, g_ref, wo_ref, wr_ref, br_ref, cin_ref,
                x1_ref, h_ref, bucket_ref, rank_ref, cout_ref, carry_ref, before_ref):
    i = pl.program_id(0)
    tm = x_ref.shape[0]

    @pl.when(i == 0)
    def _():
        carry_ref[...] = cin_ref[...]
        r_i = lax.broadcasted_iota(jnp.int32, (tm, tm), 0)
        c_i = lax.broadcasted_iota(jnp.int32, (tm, tm), 1)
        before_ref[...] = jnp.where(r_i > c_i, 1.0, 0.0).astype(BF16)

    y = _dot(a_ref[...], wo_ref[0:SEG, :]) + _dot(b_ref[...], wo_ref[SEG:2 * SEG, :])
    x1 = x_ref[...] + mod_ref[0, 2:3, :] * y
    x1_ref[...] = x1
    h = _rmsnorm_mod(x1, g_ref[...], mod_ref[0, 4:5, :], mod_ref[0, 3:4, :])
    _store_token_tiles(h_ref, h)
    logits = _dot(h.astype(BF16), wr_ref[...]) + br_ref[...]
    lane = lax.broadcasted_iota(jnp.int32, (tm, V7X_LANES), 1)

    def first_max(v):
        m = jnp.max(v, axis=1, keepdims=True)
        return jnp.min(jnp.where(v == m, lane, jnp.int32(V7X_LANES)), axis=1, keepdims=True)

    g_sel = first_max(jnp.where(lane < N_GROUPS, logits, -jnp.inf))
    e_lane0 = ROUTER_EXPERT_LANE0 + EXPERTS_PER_GROUP * g_sel
    in_group = (lane >= e_lane0) & (lane < e_lane0 + EXPERTS_PER_GROUP)
    el = jnp.where(in_group, logits, -jnp.inf)
    i1 = first_max(el)
    i2 = first_max(jnp.where(lane == i1, -jnp.inf, el))
    lo = jnp.minimum(i1, i2) - e_lane0
    hi = jnp.maximum(i1, i2) - e_lane0
    bucket = g_sel * PAIRS_PER_GROUP + ((lo * (2 * EXPERTS_PER_GROUP - 1 - lo)) >> 1) + (hi - lo - 1)
    oh = jnp.where(lane == bucket, 1.0, 0.0)
    oh16 = oh.astype(BF16)
    cum = _dot(before_ref[...], oh16) + carry_ref[...]
    carry_ref[...] = carry_ref[...] + jnp.sum(oh, axis=0, keepdims=True)
    rk = oh * cum
    rk_hi = jnp.floor(rk * (1.0 / 256.0))
    rk_lo = rk - 256.0 * rk_hi
    sub = lax.broadcasted_iota(jnp.int32, (V7X_SUBLANES, V7X_LANES), 0)
    lane_id = lax.broadcasted_iota(jnp.int32, (V7X_SUBLANES, V7X_LANES), 1)
    pick_lane = jnp.where(sub == 0, lane_id, 0).astype(F32).astype(BF16)
    ones_row = jnp.where(sub == 0, 1.0, 0.0).astype(BF16)
    bucket_ref[...] = _dot_t(pick_lane, oh16)[0:1, :].astype(jnp.int32)
    rank_row = 256.0 * _dot_t(ones_row, rk_hi.astype(BF16)) + _dot_t(ones_row, rk_lo.astype(BF16))
    rank_ref[...] = rank_row[0:1, :].astype(jnp.int32)

    @pl.when(i == pl.num_programs(0) - 1)
    def _():
        cout_ref[...] = carry_ref[...]


def _oproj(a, b, x, mod, g, wo, wr, br, counts_in, *, seq, mod_row, name="oproj"):
    t, d = x.shape
    tm = min(PROJ_TM, t if seq is None else seq)
    if seq is None:
        mod_map = lambda i: (mod_row, 0, 0)
    else:
        tiles_per_seq = seq // tm
        mod_map = lambda i: (i // tiles_per_seq, 0, 0)
    row = lambda i: (i, 0)
    const = lambda i: (0, 0)
    return pl.pallas_call(
        _oproj_body,
        out_shape=[jax.ShapeDtypeStruct((t, d), F32), jax.ShapeDtypeStruct((t * V7X_SUBLANES, V7X_LANES), F32),
                   jax.ShapeDtypeStruct((1, t), jnp.int32), jax.ShapeDtypeStruct((1, t), jnp.int32),
                   jax.ShapeDtypeStruct((1, V7X_LANES), F32)],
        grid=(t // tm,),
        in_specs=[pl.BlockSpec((tm, SEG), row), pl.BlockSpec((tm, SEG), row), pl.BlockSpec((tm, d), row),
                  pl.BlockSpec((1, N_MOD, d), mod_map), pl.BlockSpec((1, d), const),
                  pl.BlockSpec((2 * SEG, d), const), pl.BlockSpec((d, V7X_LANES), const),
                  pl.BlockSpec((1, V7X_LANES), const), pl.BlockSpec((1, V7X_LANES), const)],
        out_specs=[pl.BlockSpec((tm, d), row), pl.BlockSpec((tm * V7X_SUBLANES, V7X_LANES), row),
                   pl.BlockSpec((1, tm), lambda i: (0, i)), pl.BlockSpec((1, tm), lambda i: (0, i)),
                   pl.BlockSpec((1, V7X_LANES), const)],
        scratch_shapes=[pltpu.VMEM((1, V7X_LANES), F32), pltpu.VMEM((tm, tm), BF16)],
        compiler_params=_cparams("arbitrary"), name=name,
    )(a, b, x, mod, g.reshape(1, d), wo, wr, br, counts_in)


def _router_weights(wg, bg, we, be):
    d = wg.shape[0]
    w = jnp.zeros((d, V7X_LANES), F32)
    w = w.at[:, ROUTER_GROUP_LANE0:ROUTER_GROUP_LANE0 + N_GROUPS].set(wg)
    w = w.at[:, ROUTER_EXPERT_LANE0:ROUTER_EXPERT_LANE0 + N_EXPERTS].set(we)
    b = jnp.zeros((1, V7X_LANES), F32)
    b = b.at[0, ROUTER_GROUP_LANE0:ROUTER_GROUP_LANE0 + N_GROUPS].set(bg)
    b = b.at[0, ROUTER_EXPERT_LANE0:ROUTER_EXPERT_LANE0 + N_EXPERTS].set(be)
    return w.astype(BF16), b


def _token_tile(ref, t):
    return ref.at[pl.ds(pl.multiple_of(t * V7X_SUBLANES, V7X_SUBLANES), V7X_SUBLANES)]


def _scatter_body(off_ref, bucket_ref, rank_ref, rows_ref, sorted_hbm, out_hbm, sem, *, chunk):
    del sorted_hbm

    def start(t, carry):
        p = off_ref[bucket_ref[t]] + rank_ref[t]
        pltpu.make_async_copy(_token_tile(rows_ref, t), _token_tile(out_hbm, p), sem).start()
        return carry

    lax.fori_loop(0, chunk, start, 0, unroll=8)
    pltpu.make_async_copy(rows_ref, out_hbm.at[pl.ds(0, chunk * V7X_SUBLANES)], sem).wait()


def _gather_body(off_ref, bucket_ref, rank_ref, sorted_hbm, out_ref, sem, *, chunk):
    def start(t, carry):
        p = off_ref[bucket_ref[t]] + rank_ref[t]
        pltpu.make_async_copy(_token_tile(sorted_hbm, p), _token_tile(out_ref, t), sem).start()
        return carry

    lax.fori_loop(0, chunk, start, 0, unroll=8)
    pltpu.make_async_copy(sorted_hbm.at[pl.ds(0, chunk * V7X_SUBLANES)], out_ref, sem).wait()


def _permute(offsets, bucket, rank, rows, sorted_rows, *, n_rows, scatter, name):
    chunk = min(PERM_CHUNK, n_rows)
    smem_blk = pl.BlockSpec((chunk,), lambda i, *_: (i,), memory_space=pltpu.SMEM)
    any_spec = pl.BlockSpec(memory_space=pl.ANY)
    vmem_blk = pl.BlockSpec((chunk * V7X_SUBLANES, V7X_LANES), lambda i, *_: (i, 0))
    if scatter:
        body = functools.partial(_scatter_body, chunk=chunk)
        args = (offsets, bucket, rank, rows, sorted_rows)
        in_specs = [smem_blk, smem_blk, vmem_blk, any_spec]
        out_shape = jax.ShapeDtypeStruct(sorted_rows.shape, sorted_rows.dtype)
        out_specs = any_spec
        aliases = {4: 0}
    else:
        body = functools.partial(_gather_body, chunk=chunk)
        args = (offsets, bucket, rank, sorted_rows)
        in_specs = [smem_blk, smem_blk, any_spec]
        out_shape = jax.ShapeDtypeStruct((n_rows * V7X_SUBLANES, V7X_LANES), sorted_rows.dtype)
        out_specs = vmem_blk
        aliases = {}
    grid_spec = pltpu.PrefetchScalarGridSpec(
        num_scalar_prefetch=1, grid=(n_rows // chunk,), in_specs=in_specs, out_specs=out_specs,
        scratch_shapes=[pltpu.SemaphoreType.DMA])
    return pl.pallas_call(
        body, out_shape=out_shape, grid_spec=grid_spec, input_output_aliases=aliases,
        compiler_params=pltpu.CompilerParams(dimension_semantics=("arbitrary",), has_side_effects=True,
                                             vmem_limit_bytes=V7X_VMEM_LIMIT_BYTES),
        name=name,
    )(*args)


def _moe_body(te1_ref, te2_ref, tval_ref, trow_ref, x_ref, w13a_ref, w13b_ref, w2a_ref, w2b_ref,
              wr_ref, br_ref, y_ref):
    i = pl.program_id(0)

    @pl.when(tval_ref[i] == 0)
    def _():
        y_ref[...] = jnp.zeros_like(y_ref)

    @pl.when(tval_ref[i] > 0)
    def _():
        tm = x_ref.shape[0] // V7X_SUBLANES
        x = _load_token_tiles(x_ref).astype(BF16)
        e1 = te1_ref[i]
        e2 = te2_ref[i]
        grp = e1 >> 3
        logits = _dot(x, wr_ref[...]) + br_ref[...]
        lane = lax.broadcasted_iota(jnp.int32, (tm, V7X_LANES), 1)
        pick = lambda idx: jnp.sum(jnp.where(lane == idx, logits, 0.0), axis=1, keepdims=True)
        gl = jnp.where(lane < N_GROUPS, logits, -jnp.inf)
        gm = jnp.max(gl, axis=1, keepdims=True)
        gz = jnp.sum(jnp.exp(gl - gm), axis=1, keepdims=True)
        g_w = jnp.exp(pick(ROUTER_GROUP_LANE0 + grp) - gm) / gz
        l1 = pick(ROUTER_EXPERT_LANE0 + e1)
        l2 = pick(ROUTER_EXPERT_LANE0 + e2)
        m = jnp.maximum(l1, l2)
        p1 = jnp.exp(l1 - m)
        p2 = jnp.exp(l2 - m)
        c1 = g_w * p1 / (p1 + p2)
        c2 = g_w * p2 / (p1 + p2)

        def expert(w13_ref, w2_ref):
            hg = _dot(x, w13_ref[0])
            h = hg[:, :D_EXPERT]
            hid = (h * _sigmoid(h)) * hg[:, D_EXPERT:]
            return _dot(hid.astype(BF16), w2_ref[0])

        _store_token_tiles(y_ref, c1 * expert(w13a_ref, w2a_ref) + c2 * expert(w13b_ref, w2b_ref))


def _moe_sorted(xs, te1, te2, tval, trow, w13, w2, wr_hi, br):
    d = w13.shape[1]
    n_tiles = xs.shape[0] // (MOE_TM * V7X_SUBLANES)
    tile_blk = (MOE_TM * V7X_SUBLANES, V7X_LANES)
    xmap = lambda i, te1, te2, tval, trow: (trow[i], 0)
    wa = lambda i, te1, te2, tval, trow: (te1[i], 0, 0)
    wb = lambda i, te1, te2, tval, trow: (te2[i], 0, 0)
    const = lambda i, *_: (0, 0)
    grid_spec = pltpu.PrefetchScalarGridSpec(
        num_scalar_prefetch=4, grid=(n_tiles,),
        in_specs=[pl.BlockSpec(tile_blk, xmap),
                  pl.BlockSpec((1, d, 2 * D_EXPERT), wa), pl.BlockSpec((1, d, 2 * D_EXPERT), wb),
                  pl.BlockSpec((1, D_EXPERT, d), wa), pl.BlockSpec((1, D_EXPERT, d), wb),
                  pl.BlockSpec((d, V7X_LANES), const), pl.BlockSpec((1, V7X_LANES), const)],
        out_specs=pl.BlockSpec(tile_blk, lambda i, *_: (i, 0)))
    return pl.pallas_call(
        _moe_body, out_shape=jax.ShapeDtypeStruct(xs.shape, F32), grid_spec=grid_spec,
        compiler_params=_cparams("arbitrary"), name="moe_experts",
    )(te1, te2, tval, trow, xs, w13, w13, w2, w2, wr_hi, br)


def _moe(h_parts, route_parts, counts, w13, w2, wr_hi, br):
    n_rows = [h.shape[0] // V7X_SUBLANES for h in h_parts]
    n_tiles = sum(n_rows) // MOE_TM + N_BUCKETS
    cnt = counts[0, :N_BUCKETS].astype(jnp.int32)
    tiles_b = (cnt + MOE_TM - 1) // MOE_TM
    tile_end = jnp.cumsum(tiles_b)
    n_valid = tile_end[-1]
    offsets = jnp.zeros((V7X_LANES,), jnp.int32).at[:N_BUCKETS].set((tile_end - tiles_b) * MOE_TM)
    tile_id = jnp.arange(n_tiles, dtype=jnp.int32)
    tval = (tile_id < n_valid).astype(jnp.int32)
    trow = jnp.minimum(tile_id, n_valid - 1)
    tbucket = jnp.sum((trow[:, None] >= tile_end[None, :]).astype(jnp.int32), axis=1)
    tbucket = jnp.minimum(tbucket, N_BUCKETS - 1)
    te1 = jnp.asarray(_BUCKET_E1)[tbucket]
    te2 = jnp.asarray(_BUCKET_E2)[tbucket]

    xs = jnp.zeros((n_tiles * MOE_TM * V7X_SUBLANES, V7X_LANES), F32)
    for h, n, (bucket, rank) in zip(h_parts, n_rows, route_parts):
        xs = _permute(offsets, bucket, rank, h, xs, n_rows=n, scatter=True, name="moe_scatter_rows")
    ys = _moe_sorted(xs, te1, te2, tval, trow, w13, w2, wr_hi, br)
    return [_permute(offsets, bucket, rank, None, ys, n_rows=n, scatter=False, name="moe_gather_rows")
            for n, (bucket, rank) in zip(n_rows, route_parts)]


def _final_body(x_ref, y_ref, mod_ref, g_ref, o_ref):
    x = x_ref[...] + mod_ref[0, 5:6, :] * _load_token_tiles(y_ref)
    o_ref[...] = x * lax.rsqrt(jnp.mean(x * x, axis=-1, keepdims=True) + NORM_EPS) * g_ref[...]


def _final(x, y, mod, g, *, seq):
    t, d = x.shape
    tm = min(PROJ_TM, seq)
    tiles_per_seq = seq // tm
    row = lambda i: (i, 0)
    return pl.pallas_call(
        _final_body, out_shape=jax.ShapeDtypeStruct((t, d), F32), grid=(t // tm,),
        in_specs=[pl.BlockSpec((tm, d), row), pl.BlockSpec((tm * V7X_SUBLANES, V7X_LANES), row),
                  pl.BlockSpec((1, N_MOD, d), lambda i: (i // tiles_per_seq, 0, 0)),
                  pl.BlockSpec((1, d), lambda i: (0, 0))],
        out_specs=pl.BlockSpec((tm, d), row),
        compiler_params=_cparams("arbitrary"), name="final_norm",
    )(x, y, mod, g.reshape(1, d))


def kernel(x, c, ctx, c_ctx, w_mod, b_mod, norm_mix, norm_ffn, norm_final, even_w_in, even_w_out, even_rpb,
           even_dw, even_dw_b, even_ln_g, even_ln_b, odd_w_in, odd_w_out, odd_conv, odd_lambda, odd_subln,
           moe_wg, moe_bg, moe_we, moe_be, moe_w1, moe_w3, moe_w2):
    bsz, n_tok, d = x.shape
    ctx_len = ctx.shape[1]
    depth = w_mod.shape[0]
    assert depth == 2 and d == D_MODEL
    xs = x.reshape(bsz * n_tok, d)
    cs = ctx.reshape(bsz * ctx_len, d)

    mod_rows = -(-(bsz + 1) // 8) * 8
    cc = jnp.zeros((mod_rows, d), F32).at[:bsz].set(c).at[bsz].set(c_ctx)
    mod = _modulation(cc, w_mod, b_mod).reshape(depth, mod_rows, N_MOD, d)
    bf = lambda a: a.astype(BF16)
    ew = _expert_weights_bf16
    zero_counts = jnp.zeros((1, V7X_LANES), F32)

    l = 0
    w_in = bf(even_w_in[0])
    w_out = bf(even_w_out[0])
    px = _proj(xs, mod[l], norm_mix[l], w_in, seq=n_tok, mod_row=None, q_seg=0, name="proj_even_x")
    pc = _proj(cs, mod[l], norm_mix[l], w_in, seq=None, mod_row=bsz, q_seg=0, name="proj_even_c")
    a_x = _natten(px, pc, even_rpb[0], bsz=bsz, n_tok=n_tok, ctx_len=ctx_len)
    a_c = _ctxattn(pc, bsz=bsz, ctx_len=ctx_len)
    b_x = _conformer(px, even_dw[0], even_dw_b[0], even_ln_g[0], even_ln_b[0], bsz=bsz, seq=n_tok)
    b_c = _conformer(pc, even_dw[0], even_dw_b[0], even_ln_g[0], even_ln_b[0], bsz=bsz, seq=ctx_len)
    wr, br = _router_weights(moe_wg[l], moe_bg[l], moe_we[l], moe_be[l])
    x1, hx, bkt_x, rnk_x, counts = _oproj(a_x, b_x, xs, mod[l], norm_ffn[l], w_out, wr, br, zero_counts,
                                          seq=n_tok, mod_row=None, name="oproj_even_x")
    c1, hc, bkt_c, rnk_c, counts = _oproj(a_c, b_c, cs, mod[l], norm_ffn[l], w_out, wr, br, counts,
                                          seq=None, mod_row=bsz, name="oproj_even_c")
    y_x, y_c = _moe([hx, hc], [(bkt_x.reshape(-1), rnk_x.reshape(-1)), (bkt_c.reshape(-1), rnk_c.reshape(-1))],
                    counts, ew((moe_w1, moe_w3), l), ew((moe_w2,), l), wr, br)

    l = 1
    lam_init = 0.8 - 0.6 * math.exp(-0.3 * l)
    w_in = bf(odd_w_in[0])
    w_out = bf(odd_w_out[0])
    rope = _rope_tables(n_tok)
    px, x2 = _proj(x1, mod[l], norm_mix[l], w_in, seq=n_tok, mod_row=None, y=y_x, res_mod=mod[l - 1], res_idx=5,
                   rope=rope, rope_segs=(3, 4), q_seg=3, write_x=True, name="proj_odd_x")
    pc = _proj(c1, mod[l], norm_mix[l], w_in[:, 4 * SEG:], seq=None, mod_row=bsz, y=y_c, res_mod=mod[l - 1],
               res_idx=5, name="proj_odd_c")
    s_x = _shortconv(px, odd_conv[0], bsz=bsz, seq=n_tok)
    d_x = _diffattn(px, pc, odd_lambda[0], odd_subln[0], lam_init, bsz=bsz, n_tok=n_tok, ctx_len=ctx_len)
    wr, br = _router_weights(moe_wg[l], moe_bg[l], moe_we[l], moe_be[l])
    x3, hx, bkt_x, rnk_x, counts = _oproj(s_x, d_x, x2, mod[l], norm_ffn[l], w_out, wr, br, zero_counts,
                                          seq=n_tok, mod_row=None, name="oproj_odd_x")
    (y_x,) = _moe([hx], [(bkt_x.reshape(-1), rnk_x.reshape(-1))], counts,
                  ew((moe_w1, moe_w3), l), ew((moe_w2,), l), wr, br)
    out = _final(x3, y_x, mod[l], norm_final, seq=n_tok)
    return out.reshape(bsz, n_tok, d)
```
